```python
import math
import jax, jax.numpy as jnp
from jax import lax
import numpy as np

D_MODEL = 1024
BATCH = 4
SEQ = 8192
DEPTH = 2
DEC_BATCH = 8
DEC_SEQ = 16
PAST_LEN = 1024

CHUNK = 64
Q_BLOCK = 128
H_A = 4
HD_A = 128
DV_A = 2 * HD_A
W_A = H_A * DV_A
H_B = 8
HD_B = 128
W_B = H_B * HD_B
CONV_W = 4
H_C = 8
HD_C = 128
W_C = H_C * HD_C
D_FF = 4 * D_MODEL
ROPE_THETA = 10000.0
EPS = 1e-6
IN_SIZES = (2 * H_A * HD_A, 2 * H_A * HD_A, H_A * DV_A, 3 * W_B, H_B, H_B, W_B, W_C, W_C, W_C, 3 * D_MODEL)
N_IN = 2 * H_A * HD_A + 2 * H_A * HD_A + H_A * DV_A + 3 * W_B + 2 * H_B + W_B + 3 * W_C + 3 * D_MODEL

kernel_name = 'hybrid_diff_gdn_stickbreak_stream_step'


def _split_points():
    pts, acc = [], 0
    for s in IN_SIZES[:-1]:
        acc += s
        pts.append(acc)
    return pts


def rmsnorm(x, w):
    xf = x.astype(jnp.float32)
    y = xf * lax.rsqrt(jnp.mean(xf * xf, axis=-1, keepdims=True) + EPS)
    return (y * w.astype(jnp.float32)).astype(x.dtype)


def l2norm(x):
    xf = x.astype(jnp.float32)
    return xf * lax.rsqrt(jnp.sum(xf * xf, axis=-1, keepdims=True) + EPS)


def rope(x, pos):
    half = x.shape[-1] // 2
    inv = ROPE_THETA ** (-jnp.arange(half, dtype=jnp.float32) / half)
    ang = pos.astype(jnp.float32)[:, None] * inv[None, :]
    cos = jnp.cos(ang)[None, :, None, :]
    sin = jnp.sin(ang)[None, :, None, :]
    xf = x.astype(jnp.float32)
    x1, x2 = xf[..., :half], xf[..., half:]
    return jnp.concatenate([x1 * cos - x2 * sin, x2 * cos + x1 * sin], axis=-1).astype(x.dtype)


def sweep_query_blocks(fn, q, qpos):
    B, L = q.shape[0], q.shape[1]
    nb = L // Q_BLOCK
    qb = jnp.moveaxis(q.reshape((B, nb, Q_BLOCK) + q.shape[2:]), 1, 0)
    pb = qpos.reshape(nb, Q_BLOCK)
    out = lax.map(lambda a: fn(a[0], a[1]), (qb, pb))
    out = jnp.moveaxis(out, 0, 1)
    return out.reshape((B, L) + out.shape[3:])


def diff_attn_block(q, k, v, qpos, kpos, lam):
    B, Lq = q.shape[0], q.shape[1]
    qh = q.reshape(B, Lq, H_A, 2, HD_A)
    kh = k.reshape(B, k.shape[1], H_A, 2, HD_A)
    s = jnp.einsum('bqhcd,bkhcd->bhcqk', qh, kh).astype(jnp.float32) * (HD_A ** -0.5)
    mask = (kpos[None, :] // CHUNK) <= (qpos[:, None] // CHUNK)
    s = jnp.where(mask, s, -jnp.inf)
    p = jax.nn.softmax(s, axis=-1)
    a = p[:, :, 0] - lam * p[:, :, 1]
    o = jnp.einsum('bhqk,bkhe->bqhe', a, v.astype(jnp.float32))
    return o.astype(v.dtype)


def stick_breaking_block(q, k, v, qpos, kpos):
    z = jnp.einsum('bqhd,bkhd->bhqk', q, k).astype(jnp.float32) * (HD_C ** -0.5)
    mask = kpos[None, :] < qpos[:, None]
    log_beta = jax.nn.log_sigmoid(z)
    log_1m = jnp.where(mask, jax.nn.log_sigmoid(-z), 0.0)
    tail = lax.cumsum(log_1m, axis=3, reverse=True) - log_1m
    a = jnp.where(mask, jnp.exp(log_beta + tail), 0.0)
    o = jnp.einsum('bhqk,bkhd->bqhd', a, v.astype(jnp.float32))
    return o.astype(v.dtype)


def gated_delta_chunk(S, q, k, v, g, beta):
    f32 = jnp.float32
    S = S.astype(f32)
    q, k, v = (jnp.swapaxes(t.astype(f32), 1, 2) for t in (q, k, v))
    g, beta = (jnp.swapaxes(t.astype(f32), 1, 2) for t in (g, beta))
    L = q.shape[2]
    gc = jnp.cumsum(g, axis=-1)
    tri = jnp.tril(jnp.ones((L, L), dtype=bool))
    strict = jnp.tril(jnp.ones((L, L), dtype=bool), -1)
    decay = jnp.exp(jnp.where(tri, gc[..., :, None] - gc[..., None, :], -jnp.inf))
    kb = k * beta[..., None]
    M = jnp.where(strict, jnp.einsum('bhid,bhjd->bhij', kb, k) * decay, 0.0)
    eye = jnp.eye(L, dtype=f32)
    T = lax.linalg.triangular_solve(eye + M, jnp.broadcast_to(eye, M.shape),
                                    left_side=True, lower=True, unit_diagonal=True)
    u0 = jnp.einsum('bhij,bhje->bhie', T, v * beta[..., None])
    w = jnp.einsum('bhij,bhjd->bhid', T, kb * jnp.exp(gc)[..., None])
    u = u0 - jnp.einsum('bhld,bhde->bhle', w, S)
    o = (jnp.einsum('bhld,bhde->bhle', q * jnp.exp(gc)[..., None], S)
         + jnp.einsum('bhij,bhje->bhie', jnp.einsum('bhid,bhjd->bhij', q, k) * decay, u))
    gL = gc[..., -1]
    S_new = (S * jnp.exp(gL)[..., None, None]
             + jnp.einsum('bhjd,bhje->bhde', k * jnp.exp(gL[..., None] - gc)[..., None], u))
    return S_new, jnp.swapaxes(o, 1, 2)


def gated_delta_scan(S0, q, k, v, g, beta):
    B, L = q.shape[0], q.shape[1]
    nc = L // CHUNK

    def resh(t):
        return jnp.moveaxis(t.reshape((B, nc, CHUNK) + t.shape[2:]), 1, 0)

    def step(S, xs):
        return gated_delta_chunk(S, *xs)

    S, o = lax.scan(step, S0, (resh(q), resh(k), resh(v), resh(g), resh(beta)))
    o = jnp.moveaxis(o, 0, 1)
    return S, o.reshape((B, L) + o.shape[3:])


def hybrid_layer(x, pos, past_ak, past_av, past_ck, past_cv, conv_hist, s0,
                 norm1_w, w_in, lam_q1, lam_k1, lam_q2, lam_k2, subln_w,
                 conv_w, a_log, dt_bias, gdn_norm_w, w_out, norm2_w, w_up, w_down,
                 layer_idx, prompt):
    B, L, _ = x.shape
    h = rmsnorm(x, norm1_w)
    proj = jnp.einsum('bld,dn->bln', h, w_in)
    aq, ak, av, bqkv, ba, bb, bz, cq, ck, cv, gates = jnp.split(proj, _split_points(), axis=-1)

    aq = rope(aq.reshape(B, L, 2 * H_A, HD_A), pos)
    ak = rope(ak.reshape(B, L, 2 * H_A, HD_A), pos)
    av = av.reshape(B, L, H_A, DV_A)
    if prompt:
        ak_all, av_all = ak, av
    else:
        ak_all = jnp.concatenate([past_ak.astype(ak.dtype), ak], axis=1)
        av_all = jnp.concatenate([past_av.astype(av.dtype), av], axis=1)
    kpos_a = jnp.arange(ak_all.shape[1])
    lam_init = 0.8 - 0.6 * math.exp(-0.3 * layer_idx)
    lam = (jnp.exp(jnp.sum(lam_q1.astype(jnp.float32) * lam_k1.astype(jnp.float32)))
           - jnp.exp(jnp.sum(lam_q2.astype(jnp.float32) * lam_k2.astype(jnp.float32))) + lam_init)
    fa = lambda qb, qp: diff_attn_block(qb, ak_all, av_all, qp, kpos_a, lam)
    oa = sweep_query_blocks(fa, aq, pos) if prompt else fa(aq, pos)
    oa = (rmsnorm(oa, subln_w) * (1.0 - lam_init)).reshape(B, L, W_A)

    hist = jnp.zeros((B, CONV_W - 1, 3 * W_B), bqkv.dtype) if prompt else conv_hist.astype(bqkv.dtype)
    xpad = jnp.concatenate([hist, bqkv], axis=1)
    conv = xpad[:, 0:L] * conv_w[0]
    for i in range(1, CONV_W):
        conv = conv + xpad[:, i:i + L] * conv_w[i]
    new_conv = xpad[:, L:]
    qkv = jax.nn.silu(conv)
    bq, bk, bv = jnp.split(qkv, [W_B, 2 * W_B], axis=-1)
    bq = l2norm(bq.reshape(B, L, H_B, HD_B)) * (HD_B ** -0.5)
    bk = l2norm(bk.reshape(B, L, H_B, HD_B))
    bv = bv.reshape(B, L, H_B, HD_B)
    g = -jnp.exp(a_log.astype(jnp.float32)) * jax.nn.softplus(ba.astype(jnp.float32) + dt_bias.astype(jnp.float32))
    beta = jax.nn.sigmoid(bb.astype(jnp.float32))
    if prompt:
        S_init = jnp.zeros((B, H_B, HD_B, HD_B), jnp.float32)
        s_new, ob = gated_delta_scan(S_init, bq, bk, bv, g, beta)
    else:
        s_new, ob = gated_delta_chunk(s0, bq, bk, bv, g, beta)
    ob = rmsnorm(ob.astype(x.dtype), gdn_norm_w) * jax.nn.silu(bz.reshape(B, L, H_B, HD_B))
    ob = ob.reshape(B, L, W_B)

    cq = cq.reshape(B, L, H_C, HD_C)
    ck = ck.reshape(B, L, H_C, HD_C)
    cv = cv.reshape(B, L, H_C, HD_C)
    if prompt:
        ck_all, cv_all = ck, cv
    else:
        ck_all = jnp.concatenate([past_ck.astype(ck.dtype), ck], axis=1)
        cv_all = jnp.concatenate([past_cv.astype(cv.dtype), cv], axis=1)
    kpos_c = jnp.arange(ck_all.shape[1])
    fc = lambda qb, qp: stick_breaking_block(qb, ck_all, cv_all, qp, kpos_c)
    oc = sweep_query_blocks(fc, cq, pos) if prompt else fc(cq, pos)
    oc = oc.reshape(B, L, W_C)

    gs = jax.nn.sigmoid(gates.reshape(B, L, 3, D_MODEL))
    merged = gs[:, :, 0] * oa + gs[:, :, 1] * ob + gs[:, :, 2] * oc
    x = x + jnp.einsum('bld,de->ble', merged, w_out)

    h2 = rmsnorm(x, norm2_w)
    up = jnp.square(jax.nn.relu(jnp.einsum('bld,df->blf', h2, w_up)))
    x = x + jnp.einsum('blf,fd->bld', up, w_down)
    return x, ak, av, ck, cv, new_conv, s_new


def setup_inputs(seed: int = 0) -> dict:
    key = jax.random.key(seed)
    ks = jax.random.split(key, 32)
    f32 = jnp.float32

    def nrm(k, shape, scale=1.0):
        return jax.random.normal(k, shape, f32) * scale

    def gain(k, shape):
        return 1.0 + 0.01 * jax.random.normal(k, shape, f32)

    dt = jnp.exp(jax.random.uniform(ks[20], (DEPTH, H_B), f32, math.log(1e-3), math.log(1e-1)))
    return {
        'x_prompt': nrm(ks[0], (BATCH, SEQ, D_MODEL)),
        'x_sample': nrm(ks[1], (DEC_BATCH, DEC_SEQ, D_MODEL)),
        'cache_a_k': nrm(ks[2], (DEPTH, DEC_BATCH, PAST_LEN, 2 * H_A, HD_A)),
        'cache_a_v': nrm(ks[3], (DEPTH, DEC_BATCH, PAST_LEN, H_A, DV_A)),
        'cache_c_k': nrm(ks[4], (DEPTH, DEC_BATCH, PAST_LEN, H_C, HD_C)),
        'cache_c_v': nrm(ks[5], (DEPTH, DEC_BATCH, PAST_LEN, H_C, HD_C)),
        'state_b_conv': nrm(ks[6], (DEPTH, DEC_BATCH, CONV_W - 1, 3 * W_B)),
        'state_b_ssm': nrm(ks[7], (DEPTH, DEC_BATCH, H_B, HD_B, HD_B), 0.1),
        'norm1_w': gain(ks[8], (DEPTH, D_MODEL)),
        'w_in': nrm(ks[9], (DEPTH, D_MODEL, N_IN), D_MODEL ** -0.5),
        'lam_q1': nrm(ks[10], (DEPTH, HD_A), 0.1),
        'lam_k1': nrm(ks[11], (DEPTH, HD_A), 0.1),
        'lam_q2': nrm(ks[12], (DEPTH, HD_A), 0.1),
        'lam_k2': nrm(ks[13], (DEPTH, HD_A), 0.1),
        'subln_w': gain(ks[14], (DEPTH, DV_A)),
        'conv_w': nrm(ks[15], (DEPTH, CONV_W, 3 * W_B), CONV_W ** -0.5),
        'a_log': jnp.log(jax.random.uniform(ks[16], (DEPTH, H_B), f32, 1.0, 16.0)),
        'dt_bias': dt + jnp.log(-jnp.expm1(-dt)),
        'gdn_norm_w': gain(ks[17], (DEPTH, HD_B)),
        'w_out': nrm(ks[18], (DEPTH, D_MODEL, D_MODEL), D_MODEL ** -0.5),
        'norm2_w': gain(ks[19], (DEPTH, D_MODEL)),
        'w_up': nrm(ks[21], (DEPTH, D_MODEL, D_FF), D_MODEL ** -0.5),
        'w_down': nrm(ks[22], (DEPTH, D_FF, D_MODEL), D_FF ** -0.5),
        'final_norm_w': gain(ks[23], (D_MODEL,)),
    }


def reference(x_prompt, x_sample, cache_a_k, cache_a_v, cache_c_k, cache_c_v, state_b_conv, state_b_ssm,
              norm1_w, w_in, lam_q1, lam_k1, lam_q2, lam_k2, subln_w, conv_w, a_log, dt_bias,
              gdn_norm_w, w_out, norm2_w, w_up, w_down, final_norm_w):
    pos_p = jnp.arange(x_prompt.shape[1])
    past = cache_a_k.shape[2]
    pos_s = past + jnp.arange(x_sample.shape[1])
    xp, xs = x_prompt, x_sample
    pa_k, pa_v, pc_k, pc_v, pb_conv, pb_ssm = [], [], [], [], [], []
    sa_k, sa_v, sc_k, sc_v, sb_conv, sb_ssm = [], [], [], [], [], []
    for l in range(DEPTH):
        w = (norm1_w[l], w_in[l], lam_q1[l], lam_k1[l], lam_q2[l], lam_k2[l], subln_w[l],
             conv_w[l], a_log[l], dt_bias[l], gdn_norm_w[l], w_out[l], norm2_w[l], w_up[l], w_down[l])
        xp, ak, av, ck, cv, cb, sb = hybrid_layer(xp, pos_p, None, None, None, None, None, None,
                                                  *w, layer_idx=l, prompt=True)
        pa_k.append(ak); pa_v.append(av); pc_k.append(ck); pc_v.append(cv); pb_conv.append(cb); pb_ssm.append(sb)
        xs, ak, av, ck, cv, cb, sb = hybrid_layer(xs, pos_s, cache_a_k[l], cache_a_v[l], cache_c_k[l], cache_c_v[l],
                                                  state_b_conv[l], state_b_ssm[l],
                                                  *w, layer_idx=l, prompt=False)
        sa_k.append(ak); sa_v.append(av); sc_k.append(ck); sc_v.append(cv); sb_conv.append(cb); sb_ssm.append(sb)
    y_prompt = rmsnorm(xp, final_norm_w)
    y_sample = rmsnorm(xs, final_norm_w)
    return (y_prompt, y_sample,
            jnp.stack(pa_k), jnp.stack(pa_v), jnp.stack(pc_k), jnp.stack(pc_v), jnp.stack(pb_conv), jnp.stack(pb_ssm),
            jnp.stack(sa_k), jnp.stack(sa_v), jnp.stack(sc_k), jnp.stack(sc_v), jnp.stack(sb_conv), jnp.stack(sb_ssm))
```

```python
import functools
import math

import jax
import jax.numpy as jnp
from jax import lax
from jax.experimental import pallas as pl
from jax.experimental.pallas import tpu as pltpu

F32 = jnp.float32
BF16 = jnp.bfloat16

D_MODEL = 1024
CHUNK = 64
CHUNK_SHIFT = 6
H_A, HD_A, DV_A = 4, 128, 256
H_B, HD_B = 8, 128
H_C, HD_C = 8, 128
W_B = H_B * HD_B
CONV_W = 4
D_FF = 4 * D_MODEL
ROPE_THETA = 10000.0
EPS = 1e-6
LANES = 128
SUBLANES = 8
MIB = 1024 * 1024
EXP_UNDERFLOW = -104.0


def _cparams(sem, vmem_mib):
    return pltpu.CompilerParams(dimension_semantics=sem, vmem_limit_bytes=vmem_mib * MIB)


def _sigmoid(x):
    return 1.0 / (1.0 + jnp.exp(-x))


def _softplus_neg_abs(x):
    return jnp.log1p(jnp.exp(-jnp.abs(x)))


def _dot(a, b):
    return jnp.dot(a, b, preferred_element_type=F32)


def _split_bf16(a):
    hi = a.astype(BF16)
    return hi, (a - hi.astype(F32)).astype(BF16)


def _dot3(a, b):
    ah, al = _split_bf16(a)
    bh, bl = _split_bf16(b)
    return _dot(ah, bh) + _dot(ah, bl) + _dot(al, bh)


def _dot_nt(a, b):
    return lax.dot_general(a, b, (((1,), (1,)), ((), ())), preferred_element_type=F32)


def _dot_tn(a, b):
    return lax.dot_general(a, b, (((0,), (0,)), ((), ())), preferred_element_type=F32)


PROJ_TN = 512
PROJ_SEGS = (("aq", 2, True), ("ak", 2, True), ("av", 2, False), ("bqkv", 6, False), ("bz", 2, False),
             ("cq", 2, False), ("ck", 2, False), ("cv", 2, False), ("gates", 6, False))


def _proj_kernel(x_ref, nw_ref, w_ref, wsm_ref, cos_ref, sin_ref, *refs):
    out_refs = refs[:len(PROJ_SEGS)]
    sm_ref = refs[len(PROJ_SEGS)]
    h_scr = refs[len(PROJ_SEGS) + 1]
    n = pl.program_id(1)

    @pl.when(n == 0)
    def _():
        x = x_ref[...]
        y = x * lax.rsqrt(jnp.mean(x * x, axis=-1, keepdims=True) + EPS) * nw_ref[...]
        hb = y.astype(BF16)
        h_scr[...] = hb
        sm_ref[...] = _dot(hb, wsm_ref[...])

    acc = _dot(h_scr[...], w_ref[...])

    start = 0
    for (name, cnt, rot), o_ref in zip(PROJ_SEGS, out_refs):
        @pl.when((n >= start) & (n < start + cnt))
        def _(o_ref=o_ref, rot=rot):
            if rot:
                cos = cos_ref[...]
                sin = sin_ref[...]
                for s in range(PROJ_TN // LANES):
                    sl = acc[:, s * LANES:(s + 1) * LANES]
                    o_ref[:, s * LANES:(s + 1) * LANES] = sl * cos + pltpu.roll(sl, LANES // 2, 1) * sin
            else:
                o_ref[...] = acc
        start += cnt


def _in_proj(x2d, norm_w, w_main, w_small, cos_t, sin_t, tm):
    T = x2d.shape[0]
    nm = T // tm
    nn = w_main.shape[1] // PROJ_TN
    tbl_blocks = cos_t.shape[0] // tm
    out_shapes, out_specs = [], []
    start = 0
    for name, cnt, _ in PROJ_SEGS:
        out_shapes.append(jax.ShapeDtypeStruct((T, cnt * PROJ_TN), F32))
        out_specs.append(pl.BlockSpec((tm, PROJ_TN),
                                      lambda m, n, s=start, c=cnt: (m, jnp.clip(n - s, 0, c - 1))))
        start += cnt
    out_shapes.append(jax.ShapeDtypeStruct((T, LANES), F32))
    out_specs.append(pl.BlockSpec((tm, LANES), lambda m, n: (m, 0)))
    outs = pl.pallas_call(
        _proj_kernel,
        grid=(nm, nn),
        in_specs=[
            pl.BlockSpec((tm, D_MODEL), lambda m, n: (m, 0)),
            pl.BlockSpec((1, D_MODEL), lambda m, n: (0, 0)),
            pl.BlockSpec((D_MODEL, PROJ_TN), lambda m, n: (0, n)),
            pl.BlockSpec((D_MODEL, LANES), lambda m, n: (0, 0)),
            pl.BlockSpec((tm, LANES), lambda m, n: (m % tbl_blocks, 0)),
            pl.BlockSpec((tm, LANES), lambda m, n: (m % tbl_blocks, 0)),
        ],
        out_specs=out_specs,
        out_shape=out_shapes,
        scratch_shapes=[pltpu.VMEM((tm, D_MODEL), BF16)],
        compiler_params=_cparams(("parallel", "arbitrary"), 48),
        name="in_proj",
    )(x2d, norm_w.reshape(1, D_MODEL), w_main, w_small, cos_t, sin_t)
    res = {name: o for (name, _, _), o in zip(PROJ_SEGS, outs[:-1])}
    res["small"] = outs[-1]
    return res


def _attn_a_last_block(i, tq, tk, q_off, nk):
    last_q = q_off + i * tq + tq - 1
    last_key = ((last_q >> CHUNK_SHIFT) << CHUNK_SHIFT) + CHUNK - 1
    return jnp.minimum(last_key // tk, nk - 1)


def _attn_a_kernel(q_ref, k_ref, v_ref, lq1_ref, lk1_ref, lq2_ref, lk2_ref, sw_ref, o_ref,
                   m_scr, l_scr, acc_scr, *, tq, tk, q_off, n_valid, lam_init):
    i = pl.program_id(2)
    j = pl.program_id(3)
    nk = pl.num_programs(3)

    @pl.when(j == 0)
    def _():
        m_scr[...] = jnp.full(m_scr.shape, -jnp.inf, F32)
        l_scr[...] = jnp.zeros(l_scr.shape, F32)
        acc_scr[...] = jnp.zeros(acc_scr.shape, F32)

    @pl.when(j <= _attn_a_last_block(i, tq, tk, q_off, nk))
    def _():
        q = q_ref[0]
        k = k_ref[0]
        vb = v_ref[0].astype(BF16)
        qpos = q_off + i * tq + lax.broadcasted_iota(jnp.int32, (tq, 1), 0)
        kpos = j * tk + lax.broadcasted_iota(jnp.int32, (1, tk), 1)
        mask = ((kpos >> CHUNK_SHIFT) <= (qpos >> CHUNK_SHIFT)) & (kpos < n_valid)
        for c in range(2):
            qc = (q[:, c * HD_A:(c + 1) * HD_A] * (HD_A ** -0.5)).astype(BF16)
            kc = k[:, c * HD_A:(c + 1) * HD_A].astype(BF16)
            s = jnp.where(mask, _dot_nt(qc, kc), -jnp.inf)
            m_prev = m_scr[c]
            m_new = jnp.maximum(m_prev, jnp.max(s, axis=1, keepdims=True))
            alpha = jnp.exp(m_prev - m_new)
            p = jnp.exp(s - m_new)
            l_scr[c] = alpha * l_scr[c] + jnp.sum(p, axis=1, keepdims=True)
            acc_scr[c] = alpha * acc_scr[c] + _dot(p.astype(BF16), vb)
            m_scr[c] = m_new

    @pl.when(j == nk - 1)
    def _():
        lam = (jnp.exp(jnp.sum(lq1_ref[...] * lk1_ref[...], axis=1, keepdims=True))
               - jnp.exp(jnp.sum(lq2_ref[...] * lk2_ref[...], axis=1, keepdims=True)) + lam_init)
        o = acc_scr[0] / l_scr[0] - lam * (acc_scr[1] / l_scr[1])
        o = o * lax.rsqrt(jnp.mean(o * o, axis=-1, keepdims=True) + EPS) * sw_ref[...]
        o_ref[0] = o * (1.0 - lam_init)


def _attn_a(q, k, v, lq1, lk1, lq2, lk2, subln_w, *, tq, tk, q_off, n_valid, lam_init):
    B, Lq, _ = q.shape
    Lk = k.shape[1]
    nq, nk = Lq // tq, Lk // tk

    def kv_map(b, h, i, j):
        return (b, jnp.minimum(j, _attn_a_last_block(i, tq, tk, q_off, nk)), h)

    vec = pl.BlockSpec((1, HD_A), lambda b, h, i, j: (0, 0))
    return pl.pallas_call(
        functools.partial(_attn_a_kernel, tq=tq, tk=tk, q_off=q_off, n_valid=n_valid, lam_init=lam_init),
        grid=(B, H_A, nq, nk),
        in_specs=[
            pl.BlockSpec((1, tq, 2 * HD_A), lambda b, h, i, j: (b, i, h)),
            pl.BlockSpec((1, tk, 2 * HD_A), kv_map),
            pl.BlockSpec((1, tk, DV_A), kv_map),
            vec, vec, vec, vec,
            pl.BlockSpec((1, DV_A), lambda b, h, i, j: (0, 0)),
        ],
        out_specs=pl.BlockSpec((1, tq, DV_A), lambda b, h, i, j: (b, i, h)),
        out_shape=jax.ShapeDtypeStruct((B, Lq, H_A * DV_A), F32),
        scratch_shapes=[pltpu.VMEM((2, tq, 1), F32), pltpu.VMEM((2, tq, 1), F32),
                        pltpu.VMEM((2, tq, DV_A), F32)],
        compiler_params=_cparams(("parallel", "parallel", "parallel", "arbitrary"), 32),
        name="attn_a",
    )(q, k, v, lq1.reshape(1, HD_A), lk1.reshape(1, HD_A), lq2.reshape(1, HD_A), lk2.reshape(1, HD_A),
      subln_w.reshape(1, DV_A))


def _attn_c_kernel(q_ref, k_ref, v_ref, o_ref, *, tq, tk, q_off):
    i = pl.program_id(2)
    qb = (q_ref[0] * (HD_C ** -0.5)).astype(BF16)
    qpos = q_off + i * tq + lax.broadcasted_iota(jnp.int32, (tq, 1), 0)
    rj = lax.broadcasted_iota(jnp.int32, (tk, tk), 0)
    cs = lax.broadcasted_iota(jnp.int32, (tk, tk), 1)
    upper = jnp.where(rj > cs, 1.0, 0.0).astype(BF16)
    j_start = (q_off + (i + 1) * tq - 2) // tk

    def cond(carry):
        j, go, _, _ = carry
        return (j >= 0) & (go > 0)

    def body(carry):
        j, _, run, acc = carry
        base = pl.multiple_of(j * tk, tk)
        kb = k_ref[0, pl.ds(base, tk), :].astype(BF16)
        vb = v_ref[0, pl.ds(base, tk), :].astype(BF16)
        z = _dot_nt(qb, kb)
        kpos = base + lax.broadcasted_iota(jnp.int32, (1, tk), 1)
        mask = kpos < qpos
        t = _softplus_neg_abs(z)
        log_beta = jnp.minimum(z, 0.0) - t
        log_1m = jnp.where(mask, jnp.minimum(-z, 0.0) - t, 0.0)
        hi = log_1m.astype(BF16)
        lo = (log_1m - hi.astype(F32)).astype(BF16)
        tail = _dot(hi, upper) + _dot(lo, upper) + run
        a = jnp.where(mask, jnp.exp(log_beta + tail), 0.0)
        acc = acc + _dot(a.astype(BF16), vb)
        run = run + jnp.sum(log_1m, axis=1, keepdims=True)
        go = (jnp.max(run) > EXP_UNDERFLOW).astype(jnp.int32)
        return j - 1, go, run, acc

    init = (j_start, jnp.int32(1), jnp.zeros((tq, 1), F32), jnp.zeros((tq, HD_C), F32))
    _, _, _, acc = lax.while_loop(cond, body, init)
    o_ref[0] = acc


def _attn_c(q, k, v, *, tq, tk, q_off):
    B, Lq, _ = q.shape
    Lk = k.shape[1]
    return pl.pallas_call(
        functools.partial(_attn_c_kernel, tq=tq, tk=tk, q_off=q_off),
        grid=(B, H_C, Lq // tq),
        in_specs=[
            pl.BlockSpec((1, tq, HD_C), lambda b, h, i: (b, i, h)),
            pl.BlockSpec((1, Lk, HD_C), lambda b, h, i: (b, 0, h)),
            pl.BlockSpec((1, Lk, HD_C), lambda b, h, i: (b, 0, h)),
        ],
        out_specs=pl.BlockSpec((1, tq, HD_C), lambda b, h, i: (b, i, h)),
        out_shape=jax.ShapeDtypeStruct((B, Lq, H_C * HD_C), F32),
        compiler_params=_cparams(("parallel", "parallel", "arbitrary"), 40),
        name="attn_c",
    )(q, k, v)


CONV_CW = 512


def _conv_kernel(cur_ref, prev_ref, hist_ref, w_ref, o_ref, xs_scr, *, tl):
    i = pl.program_id(1)
    c = pl.program_id(2)
    xs_scr[0:SUBLANES, :] = jnp.where(i == 0, hist_ref[0], prev_ref[0])
    xs_scr[SUBLANES:, :] = cur_ref[0]
    w = w_ref[...]
    first = SUBLANES - (CONV_W - 1)
    conv = xs_scr[first:first + tl, :] * w[0:1, :]
    for t in range(1, CONV_W):
        conv = conv + xs_scr[first + t:first + t + tl, :] * w[t:t + 1, :]
    y = conv * _sigmoid(conv)
    blocks_per_part = W_B // CONV_CW
    is_qk = c < 2 * blocks_per_part
    qk_scale = jnp.where(c < blocks_per_part, HD_B ** -0.5, 1.0)
    for s in range(CONV_CW // HD_B):
        sl = y[:, s * HD_B:(s + 1) * HD_B]
        nrm = lax.rsqrt(jnp.sum(sl * sl, axis=-1, keepdims=True) + EPS) * qk_scale
        o_ref[0, :, s * HD_B:(s + 1) * HD_B] = sl * jnp.where(is_qk, nrm, 1.0)


def _conv_qkv(bqkv, hist8, conv_w, tl):
    B, L, N = bqkv.shape
    r = tl // SUBLANES
    return pl.pallas_call(
        functools.partial(_conv_kernel, tl=tl),
        grid=(B, L // tl, N // CONV_CW),
        in_specs=[
            pl.BlockSpec((1, tl, CONV_CW), lambda b, i, c: (b, i, c)),
            pl.BlockSpec((1, SUBLANES, CONV_CW), lambda b, i, c: (b, jnp.maximum(i * r - 1, 0), c)),
            pl.BlockSpec((1, SUBLANES, CONV_CW), lambda b, i, c: (b, 0, c)),
            pl.BlockSpec((CONV_W, CONV_CW), lambda b, i, c: (0, c)),
        ],
        out_specs=pl.BlockSpec((1, tl, CONV_CW), lambda b, i, c: (b, i, c)),
        out_shape=jax.ShapeDtypeStruct((B, L, N), F32),
        scratch_shapes=[pltpu.VMEM((tl + SUBLANES, CONV_CW), F32)],
        compiler_params=_cparams(("parallel", "parallel", "parallel"), 32),
        name="gdn_conv",
    )(bqkv, bqkv, hist8, conv_w)


def _gates_kernel(sm_ref, alog_ref, dtb_ref, o_ref, *, r, valid):
    blk = pl.program_id(1)
    x = sm_ref[0]
    y = x + dtb_ref[...]
    g = -jnp.exp(alog_ref[...]) * (jnp.maximum(y, 0.0) + _softplus_neg_abs(y))
    beta = _sigmoid(x)
    row = blk * r + lax.broadcasted_iota(jnp.int32, (r, 1), 0)
    live = row < valid
    g = jnp.where(live, g, 0.0)
    beta = jnp.where(live, beta, 0.0)
    ri = lax.broadcasted_iota(jnp.int32, (r, r), 0)
    ci = lax.broadcasted_iota(jnp.int32, (r, r), 1)
    tri = jnp.where(((ri >> CHUNK_SHIFT) == (ci >> CHUNK_SHIFT)) & (ci <= ri), 1.0, 0.0)
    gc = jnp.dot(tri, g, precision=lax.Precision.HIGHEST, preferred_element_type=F32)
    lane = lax.broadcasted_iota(jnp.int32, (r, LANES), 1)
    o_ref[0] = jnp.where(lane < H_B, gc, beta)


def _gdn_gates(small, a_log, dt_bias, r, valid):
    B, L, _ = small.shape
    pad = lambda v: jnp.pad(v.astype(F32), (0, LANES - H_B)).reshape(1, LANES)
    return pl.pallas_call(
        functools.partial(_gates_kernel, r=r, valid=valid),
        grid=(B, L // r),
        in_specs=[
            pl.BlockSpec((1, r, LANES), lambda b, i: (b, i, 0)),
            pl.BlockSpec((1, LANES), lambda b, i: (0, 0)),
            pl.BlockSpec((1, LANES), lambda b, i: (0, 0)),
        ],
        out_specs=pl.BlockSpec((1, r, LANES), lambda b, i: (b, i, 0)),
        out_shape=jax.ShapeDtypeStruct((B, L, LANES), F32),
        compiler_params=_cparams(("parallel", "parallel"), 32),
        name="gdn_gates",
    )(small, pad(a_log), pad(dt_bias))


def _gdn_kernel(q_ref, k_ref, v_ref, gb_ref, gbt_ref, bz_ref, nw_ref, s0_ref, o_ref, s_out_ref, s_scr, *, r):
    h = pl.program_id(1)
    blk = pl.program_id(2)
    nblk = pl.num_programs(2)

    @pl.when(blk == 0)
    def _():
        s_scr[...] = s0_ref[0, 0]

    q = q_ref[0]
    k = k_ref[0]
    v = v_ref[0]
    gb = gb_ref[0]
    lane = lax.broadcasted_iota(jnp.int32, (r, LANES), 1)
    gc_col = jnp.sum(jnp.where(lane == h, gb, 0.0), axis=1, keepdims=True)
    beta_col = jnp.sum(jnp.where(lane == h + H_B, gb, 0.0), axis=1, keepdims=True)
    gbt = gbt_ref[0]
    sub = lax.broadcasted_iota(jnp.int32, (2 * H_B, r), 0)
    gc_row = jnp.sum(jnp.where(sub == h, gbt, 0.0), axis=0, keepdims=True)

    ri = lax.broadcasted_iota(jnp.int32, (r, r), 0)
    ci = lax.broadcasted_iota(jnp.int32, (r, r), 1)
    same = (ri >> CHUNK_SHIFT) == (ci >> CHUNK_SHIFT)
    tri = same & (ci <= ri)
    strict = same & (ci < ri)
    decay = jnp.where(tri, jnp.exp(jnp.where(tri, gc_col - gc_row, 0.0)), 0.0)

    kbf = k.astype(BF16)
    kb = k * beta_col
    m = jnp.where(strict, _dot_nt(kb.astype(BF16), kbf) * decay, 0.0)
    eye = jnp.where(ri == ci, 1.0, 0.0)
    inv = eye - jnp.where((ri >> 1) == (ci >> 1), m, 0.0)
    for lg in range(1, CHUNK_SHIFT):
        off = jnp.where(((ri >> (lg + 1)) == (ci >> (lg + 1))) & ((ri >> lg) != (ci >> lg)), m, 0.0)
        inv = inv - _dot3(_dot3(inv, off), inv)
    egc = jnp.exp(gc_col)
    rhs = jnp.concatenate([v * beta_col, kb * egc], axis=1)
    sol = rhs + _dot((inv - eye).astype(BF16), rhs.astype(BF16))
    u0 = sol[:, :HD_B]
    w = sol[:, HD_B:]
    aqk = jnp.where(tri, _dot_nt(q.astype(BF16), kbf) * decay, 0.0)
    q_in = (q * egc).astype(BF16)

    s = s_scr[...]
    outs = []
    for c in range(r // CHUNK):
        lo, hi = c * CHUNK, (c + 1) * CHUNK
        g_last = gc_col[hi - 1:hi, :]
        sb = s.astype(BF16)
        u = u0[lo:hi] - _dot(w[lo:hi].astype(BF16), sb)
        ub = u.astype(BF16)
        outs.append(_dot(q_in[lo:hi], sb) + _dot(aqk[lo:hi, lo:hi].astype(BF16), ub))
        k_out = (k[lo:hi] * jnp.exp(g_last - gc_col[lo:hi])).astype(BF16)
        s = s * jnp.exp(g_last) + _dot_tn(k_out, ub)
    s_scr[...] = s
    o = outs[0] if len(outs) == 1 else jnp.concatenate(outs, axis=0)
    o = o * lax.rsqrt(jnp.mean(o * o, axis=-1, keepdims=True) + EPS) * nw_ref[...]
    z = bz_ref[0]
    o_ref[0] = o * (z * _sigmoid(z))

    @pl.when(blk == nblk - 1)
    def _():
        s_out_ref[0, 0] = s


def _gdn(qkv, gb, bz, norm_w, s0, r):
    B, L, _ = qkv.shape
    gbt = jnp.swapaxes(gb[:, :, :2 * H_B], 1, 2)
    return pl.pallas_call(
        functools.partial(_gdn_kernel, r=r),
        grid=(B, H_B, L // r),
        in_specs=[
            pl.BlockSpec((1, r, HD_B), lambda b, h, i: (b, i, h)),
            pl.BlockSpec((1, r, HD_B), lambda b, h, i: (b, i, H_B + h)),
            pl.BlockSpec((1, r, HD_B), lambda b, h, i: (b, i, 2 * H_B + h)),
            pl.BlockSpec((1, r, LANES), lambda b, h, i: (b, i, 0)),
            pl.BlockSpec((1, 2 * H_B, r), lambda b, h, i: (b, 0, i)),
            pl.BlockSpec((1, r, HD_B), lambda b, h, i: (b, i, h)),
            pl.BlockSpec((1, HD_B), lambda b, h, i: (0, 0)),
            pl.BlockSpec((1, 1, HD_B, HD_B), lambda b, h, i: (b, h, 0, 0)),
        ],
        out_specs=[
            pl.BlockSpec((1, r, HD_B), lambda b, h, i: (b, i, h)),
            pl.BlockSpec((1, 1, HD_B, HD_B), lambda b, h, i: (b, h, 0, 0)),
        ],
        out_shape=[jax.ShapeDtypeStruct((B, L, W_B), F32),
                   jax.ShapeDtypeStruct((B, H_B, HD_B, HD_B), F32)],
        scratch_shapes=[pltpu.VMEM((HD_B, HD_B), F32)],
        compiler_params=_cparams(("parallel", "parallel", "arbitrary"), 32),
        name="gdn_delta",
    )(qkv, qkv, qkv, gb, gbt, bz, norm_w.reshape(1, HD_B), s0)


def _merge_kernel(x_ref, oa_ref, ob_ref, oc_ref, g_ref, w_ref, nw_ref, x1_ref, h2_ref):
    g = g_ref[...]
    merged = (_sigmoid(g[:, 0:D_MODEL]) * oa_ref[...]
              + _sigmoid(g[:, D_MODEL:2 * D_MODEL]) * ob_ref[...]
              + _sigmoid(g[:, 2 * D_MODEL:3 * D_MODEL]) * oc_ref[...])
    x1 = x_ref[...] + _dot(merged.astype(BF16), w_ref[...])
    x1_ref[...] = x1
    h2 = x1 * lax.rsqrt(jnp.mean(x1 * x1, axis=-1, keepdims=True) + EPS) * nw_ref[...]
    h2_ref[...] = h2.astype(BF16)


def _merge_out(x2d, oa, ob, oc, gates, w_out_bf, norm2_w, tm):
    T = x2d.shape[0]
    row = pl.BlockSpec((tm, D_MODEL), lambda m: (m, 0))
    return pl.pallas_call(
        _merge_kernel,
        grid=(T // tm,),
        in_specs=[row, row, row, row,
                  pl.BlockSpec((tm, 3 * D_MODEL), lambda m: (m, 0)),
                  pl.BlockSpec((D_MODEL, D_MODEL), lambda m: (0, 0)),
                  pl.BlockSpec((1, D_MODEL), lambda m: (0, 0))],
        out_specs=[row, row],
        out_shape=[jax.ShapeDtypeStruct((T, D_MODEL), F32), jax.ShapeDtypeStruct((T, D_MODEL), BF16)],
        compiler_params=_cparams(("parallel",), 48),
        name="merge_out",
    )(x2d, oa, ob, oc, gates, w_out_bf, norm2_w.reshape(1, D_MODEL))


MLP_FC = 1024


def _mlp_kernel(h_ref, x_ref, wu_ref, wd_ref, fw_ref, o_ref, *, final):
    h = h_ref[...]
    acc = x_ref[...]
    for c in range(D_FF // MLP_FC):
        up = _dot(h, wu_ref[:, c * MLP_FC:(c + 1) * MLP_FC])
        up = jnp.square(jnp.maximum(up, 0.0))
        acc = acc + _dot(up.astype(BF16), wd_ref[c * MLP_FC:(c + 1) * MLP_FC, :])
    if final:
        acc = acc * lax.rsqrt(jnp.mean(acc * acc, axis=-1, keepdims=True) + EPS) * fw_ref[...]
    o_ref[...] = acc


def _mlp(h2, x1, w_up_bf, w_down_bf, final_w, tm, final):
    T = x1.shape[0]
    row = pl.BlockSpec((tm, D_MODEL), lambda m: (m, 0))
    return pl.pallas_call(
        functools.partial(_mlp_kernel, final=final),
        grid=(T // tm,),
        in_specs=[row, row,
                  pl.BlockSpec((D_MODEL, D_FF), lambda m: (0, 0)),
                  pl.BlockSpec((D_FF, D_MODEL), lambda m: (0, 0)),
                  pl.BlockSpec((1, D_MODEL), lambda m: (0, 0))],
        out_specs=row,
        out_shape=jax.ShapeDtypeStruct((T, D_MODEL), F32),
        compiler_params=_cparams(("parallel",), 56),
        name="mlp",
    )(h2, x1, w_up_bf, w_down_bf, final_w.reshape(1, D_MODEL))


def _rope_tables(pos):
    half = HD_A // 2
    inv = ROPE_THETA ** (-jnp.arange(half, dtype=F32) / half)
    ang = pos.astype(F32)[:, None] * inv[None, :]
    cos, sin = jnp.cos(ang), jnp.sin(ang)
    return jnp.concatenate([cos, cos], axis=-1), jnp.concatenate([-sin, sin], axis=-1)


def _pad_rows(a, n):
    return jnp.pad(a, ((0, 0), (0, n - a.shape[1]), (0, 0)))


def _layer(x, pos0, past, weights, layer_idx, final_w, final):
    (norm1_w, w_main, w_small, lq1, lk1, lq2, lk2, subln_w, conv_w, a_log, dt_bias,
     gdn_norm_w, w_out_bf, norm2_w, w_up_bf, w_down_bf) = weights
    B, L, _ = x.shape
    T = B * L
    prompt = past is None
    lam_init = 0.8 - 0.6 * math.exp(-0.3 * layer_idx)
    x2d = x.reshape(T, D_MODEL)

    cos_t, sin_t = _rope_tables(pos0 + jnp.arange(L))
    tm = min(512, T)
    if L % tm:
        cos_t, sin_t = jnp.tile(cos_t, (B, 1)), jnp.tile(sin_t, (B, 1))
    pr = _in_proj(x2d, norm1_w, w_main, w_small, cos_t, sin_t, tm)
    r3 = lambda a: a.reshape(B, L, a.shape[-1])
    aq, ak, av, cq, ck, cv = (r3(pr[n]) for n in ("aq", "ak", "av", "cq", "ck", "cv"))
    bqkv, bz, small = r3(pr["bqkv"]), r3(pr["bz"]), r3(pr["small"])

    if prompt:
        oa = _attn_a(aq, ak, av, lq1, lk1, lq2, lk2, subln_w, tq=256, tk=256, q_off=0, n_valid=L,
                     lam_init=lam_init)
    else:
        past_len = past[0].shape[1]
        ak_all = jnp.concatenate([past[0].reshape(B, past_len, -1), ak], axis=1)
        av_all = jnp.concatenate([past[1].reshape(B, past_len, -1), av], axis=1)
        oa = _attn_a(aq, ak_all, av_all, lq1, lk1, lq2, lk2, subln_w, tq=L, tk=past_len + L,
                     q_off=past_len, n_valid=past_len + L, lam_init=lam_init)

    hist8 = jnp.zeros((B, SUBLANES, 3 * W_B), F32)
    if not prompt:
        hist8 = hist8.at[:, SUBLANES - (CONV_W - 1):].set(past[4])
    qkv = _conv_qkv(bqkv, hist8, conv_w, tl=min(512, L))
    new_conv = jnp.concatenate([hist8[:, SUBLANES - (CONV_W - 1):], bqkv], axis=1)[:, L:] if L < CONV_W - 1 \
        else bqkv[:, L - (CONV_W - 1):]
    if prompt:
        r = 128
        gb = _gdn_gates(small, a_log, dt_bias, r, L)
        s0 = jnp.zeros((B, H_B, HD_B, HD_B), F32)
        ob, s_new = _gdn(qkv, gb, bz, gdn_norm_w, s0, r)
    else:
        lp = -(-L // CHUNK) * CHUNK
        gb = _gdn_gates(_pad_rows(small, lp), a_log, dt_bias, CHUNK, L)
        ob, s_new = _gdn(_pad_rows(qkv, lp), gb, _pad_rows(bz, lp), gdn_norm_w, past[5], CHUNK)
        ob = ob[:, :L]

    if prompt:
        oc = _attn_c(cq, ck, cv, tq=128, tk=128, q_off=0)
    else:
        past_len = past[2].shape[1]
        lk = -(-(past_len + L) // 128) * 128
        ck_all = _pad_rows(jnp.concatenate([past[2].reshape(B, past_len, -1), ck], axis=1), lk)
        cv_all = _pad_rows(jnp.concatenate([past[3].reshape(B, past_len, -1), cv], axis=1), lk)
        oc = _attn_c(cq, ck_all, cv_all, tq=L, tk=128, q_off=past_len)

    tm2 = min(256, T)
    f2 = lambda a: a.reshape(T, a.shape[-1])
    x1, h2 = _merge_out(x2d, f2(oa), f2(ob), f2(oc), pr["gates"], w_out_bf, norm2_w, tm2)
    x2 = _mlp(h2, x1, w_up_bf, w_down_bf, final_w, tm2, final)
    return x2.reshape(B, L, D_MODEL), ak, av, ck, cv, new_conv, s_new


def kernel(x_prompt, x_sample, cache_a_k, cache_a_v, cache_c_k, cache_c_v, state_b_conv, state_b_ssm,
           norm1_w, w_in, lam_q1, lam_k1, lam_q2, lam_k2, subln_w, conv_w, a_log, dt_bias,
           gdn_norm_w, w_out, norm2_w, w_up, w_down, final_norm_w):
    depth = w_in.shape[0]
    past_len = cache_a_k.shape[2]
    small_lo = 2 * H_A * HD_A * 2 + H_A * DV_A + 3 * W_B
    small_hi = small_lo + 2 * H_B
    xp, xs = x_prompt, x_sample
    p_out = [[] for _ in range(6)]
    s_out = [[] for _ in range(6)]
    for l in range(depth):
        w_main = jnp.concatenate([w_in[l][:, :small_lo], w_in[l][:, small_hi:]], axis=1).astype(BF16)
        w_small = jnp.pad(w_in[l][:, small_lo:small_hi], ((0, 0), (0, LANES - 2 * H_B))).astype(BF16)
        weights = (norm1_w[l], w_main, w_small, lam_q1[l], lam_k1[l], lam_q2[l], lam_k2[l], subln_w[l],
                   conv_w[l], a_log[l], dt_bias[l], gdn_norm_w[l], w_out[l].astype(BF16), norm2_w[l],
                   w_up[l].astype(BF16), w_down[l].astype(BF16))
        final = l == depth - 1
        res = _layer(xp, 0, None, weights, l, final_norm_w, final)
        xp = res[0]
        for lst, a in zip(p_out, res[1:]):
            lst.append(a)
        past = (cache_a_k[l], cache_a_v[l], cache_c_k[l], cache_c_v[l], state_b_conv[l], state_b_ssm[l])
        res = _layer(xs, past_len, past, weights, l, final_norm_w, final)
        xs = res[0]
        for lst, a in zip(s_out, res[1:]):
            lst.append(a)

    def pack(lists, batch, length):
        ak, av, ck, cv, cb, sb = (jnp.stack(t) for t in lists)
        return (ak.reshape(depth, batch, length, 2 * H_A, HD_A), av.reshape(depth, batch, length, H_A, DV_A),
                ck.reshape(depth, batch, length, H_C, HD_C), cv.reshape(depth, batch, length, H_C, HD_C), cb, sb)

    bp, lp = x_prompt.shape[0], x_prompt.shape[1]
    bs, ls = x_sample.shape[0], x_sample.shape[1]
    return (xp, xs) + pack(p_out, bp, lp) + pack(s_out, bs, ls)
```

```python
import functools
import math

import jax
import jax.numpy as jnp
from jax import lax
from jax.experimental import pallas as pl
from jax.experimental.pallas import tpu as pltpu

F32 = jnp.float32
BF16 = jnp.bfloat16

D_MODEL = 1024
CHUNK = 64
CHUNK_SHIFT = 6
H_A, HD_A, DV_A = 4, 128, 256
H_B, HD_B = 8, 128
H_C, HD_C = 8, 128
W_B = H_B * HD_B
CONV_W = 4
D_FF = 4 * D_MODEL
ROPE_THETA = 10000.0
EPS = 1e-6
LANES = 128
SUBLANES = 8
MIB = 1024 * 1024
LOG2E = 1.4426950408889634
EXP_UNDERFLOW = -104.0


def _cparams(sem, vmem_mib):
    return pltpu.CompilerParams(dimension_semantics=sem, vmem_limit_bytes=vmem_mib * MIB)


def _sigmoid(x):
    return 1.0 / (1.0 + jnp.exp(-x))


def _softplus_neg_abs(x):
    return jnp.log1p(jnp.exp(-jnp.abs(x)))


def _dot(a, b):
    return jnp.dot(a, b, preferred_element_type=F32)


def _split_bf16(a):
    hi = a.astype(BF16)
    return hi, (a - hi.astype(F32)).astype(BF16)


def _dot3(a, b):
    ah, al = _split_bf16(a)
    bh, bl = _split_bf16(b)
    return _dot(ah, bh) + _dot(ah, bl) + _dot(al, bh)


def _dot_nt(a, b):
    return lax.dot_general(a, b, (((1,), (1,)), ((), ())), preferred_element_type=F32)


def _dot_tn(a, b):
    return lax.dot_general(a, b, (((0,), (0,)), ((), ())), preferred_element_type=F32)


PROJ_TN = 1024
PROJ_GROUPS = ("aq", "ak", "av", "cq", "ck", "cv", "bqkv0", "bqkv1", "bqkv2", "bz", "g0", "g1", "g2")
F_AK, F_AV, F_CK, F_CV, F_BQKV = 0, 1, 2, 3, 4
B_AQ, B_AK, B_AV, B_CQ, B_CK, B_CV, B_BZ, B_G = 0, 1, 2, 3, 4, 5, 6, 7
N_FSEG, N_BSEG = 7, 10
QA_SCALE = HD_A ** -0.5 * LOG2E
QC_SCALE = HD_C ** -0.5


def _proj_f_index(n):
    return jnp.where(n <= 1, 0, jnp.where(n <= 3, 1, jnp.minimum(n - 2, N_FSEG - 1)))


def _proj_b_index(n):
    return jnp.where(n <= 5, n, jnp.where(n <= 8, 5, n - 3))


def _proj_kernel(x_ref, nw_ref, w_ref, wsm_ref, cos_ref, sin_ref, f_ref, b_ref, sm_ref, h_scr):
    n = pl.program_id(1)

    @pl.when(n == 0)
    def _():
        x = x_ref[...]
        y = x * lax.rsqrt(jnp.mean(x * x, axis=-1, keepdims=True) + EPS) * nw_ref[...]
        hb = y.astype(BF16)
        h_scr[...] = hb
        sm_ref[...] = _dot(hb, wsm_ref[...])

    def mm():
        return _dot(h_scr[...], w_ref[0])

    def rope_slabs(acc):
        cos = cos_ref[...]
        sin = sin_ref[...]
        for s in range(PROJ_TN // LANES):
            sl = acc[:, s * LANES:(s + 1) * LANES]
            yield s, sl * cos + pltpu.roll(sl, LANES // 2, 1) * sin

    @pl.when(n == 0)
    def _():
        for s, r in rope_slabs(mm()):
            b_ref[0, :, s * LANES:(s + 1) * LANES] = (r * QA_SCALE).astype(BF16)

    @pl.when(n == 1)
    def _():
        for s, r in rope_slabs(mm()):
            f_ref[0, :, s * LANES:(s + 1) * LANES] = r
            b_ref[0, :, s * LANES:(s + 1) * LANES] = r.astype(BF16)

    @pl.when((n == 2) | (n == 4) | (n == 5))
    def _():
        a = mm()
        f_ref[0] = a
        b_ref[0] = a.astype(BF16)

    @pl.when(n == 3)
    def _():
        b_ref[0] = (mm() * QC_SCALE).astype(BF16)

    @pl.when((n >= 6) & (n <= 8))
    def _():
        f_ref[0] = mm()

    @pl.when(n >= 9)
    def _():
        b_ref[0] = mm().astype(BF16)


def _in_proj(x2d, norm_w, w_blocks, w_small, cos_t, sin_t, tm):
    T = x2d.shape[0]
    tbl_blocks = cos_t.shape[0] // tm
    return pl.pallas_call(
        _proj_kernel,
        grid=(T // tm, len(PROJ_GROUPS)),
        in_specs=[
            pl.BlockSpec((tm, D_MODEL), lambda m, n: (m, 0)),
            pl.BlockSpec((1, D_MODEL), lambda m, n: (0, 0)),
            pl.BlockSpec((1, D_MODEL, PROJ_TN), lambda m, n: (n, 0, 0)),
            pl.BlockSpec((D_MODEL, LANES), lambda m, n: (0, 0)),
            pl.BlockSpec((tm, LANES), lambda m, n: (m % tbl_blocks, 0)),
            pl.BlockSpec((tm, LANES), lambda m, n: (m % tbl_blocks, 0)),
        ],
        out_specs=[
            pl.BlockSpec((1, tm, PROJ_TN), lambda m, n: (_proj_f_index(n), m, 0)),
            pl.BlockSpec((1, tm, PROJ_TN), lambda m, n: (_proj_b_index(n), m, 0)),
            pl.BlockSpec((tm, LANES), lambda m, n: (m, 0)),
        ],
        out_shape=[jax.ShapeDtypeStruct((N_FSEG, T, PROJ_TN), F32),
                   jax.ShapeDtypeStruct((N_BSEG, T, PROJ_TN), BF16),
                   jax.ShapeDtypeStruct((T, LANES), F32)],
        scratch_shapes=[pltpu.VMEM((tm, D_MODEL), BF16)],
        compiler_params=_cparams(("parallel", "arbitrary"), 48),
        name="in_proj",
    )(x2d, norm_w.reshape(1, D_MODEL), w_blocks, w_small, cos_t, sin_t)


def _attn_a_kernel(q_ref, k_ref, v_ref, lq1_ref, lk1_ref, lq2_ref, lk2_ref, sw_ref, o_ref,
                   m_scr, l_scr, acc_scr, *, tq, tk, rows, q_off, n_valid, lam_init):
    i = pl.program_id(2)
    q = q_ref[0, 0]
    first_q = q_off + i * tq
    last_q = first_q + tq - 1
    vis_first = jnp.minimum(((first_q >> CHUNK_SHIFT) + 1) << CHUNK_SHIFT, n_valid)
    vis_last = jnp.minimum(((last_q >> CHUNK_SHIFT) + 1) << CHUNK_SHIFT, n_valid)
    n_full = vis_first // tk
    n_blocks = (vis_last + tk - 1) // tk

    m_scr[...] = jnp.full(m_scr.shape, -jnp.inf, F32)
    l_scr[...] = jnp.zeros(l_scr.shape, F32)
    acc_scr[...] = jnp.zeros(acc_scr.shape, F32)
    rg = tq // rows
    streams = [(r, c) for r in range(rows) for c in range(2)]
    ss = range(len(streams))
    qs = [q[r * rg:(r + 1) * rg, c * HD_A:(c + 1) * HD_A] for r, c in streams]
    qpos = [first_q + r * rg + lax.broadcasted_iota(jnp.int32, (rg, 1), 0) for r in range(rows)]
    nch = tk // LANES
    nacc = DV_A // LANES

    def block(j, masked):
        base = pl.multiple_of(j * tk, tk)
        k = k_ref[0, 0, pl.ds(base, tk), :]
        v = v_ref[0, 0, pl.ds(base, tk), :]
        kc = [k[:, c * HD_A:(c + 1) * HD_A] for c in range(2)]
        s = [_dot_nt(qs[n], kc[streams[n][1]]) for n in ss]
        if masked:
            kpos = base + lax.broadcasted_iota(jnp.int32, (1, tk), 1)
            kchunk = kpos >> CHUNK_SHIFT
            mask = [(kchunk <= (qpos[r] >> CHUNK_SHIFT)) & (kpos < n_valid) for r in range(rows)]
            s = [jnp.where(mask[streams[n][0]], s[n], -jnp.inf) for n in ss]
        chunks = [[s[n][:, t * LANES:(t + 1) * LANES] for t in range(nch)] for n in ss]
        m_prev = [m_scr[n] for n in ss]
        m_new = [jnp.maximum(m_prev[n], jnp.max(functools.reduce(jnp.maximum, chunks[n]), axis=1, keepdims=True))
                 for n in ss]
        alpha = [jnp.exp2(m_prev[n] - m_new[n]) for n in ss]
        ps = [[jnp.exp2(ch - m_new[n]) for ch in chunks[n]] for n in ss]
        for n in ss:
            l_scr[n] = alpha[n] * l_scr[n] + jnp.sum(functools.reduce(jnp.add, ps[n]), axis=1, keepdims=True)
            m_scr[n] = m_new[n]
        p = [jnp.concatenate([t.astype(BF16) for t in ps[n]], axis=1) for n in ss]
        pv = [_dot(p[n], v) for n in ss]
        for n in ss:
            for t in range(nacc):
                sl = slice(t * LANES, (t + 1) * LANES)
                acc_scr[n, :, sl] = alpha[n] * acc_scr[n, :, sl] + pv[n][:, sl]

    def full_body(j, carry):
        block(j, False)
        return carry

    def edge_body(j, carry):
        block(j, True)
        return carry

    lax.fori_loop(0, n_full, full_body, 0)
    lax.fori_loop(n_full, n_blocks, edge_body, 0)

    lam = (jnp.exp(jnp.sum(lq1_ref[...] * lk1_ref[...], axis=1, keepdims=True))
           - jnp.exp(jnp.sum(lq2_ref[...] * lk2_ref[...], axis=1, keepdims=True)) + lam_init)
    for r in range(rows):
        parts = []
        for t in range(nacc):
            sl = slice(t * LANES, (t + 1) * LANES)
            parts.append(acc_scr[2 * r, :, sl] / l_scr[2 * r] - lam * (acc_scr[2 * r + 1, :, sl] / l_scr[2 * r + 1]))
        o = jnp.concatenate(parts, axis=1)
        o = o * lax.rsqrt(jnp.mean(o * o, axis=-1, keepdims=True) + EPS) * sw_ref[...]
        o_ref[0, r * rg:(r + 1) * rg, :] = o * (1.0 - lam_init)


def _attn_a(q, k, v, lq1, lk1, lq2, lk2, subln_w, *, tq, tk, rows, q_off, n_valid, lam_init):
    (qa, qs), (ka, ks), (va, vs) = q, k, v
    _, B, Lq, _ = qa.shape
    Lk = ka.shape[2]
    vec = pl.BlockSpec((1, HD_A), lambda b, h, i: (0, 0))
    return pl.pallas_call(
        functools.partial(_attn_a_kernel, tq=tq, tk=tk, rows=rows, q_off=q_off, n_valid=n_valid,
                          lam_init=lam_init),
        grid=(B, H_A, Lq // tq),
        in_specs=[
            pl.BlockSpec((1, 1, tq, 2 * HD_A), lambda b, h, i: (qs, b, i, h)),
            pl.BlockSpec((1, 1, Lk, 2 * HD_A), lambda b, h, i: (ks, b, 0, h)),
            pl.BlockSpec((1, 1, Lk, DV_A), lambda b, h, i: (vs, b, 0, h)),
            vec, vec, vec, vec,
            pl.BlockSpec((1, DV_A), lambda b, h, i: (0, 0)),
        ],
        out_specs=pl.BlockSpec((1, tq, DV_A), lambda b, h, i: (b, i, h)),
        out_shape=jax.ShapeDtypeStruct((B, Lq, H_A * DV_A), F32),
        scratch_shapes=[pltpu.VMEM((2 * rows, tq // rows, LANES), F32),
                        pltpu.VMEM((2 * rows, tq // rows, LANES), F32),
                        pltpu.VMEM((2 * rows, tq // rows, DV_A), F32)],
        compiler_params=_cparams(("parallel", "parallel", "arbitrary"), 40),
        name="attn_a",
    )(qa, ka, va, lq1.reshape(1, HD_A), lk1.reshape(1, HD_A), lq2.reshape(1, HD_A), lk2.reshape(1, HD_A),
      subln_w.reshape(1, DV_A))


C_HEADS = 4


def _attn_c_kernel(q_ref, k_ref, v_ref, o_ref, *, tq, tk, q_off):
    i = pl.program_id(2)
    q = q_ref[0, 0]
    qpos = q_off + i * tq + lax.broadcasted_iota(jnp.int32, (tq, 1), 0)
    rj = lax.broadcasted_iota(jnp.int32, (tk, tk), 0)
    cs = lax.broadcasted_iota(jnp.int32, (tk, tk), 1)
    upper = jnp.where(rj > cs, 1.0, 0.0).astype(BF16)
    j_start = (q_off + (i + 1) * tq - 2) // tk

    def cond(carry):
        j, go = carry[0], carry[1]
        return (j >= 0) & (go > 0)

    def body(carry):
        j = carry[0]
        runs, accs = carry[2], carry[3]
        base = pl.multiple_of(j * tk, tk)
        kpos = base + lax.broadcasted_iota(jnp.int32, (1, tk), 1)
        mask = kpos < qpos
        gs = range(C_HEADS)
        sl = [slice(g * HD_C, (g + 1) * HD_C) for g in gs]
        z = [_dot_nt(q[:, sl[g]], k_ref[0, 0, pl.ds(base, tk), sl[g]]) for g in gs]
        t = [_softplus_neg_abs(z[g]) for g in gs]
        log_beta = [jnp.minimum(z[g], 0.0) - t[g] for g in gs]
        log_1m = [jnp.where(mask, jnp.minimum(-z[g], 0.0) - t[g], 0.0) for g in gs]
        parts = [_split_bf16(log_1m[g]) for g in gs]
        tail = [_dot(parts[g][0], upper) + _dot(parts[g][1], upper) + runs[g] for g in gs]
        a = [jnp.where(mask, jnp.exp(log_beta[g] + tail[g]), 0.0).astype(BF16) for g in gs]
        new_accs = [accs[g] + _dot(a[g], v_ref[0, 0, pl.ds(base, tk), sl[g]]) for g in gs]
        new_runs = [runs[g] + jnp.sum(log_1m[g], axis=1, keepdims=True) for g in gs]
        top = functools.reduce(jnp.maximum, new_runs)
        go = (jnp.max(top) > EXP_UNDERFLOW).astype(jnp.int32)
        return j - 1, go, tuple(new_runs), tuple(new_accs)

    init = (j_start, jnp.int32(1),
            tuple(jnp.zeros((tq, 1), F32) for _ in range(C_HEADS)),
            tuple(jnp.zeros((tq, HD_C), F32) for _ in range(C_HEADS)))
    res = lax.while_loop(cond, body, init)
    for g in range(C_HEADS):
        o_ref[0, :, g * HD_C:(g + 1) * HD_C] = res[3][g]


def _attn_c(q, k, v, *, tq, tk, q_off):
    (qa, qs), (ka, ks), (va, vs) = q, k, v
    _, B, Lq, _ = qa.shape
    Lk = ka.shape[2]
    w = C_HEADS * HD_C
    return pl.pallas_call(
        functools.partial(_attn_c_kernel, tq=tq, tk=tk, q_off=q_off),
        grid=(B, H_C // C_HEADS, Lq // tq),
        in_specs=[
            pl.BlockSpec((1, 1, tq, w), lambda b, g, i: (qs, b, i, g)),
            pl.BlockSpec((1, 1, Lk, w), lambda b, g, i: (ks, b, 0, g)),
            pl.BlockSpec((1, 1, Lk, w), lambda b, g, i: (vs, b, 0, g)),
        ],
        out_specs=pl.BlockSpec((1, tq, w), lambda b, g, i: (b, i, g)),
        out_shape=jax.ShapeDtypeStruct((B, Lq, H_C * HD_C), F32),
        compiler_params=_cparams(("parallel", "parallel", "arbitrary"), 48),
        name="attn_c",
    )(qa, ka, va)


CONV_CW = 512


def _conv_kernel(cur_ref, prev_ref, hist_ref, w_ref, o_ref, xs_scr, *, tl):
    i = pl.program_id(1)
    c = pl.program_id(2)
    xs_scr[0:SUBLANES, :] = jnp.where(i == 0, hist_ref[0], prev_ref[0, 0])
    xs_scr[SUBLANES:, :] = cur_ref[0, 0]
    w = w_ref[...]
    first = SUBLANES - (CONV_W - 1)
    conv = xs_scr[first:first + tl, :] * w[0:1, :]
    for t in range(1, CONV_W):
        conv = conv + xs_scr[first + t:first + t + tl, :] * w[t:t + 1, :]
    y = conv * _sigmoid(conv)
    blocks_per_part = W_B // CONV_CW
    is_qk = c < 2 * blocks_per_part
    qk_scale = jnp.where(c < blocks_per_part, HD_B ** -0.5, 1.0)
    for s in range(CONV_CW // HD_B):
        sl = y[:, s * HD_B:(s + 1) * HD_B]
        nrm = lax.rsqrt(jnp.sum(sl * sl, axis=-1, keepdims=True) + EPS) * qk_scale
        o_ref[0, :, s * HD_B:(s + 1) * HD_B] = sl * jnp.where(is_qk, nrm, 1.0)


def _conv_qkv(fseg, hist8, conv_w, tl):
    _, B, L, _ = fseg.shape
    N = 3 * W_B
    r = tl // SUBLANES
    per = PROJ_TN // CONV_CW
    return pl.pallas_call(
        functools.partial(_conv_kernel, tl=tl),
        grid=(B, L // tl, N // CONV_CW),
        in_specs=[
            pl.BlockSpec((1, 1, tl, CONV_CW), lambda b, i, c: (F_BQKV + c // per, b, i, c % per)),
            pl.BlockSpec((1, 1, SUBLANES, CONV_CW),
                         lambda b, i, c: (F_BQKV + c // per, b, jnp.maximum(i * r - 1, 0), c % per)),
            pl.BlockSpec((1, SUBLANES, CONV_CW), lambda b, i, c: (b, 0, c)),
            pl.BlockSpec((CONV_W, CONV_CW), lambda b, i, c: (0, c)),
        ],
        out_specs=pl.BlockSpec((1, tl, CONV_CW), lambda b, i, c: (b, i, c)),
        out_shape=jax.ShapeDtypeStruct((B, L, N), F32),
        scratch_shapes=[pltpu.VMEM((tl + SUBLANES, CONV_CW), F32)],
        compiler_params=_cparams(("parallel", "parallel", "parallel"), 32),
        name="gdn_conv",
    )(fseg, fseg, hist8, conv_w)


def _gates_kernel(sm_ref, alog_ref, dtb_ref, o_ref, *, r, valid):
    blk = pl.program_id(1)
    x = sm_ref[0]
    y = x + dtb_ref[...]
    g = -jnp.exp(alog_ref[...]) * (jnp.maximum(y, 0.0) + _softplus_neg_abs(y))
    beta = _sigmoid(x)
    row = blk * r + lax.broadcasted_iota(jnp.int32, (r, 1), 0)
    live = row < valid
    g = jnp.where(live, g, 0.0)
    beta = jnp.where(live, beta, 0.0)
    ri = lax.broadcasted_iota(jnp.int32, (r, r), 0)
    ci = lax.broadcasted_iota(jnp.int32, (r, r), 1)
    tri = jnp.where(((ri >> CHUNK_SHIFT) == (ci >> CHUNK_SHIFT)) & (ci <= ri), 1.0, 0.0)
    gc = jnp.dot(tri, g, precision=lax.Precision.HIGHEST, preferred_element_type=F32)
    lane = lax.broadcasted_iota(jnp.int32, (r, LANES), 1)
    o_ref[0] = jnp.where(lane < H_B, gc, beta)


def _gdn_gates(small, a_log, dt_bias, r, valid):
    B, L, _ = small.shape
    pad = lambda v: jnp.pad(v.astype(F32), (0, LANES - H_B)).reshape(1, LANES)
    return pl.pallas_call(
        functools.partial(_gates_kernel, r=r, valid=valid),
        grid=(B, L // r),
        in_specs=[
            pl.BlockSpec((1, r, LANES), lambda b, i: (b, i, 0)),
            pl.BlockSpec((1, LANES), lambda b, i: (0, 0)),
            pl.BlockSpec((1, LANES), lambda b, i: (0, 0)),
        ],
        out_specs=pl.BlockSpec((1, r, LANES), lambda b, i: (b, i, 0)),
        out_shape=jax.ShapeDtypeStruct((B, L, LANES), F32),
        compiler_params=_cparams(("parallel", "parallel"), 32),
        name="gdn_gates",
    )(small, pad(a_log), pad(dt_bias))


def _gdn_kernel(qkv_ref, gb_ref, gbt_ref, bz_ref, nw_ref, s0_ref, o_ref, s_out_ref, s_scr, *, r):
    blk = pl.program_id(1)
    nblk = pl.num_programs(1)

    @pl.when(blk == 0)
    def _():
        s_scr[...] = s0_ref[0]

    gb = gb_ref[0]
    gbt = gbt_ref[0]
    ri = lax.broadcasted_iota(jnp.int32, (r, r), 0)
    ci = lax.broadcasted_iota(jnp.int32, (r, r), 1)
    same = (ri >> CHUNK_SHIFT) == (ci >> CHUNK_SHIFT)
    tri = same & (ci <= ri)
    strict = same & (ci < ri)
    eye = jnp.where(ri == ci, 1.0, 0.0)
    pair = (ri >> 1) == (ci >> 1)
    off_masks = [((ri >> (lg + 1)) == (ci >> (lg + 1))) & ((ri >> lg) != (ci >> lg))
                 for lg in range(1, CHUNK_SHIFT)]
    nw = nw_ref[...]

    hs = range(H_B)
    q = [qkv_ref[0, :, h * HD_B:(h + 1) * HD_B] for h in hs]
    k = [qkv_ref[0, :, (H_B + h) * HD_B:(H_B + h + 1) * HD_B] for h in hs]
    v = [qkv_ref[0, :, (2 * H_B + h) * HD_B:(2 * H_B + h + 1) * HD_B] for h in hs]
    gc_col = [gb[:, h:h + 1] for h in hs]
    beta_col = [gb[:, H_B + h:H_B + h + 1] for h in hs]
    decay = [jnp.where(tri, jnp.exp(jnp.where(tri, gc_col[h] - gbt[h:h + 1, :], 0.0)), 0.0) for h in hs]
    kbf = [k[h].astype(BF16) for h in hs]
    kb = [k[h] * beta_col[h] for h in hs]
    m = [jnp.where(strict, _dot_nt(kb[h].astype(BF16), kbf[h]) * decay[h], 0.0) for h in hs]
    inv = [eye - jnp.where(pair, m[h], 0.0) for h in hs]
    for off_mask in off_masks:
        inv_s = [_split_bf16(inv[h]) for h in hs]
        off_s = [_split_bf16(jnp.where(off_mask, m[h], 0.0)) for h in hs]
        t = [_dot(inv_s[h][0], off_s[h][0]) + _dot(inv_s[h][0], off_s[h][1]) + _dot(inv_s[h][1], off_s[h][0])
             for h in hs]
        t_s = [_split_bf16(t[h]) for h in hs]
        inv = [inv[h] - (_dot(t_s[h][0], inv_s[h][0]) + _dot(t_s[h][0], inv_s[h][1])
                         + _dot(t_s[h][1], inv_s[h][0])) for h in hs]
    egc = [jnp.exp(gc_col[h]) for h in hs]
    rhs = [jnp.concatenate([v[h] * beta_col[h], kb[h] * egc[h]], axis=1) for h in hs]
    sol = [rhs[h] + _dot((inv[h] - eye).astype(BF16), rhs[h].astype(BF16)) for h in hs]
    aqk = [jnp.where(tri, _dot_nt(q[h].astype(BF16), kbf[h]) * decay[h], 0.0).astype(BF16) for h in hs]
    q_in = [(q[h] * egc[h]).astype(BF16) for h in hs]

    s = [s_scr[h] for h in hs]
    outs = [[] for _ in hs]
    for c in range(r // CHUNK):
        lo, hi = c * CHUNK, (c + 1) * CHUNK
        g_last = [gc_col[h][hi - 1:hi, :] for h in hs]
        sb = [s[h].astype(BF16) for h in hs]
        u = [sol[h][lo:hi, :HD_B] - _dot(sol[h][lo:hi, HD_B:].astype(BF16), sb[h]) for h in hs]
        ub = [u[h].astype(BF16) for h in hs]
        for h in hs:
            outs[h].append(_dot(q_in[h][lo:hi], sb[h]) + _dot(aqk[h][lo:hi, lo:hi], ub[h]))
        k_out = [(k[h][lo:hi] * jnp.exp(g_last[h] - gc_col[h][lo:hi])).astype(BF16) for h in hs]
        s = [s[h] * jnp.exp(g_last[h]) + _dot_tn(k_out[h], ub[h]) for h in hs]
    for h in hs:
        s_scr[h] = s[h]
        o = outs[h][0] if len(outs[h]) == 1 else jnp.concatenate(outs[h], axis=0)
        o = o * lax.rsqrt(jnp.mean(o * o, axis=-1, keepdims=True) + EPS) * nw
        z = bz_ref[0, 0, :, h * HD_B:(h + 1) * HD_B].astype(F32)
        o_ref[0, :, h * HD_B:(h + 1) * HD_B] = o * (z * _sigmoid(z))

    @pl.when(blk == nblk - 1)
    def _():
        s_out_ref[0] = s_scr[...]


def _gdn(qkv, gb, bz, norm_w, s0, r):
    B, L, _ = qkv.shape
    bza, bzs = bz
    gbt = jnp.swapaxes(gb[:, :, :2 * H_B], 1, 2)
    return pl.pallas_call(
        functools.partial(_gdn_kernel, r=r),
        grid=(B, L // r),
        in_specs=[
            pl.BlockSpec((1, r, 3 * W_B), lambda b, i: (b, i, 0)),
            pl.BlockSpec((1, r, LANES), lambda b, i: (b, i, 0)),
            pl.BlockSpec((1, 2 * H_B, r), lambda b, i: (b, 0, i)),
            pl.BlockSpec((1, 1, r, W_B), lambda b, i: (bzs, b, i, 0)),
            pl.BlockSpec((1, HD_B), lambda b, i: (0, 0)),
            pl.BlockSpec((1, H_B, HD_B, HD_B), lambda b, i: (b, 0, 0, 0)),
        ],
        out_specs=[
            pl.BlockSpec((1, r, W_B), lambda b, i: (b, i, 0)),
            pl.BlockSpec((1, H_B, HD_B, HD_B), lambda b, i: (b, 0, 0, 0)),
        ],
        out_shape=[jax.ShapeDtypeStruct((B, L, W_B), F32),
                   jax.ShapeDtypeStruct((B, H_B, HD_B, HD_B), F32)],
        scratch_shapes=[pltpu.VMEM((H_B, HD_B, HD_B), F32)],
        compiler_params=_cparams(("parallel", "arbitrary"), 40),
        name="gdn_delta",
    )(qkv, gb, gbt, bza, norm_w.reshape(1, HD_B), s0)


def _merge_kernel(x_ref, oa_ref, ob_ref, oc_ref, g0_ref, g1_ref, g2_ref, w_ref, nw_ref, x1_ref, h2_ref):
    merged = (_sigmoid(g0_ref[0].astype(F32)) * oa_ref[...]
              + _sigmoid(g1_ref[0].astype(F32)) * ob_ref[...]
              + _sigmoid(g2_ref[0].astype(F32)) * oc_ref[...])
    x1 = x_ref[...] + _dot(merged.astype(BF16), w_ref[...])
    x1_ref[...] = x1
    h2 = x1 * lax.rsqrt(jnp.mean(x1 * x1, axis=-1, keepdims=True) + EPS) * nw_ref[...]
    h2_ref[...] = h2.astype(BF16)


def _merge_out(x2d, oa, ob, oc, bseg, w_out_bf, norm2_w, tm):
    T = x2d.shape[0]
    row = pl.BlockSpec((tm, D_MODEL), lambda m: (m, 0))
    gate = lambda s: pl.BlockSpec((1, tm, D_MODEL), lambda m: (B_G + s, m, 0))
    return pl.pallas_call(
        _merge_kernel,
        grid=(T // tm,),
        in_specs=[row, row, row, row, gate(0), gate(1), gate(2),
                  pl.BlockSpec((D_MODEL, D_MODEL), lambda m: (0, 0)),
                  pl.BlockSpec((1, D_MODEL), lambda m: (0, 0))],
        out_specs=[row, row],
        out_shape=[jax.ShapeDtypeStruct((T, D_MODEL), F32), jax.ShapeDtypeStruct((T, D_MODEL), BF16)],
        compiler_params=_cparams(("parallel",), 48),
        name="merge_out",
    )(x2d, oa, ob, oc, bseg, bseg, bseg, w_out_bf, norm2_w.reshape(1, D_MODEL))


MLP_FC = 1024


def _mlp_kernel(h_ref, x_ref, wu_ref, wd_ref, fw_ref, o_ref, *, final):
    h = h_ref[...]
    acc = x_ref[...]
    for c in range(D_FF // MLP_FC):
        up = _dot(h, wu_ref[:, c * MLP_FC:(c + 1) * MLP_FC])
        up = jnp.square(jnp.maximum(up, 0.0))
        acc = acc + _dot(up.astype(BF16), wd_ref[c * MLP_FC:(c + 1) * MLP_FC, :])
    if final:
        acc = acc * lax.rsqrt(jnp.mean(acc * acc, axis=-1, keepdims=True) + EPS) * fw_ref[...]
    o_ref[...] = acc


def _mlp(h2, x1, w_up_bf, w_down_bf, final_w, tm, final):
    T = x1.shape[0]
    row = pl.BlockSpec((tm, D_MODEL), lambda m: (m, 0))
    return pl.pallas_call(
        functools.partial(_mlp_kernel, final=final),
        grid=(T // tm,),
        in_specs=[row, row,
                  pl.BlockSpec((D_MODEL, D_FF), lambda m: (0, 0)),
                  pl.BlockSpec((D_FF, D_MODEL), lambda m: (0, 0)),
                  pl.BlockSpec((1, D_MODEL), lambda m: (0, 0))],
        out_specs=row,
        out_shape=jax.ShapeDtypeStruct((T, D_MODEL), F32),
        compiler_params=_cparams(("parallel",), 56),
        name="mlp",
    )(h2, x1, w_up_bf, w_down_bf, final_w.reshape(1, D_MODEL))


def _rope_tables(pos):
    half = HD_A // 2
    inv = ROPE_THETA ** (-jnp.arange(half, dtype=F32) / half)
    ang = pos.astype(F32)[:, None] * inv[None, :]
    cos, sin = jnp.cos(ang), jnp.sin(ang)
    return jnp.concatenate([cos, cos], axis=-1), jnp.concatenate([-sin, sin], axis=-1)


def _pad_rows(a, n):
    return jnp.pad(a, ((0, 0), (0, n - a.shape[1]), (0, 0)))


def _round_up(n, m):
    return -(-n // m) * m


def _layer(x, pos0, past, weights, layer_idx, final_w, final):
    (norm1_w, w_blocks, w_small, lq1, lk1, lq2, lk2, subln_w, conv_w, a_log, dt_bias,
     gdn_norm_w, w_out_bf, norm2_w, w_up_bf, w_down_bf) = weights
    B, L, _ = x.shape
    T = B * L
    prompt = past is None
    lam_init = 0.8 - 0.6 * math.exp(-0.3 * layer_idx)
    x2d = x.reshape(T, D_MODEL)

    cos_t, sin_t = _rope_tables(pos0 + jnp.arange(L))
    tm = min(1024, T)
    if L % tm:
        cos_t, sin_t = jnp.tile(cos_t, (B, 1)), jnp.tile(sin_t, (B, 1))
    fseg, bseg, small = _in_proj(x2d, norm1_w, w_blocks, w_small, cos_t, sin_t, tm)
    fseg4 = fseg.reshape(N_FSEG, B, L, PROJ_TN)
    bseg4 = bseg.reshape(N_BSEG, B, L, PROJ_TN)
    small = small.reshape(B, L, LANES)
    ak, av, ck, cv = fseg4[F_AK], fseg4[F_AV], fseg4[F_CK], fseg4[F_CV]

    def with_past(past_arr, seg, lk):
        past_len = past_arr.shape[1]
        allk = jnp.concatenate([past_arr.reshape(B, past_len, -1).astype(BF16), bseg4[seg]], axis=1)
        return _pad_rows(allk, lk)[None], 0

    if prompt:
        oa = _attn_a((bseg4, B_AQ), (bseg4, B_AK), (bseg4, B_AV), lq1, lk1, lq2, lk2, subln_w,
                     tq=512, tk=512, rows=2, q_off=0, n_valid=L, lam_init=lam_init)
    else:
        past_len = past[0].shape[1]
        lk = _round_up(past_len + L, LANES)
        oa = _attn_a((bseg4, B_AQ), with_past(past[0], B_AK, lk), with_past(past[1], B_AV, lk),
                     lq1, lk1, lq2, lk2, subln_w, tq=L, tk=lk, rows=1, q_off=past_len, n_valid=past_len + L,
                     lam_init=lam_init)

    hist8 = jnp.zeros((B, SUBLANES, 3 * W_B), F32)
    if not prompt:
        hist8 = hist8.at[:, SUBLANES - (CONV_W - 1):].set(past[4])
    qkv = _conv_qkv(fseg4, hist8, conv_w, tl=min(512, L))
    tail = CONV_W - 1
    assert L >= tail
    new_conv = jnp.concatenate([fseg4[F_BQKV + s, :, L - tail:] for s in range(3)], axis=-1)
    if prompt:
        r = 128
        gb = _gdn_gates(small, a_log, dt_bias, r, L)
        s0 = jnp.zeros((B, H_B, HD_B, HD_B), F32)
        ob, s_new = _gdn(qkv, gb, (bseg4, B_BZ), gdn_norm_w, s0, r)
    else:
        lp = _round_up(L, CHUNK)
        gb = _gdn_gates(_pad_rows(small, lp), a_log, dt_bias, CHUNK, L)
        bz_pad = _pad_rows(bseg4[B_BZ], lp)[None]
        ob, s_new = _gdn(_pad_rows(qkv, lp), gb, (bz_pad, 0), gdn_norm_w, past[5], CHUNK)
        ob = ob[:, :L]

    if prompt:
        oc = _attn_c((bseg4, B_CQ), (bseg4, B_CK), (bseg4, B_CV), tq=128, tk=128, q_off=0)
    else:
        past_len = past[2].shape[1]
        lk = _round_up(past_len + L, LANES)
        oc = _attn_c((bseg4, B_CQ), with_past(past[2], B_CK, lk), with_past(past[3], B_CV, lk),
                     tq=L, tk=LANES, q_off=past_len)

    tm2 = min(256, T)
    f2 = lambda a: a.reshape(T, a.shape[-1])
    x1, h2 = _merge_out(x2d, f2(oa), f2(ob), f2(oc), bseg, w_out_bf, norm2_w, tm2)
    x2 = _mlp(h2, x1, w_up_bf, w_down_bf, final_w, tm2, final)
    return x2.reshape(B, L, D_MODEL), ak, av, ck, cv, new_conv, s_new


def _proj_weight_blocks(w):
    o_ak = 2 * H_A * HD_A
    o_av = 2 * o_ak
    o_bqkv = o_av + H_A * DV_A
    o_small = o_bqkv + 3 * W_B
    o_bz = o_small + 2 * H_B
    o_cq = o_bz + W_B
    o_gates = o_cq + 3 * H_C * HD_C
    starts = [0, o_ak, o_av, o_cq, o_cq + 1024, o_cq + 2048, o_bqkv, o_bqkv + 1024, o_bqkv + 2048,
              o_bz, o_gates, o_gates + 1024, o_gates + 2048]
    blocks = jnp.stack([w[:, s:s + PROJ_TN] for s in starts]).astype(BF16)
    small = jnp.pad(w[:, o_small:o_bz], ((0, 0), (0, LANES - 2 * H_B))).astype(BF16)
    return blocks, small


def kernel(x_prompt, x_sample, cache_a_k, cache_a_v, cache_c_k, cache_c_v, state_b_conv, state_b_ssm,
           norm1_w, w_in, lam_q1, lam_k1, lam_q2, lam_k2, subln_w, conv_w, a_log, dt_bias,
           gdn_norm_w, w_out, norm2_w, w_up, w_down, final_norm_w):
    depth = w_in.shape[0]
    past_len = cache_a_k.shape[2]
    xp, xs = x_prompt, x_sample
    p_out = [[] for _ in range(6)]
    s_out = [[] for _ in range(6)]
    for l in range(depth):
        w_blocks, w_small = _proj_weight_blocks(w_in[l])
        weights = (norm1_w[l], w_blocks, w_small, lam_q1[l], lam_k1[l], lam_q2[l], lam_k2[l], subln_w[l],
                   conv_w[l], a_log[l], dt_bias[l], gdn_norm_w[l], w_out[l].astype(BF16), norm2_w[l],
                   w_up[l].astype(BF16), w_down[l].astype(BF16))
        final = l == depth - 1
        res = _layer(xp, 0, None, weights, l, final_norm_w, final)
        xp = res[0]
        for lst, a in zip(p_out, res[1:]):
            lst.append(a)
        past = (cache_a_k[l], cache_a_v[l], cache_c_k[l], cache_c_v[l], state_b_conv[l], state_b_ssm[l])
        res = _layer(xs, past_len, past, weights, l, final_norm_w, final)
        xs = res[0]
        for lst, a in zip(s_out, res[1:]):
            lst.append(a)

    def pack(lists, batch, length):
        ak, av, ck, cv, cb, sb = (jnp.stack(t) for t in lists)
        return (ak.reshape(depth, batch, length, 2 * H_A, HD_A), av.reshape(depth, batch, length, H_A, DV_A),
                ck.reshape(depth, batch, length, H_C, HD_C), cv.reshape(depth, batch, length, H_C, HD_C), cb, sb)

    bp, lp = x_prompt.shape[0], x_prompt.shape[1]
    bs, ls = x_sample.shape[0], x_sample.shape[1]
    return (xp, xs) + pack(p_out, bp, lp) + pack(s_out, bs, ls)
```

```python
import functools
import math

import jax
import jax.numpy as jnp
from jax import lax
from jax.experimental import pallas as pl
from jax.experimental.pallas import tpu as pltpu

F32 = jnp.float32
BF16 = jnp.bfloat16

D_MODEL = 1024
CHUNK = 64
CHUNK_SHIFT = 6
H_A, HD_A, DV_A = 4, 128, 256
H_B, HD_B = 8, 128
H_C, HD_C = 8, 128
W_B = H_B * HD_B
CONV_W = 4
D_FF = 4 * D_MODEL
ROPE_THETA = 10000.0
EPS = 1e-6
LANES = 128
SUBLANES = 8
MIB = 1024 * 1024
LOG2E = 1.4426950408889634
EXP_UNDERFLOW = -104.0


def _cparams(sem, vmem_mib):
    return pltpu.CompilerParams(dimension_semantics=sem, vmem_limit_bytes=vmem_mib * MIB)


def _sigmoid(x):
    return 1.0 / (1.0 + jnp.exp(-x))


def _softplus_neg_abs(x):
    return jnp.log1p(jnp.exp(-jnp.abs(x)))


def _dot(a, b):
    return jnp.dot(a, b, preferred_element_type=F32)


def _split_bf16(a):
    hi = a.astype(BF16)
    return hi, (a - hi.astype(F32)).astype(BF16)


def _dot3(a, b):
    ah, al = _split_bf16(a)
    bh, bl = _split_bf16(b)
    return _dot(ah, bh) + _dot(ah, bl) + _dot(al, bh)


def _dot_nt(a, b):
    return lax.dot_general(a, b, (((1,), (1,)), ((), ())), preferred_element_type=F32)


def _dot_tn(a, b):
    return lax.dot_general(a, b, (((0,), (0,)), ((), ())), preferred_element_type=F32)


PROJ_TN = 1024
PROJ_GROUPS = ("aq", "ak", "av", "cq", "ck", "cv", "bqkv0", "bqkv1", "bqkv2", "bz", "g0", "g1", "g2")
F_AK, F_AV, F_CK, F_CV, F_BQKV = 0, 1, 2, 3, 4
B_AQ, B_AK, B_AV, B_CQ, B_CK, B_CV, B_BZ, B_G = 0, 1, 2, 3, 4, 5, 6, 7
N_FSEG, N_BSEG = 7, 10
QA_SCALE = HD_A ** -0.5 * LOG2E
QC_SCALE = HD_C ** -0.5


def _proj_f_index(n):
    return jnp.where(n <= 1, 0, jnp.where(n <= 3, 1, jnp.minimum(n - 2, N_FSEG - 1)))


def _proj_b_index(n):
    return jnp.where(n <= 5, n, jnp.where(n <= 8, 5, n - 3))


def _proj_kernel(x_ref, nw_ref, w_ref, wsm_ref, cos_ref, sin_ref, f_ref, b_ref, sm_ref, h_scr):
    n = pl.program_id(1)

    @pl.when(n == 0)
    def _():
        x = x_ref[...]
        y = x * lax.rsqrt(jnp.mean(x * x, axis=-1, keepdims=True) + EPS) * nw_ref[...]
        hb = y.astype(BF16)
        h_scr[...] = hb
        sm_ref[...] = _dot(hb, wsm_ref[0])

    def mm():
        return _dot(h_scr[...], w_ref[0, 0])

    def rope_slabs(acc):
        cos = cos_ref[...]
        sin = sin_ref[...]
        for s in range(PROJ_TN // LANES):
            sl = acc[:, s * LANES:(s + 1) * LANES]
            yield s, sl * cos + pltpu.roll(sl, LANES // 2, 1) * sin

    @pl.when(n == 0)
    def _():
        for s, r in rope_slabs(mm()):
            b_ref[0, :, s * LANES:(s + 1) * LANES] = (r * QA_SCALE).astype(BF16)

    @pl.when(n == 1)
    def _():
        for s, r in rope_slabs(mm()):
            f_ref[0, :, s * LANES:(s + 1) * LANES] = r
            b_ref[0, :, s * LANES:(s + 1) * LANES] = r.astype(BF16)

    @pl.when((n == 2) | (n == 4) | (n == 5))
    def _():
        a = mm()
        f_ref[0] = a
        b_ref[0] = a.astype(BF16)

    @pl.when(n == 3)
    def _():
        b_ref[0] = (mm() * QC_SCALE).astype(BF16)

    @pl.when((n >= 6) & (n <= 8))
    def _():
        f_ref[0] = mm()

    @pl.when(n >= 9)
    def _():
        b_ref[0] = mm().astype(BF16)


def _in_proj(x2d, norm_w, w_blocks, w_small, layer, cos_t, sin_t, tm):
    T = x2d.shape[0]
    tbl_blocks = cos_t.shape[0] // tm
    return pl.pallas_call(
        _proj_kernel,
        grid=(T // tm, len(PROJ_GROUPS)),
        in_specs=[
            pl.BlockSpec((tm, D_MODEL), lambda m, n: (m, 0)),
            pl.BlockSpec((1, D_MODEL), lambda m, n: (0, 0)),
            pl.BlockSpec((1, 1, D_MODEL, PROJ_TN), lambda m, n: (layer, n, 0, 0)),
            pl.BlockSpec((1, D_MODEL, LANES), lambda m, n: (layer, 0, 0)),
            pl.BlockSpec((tm, LANES), lambda m, n: (m % tbl_blocks, 0)),
            pl.BlockSpec((tm, LANES), lambda m, n: (m % tbl_blocks, 0)),
        ],
        out_specs=[
            pl.BlockSpec((1, tm, PROJ_TN), lambda m, n: (_proj_f_index(n), m, 0)),
            pl.BlockSpec((1, tm, PROJ_TN), lambda m, n: (_proj_b_index(n), m, 0)),
            pl.BlockSpec((tm, LANES), lambda m, n: (m, 0)),
        ],
        out_shape=[jax.ShapeDtypeStruct((N_FSEG, T, PROJ_TN), F32),
                   jax.ShapeDtypeStruct((N_BSEG, T, PROJ_TN), BF16),
                   jax.ShapeDtypeStruct((T, LANES), F32)],
        scratch_shapes=[pltpu.VMEM((tm, D_MODEL), BF16)],
        compiler_params=_cparams(("parallel", "arbitrary"), 48),
        name="in_proj",
    )(x2d, norm_w.reshape(1, D_MODEL), w_blocks, w_small, cos_t, sin_t)


def _attn_a_kernel(q_ref, k_ref, v_ref, lq1_ref, lk1_ref, lq2_ref, lk2_ref, sw_ref, o_ref,
                   m_scr, l_scr, acc_scr, *, tq, tk, rows, q_off, n_valid, lam_init):
    i = pl.program_id(2)
    q = q_ref[0, 0]
    first_q = q_off + i * tq
    last_q = first_q + tq - 1
    vis_first = jnp.minimum(((first_q >> CHUNK_SHIFT) + 1) << CHUNK_SHIFT, n_valid)
    vis_last = jnp.minimum(((last_q >> CHUNK_SHIFT) + 1) << CHUNK_SHIFT, n_valid)
    n_full = vis_first // tk
    n_blocks = (vis_last + tk - 1) // tk

    m_scr[...] = jnp.full(m_scr.shape, -jnp.inf, F32)
    l_scr[...] = jnp.zeros(l_scr.shape, F32)
    acc_scr[...] = jnp.zeros(acc_scr.shape, F32)
    rg = tq // rows
    streams = [(r, c) for r in range(rows) for c in range(2)]
    ss = range(len(streams))
    qs = [q[r * rg:(r + 1) * rg, c * HD_A:(c + 1) * HD_A] for r, c in streams]
    qpos = [first_q + r * rg + lax.broadcasted_iota(jnp.int32, (rg, 1), 0) for r in range(rows)]
    nch = tk // LANES
    nacc = DV_A // LANES

    def block(j, masked):
        base = pl.multiple_of(j * tk, tk)
        k = k_ref[0, 0, pl.ds(base, tk), :]
        v = v_ref[0, 0, pl.ds(base, tk), :]
        kc = [k[:, c * HD_A:(c + 1) * HD_A] for c in range(2)]
        s = [_dot_nt(qs[n], kc[streams[n][1]]) for n in ss]
        if masked:
            kpos = base + lax.broadcasted_iota(jnp.int32, (1, tk), 1)
            kchunk = kpos >> CHUNK_SHIFT
            mask = [(kchunk <= (qpos[r] >> CHUNK_SHIFT)) & (kpos < n_valid) for r in range(rows)]
            s = [jnp.where(mask[streams[n][0]], s[n], -jnp.inf) for n in ss]
        chunks = [[s[n][:, t * LANES:(t + 1) * LANES] for t in range(nch)] for n in ss]
        m_prev = [m_scr[n] for n in ss]
        m_new = [jnp.maximum(m_prev[n], jnp.max(functools.reduce(jnp.maximum, chunks[n]), axis=1, keepdims=True))
                 for n in ss]
        alpha = [jnp.exp2(m_prev[n] - m_new[n]) for n in ss]
        ps = [[jnp.exp2(ch - m_new[n]) for ch in chunks[n]] for n in ss]
        for n in ss:
            l_scr[n] = alpha[n] * l_scr[n] + jnp.sum(functools.reduce(jnp.add, ps[n]), axis=1, keepdims=True)
            m_scr[n] = m_new[n]
        p = [jnp.concatenate([t.astype(BF16) for t in ps[n]], axis=1) for n in ss]
        pv = [_dot(p[n], v) for n in ss]
        for n in ss:
            for t in range(nacc):
                sl = slice(t * LANES, (t + 1) * LANES)
                acc_scr[n, :, sl] = alpha[n] * acc_scr[n, :, sl] + pv[n][:, sl]

    def full_body(j, carry):
        block(j, False)
        return carry

    def edge_body(j, carry):
        block(j, True)
        return carry

    def pair_body(jj, carry):
        block(2 * jj, False)
        block(2 * jj + 1, False)
        return carry

    n_pairs = n_full // 2
    lax.fori_loop(0, n_pairs, pair_body, 0)
    lax.fori_loop(2 * n_pairs, n_full, full_body, 0)
    lax.fori_loop(n_full, n_blocks, edge_body, 0)

    lam = (jnp.exp(jnp.sum(lq1_ref[...] * lk1_ref[...], axis=1, keepdims=True))
           - jnp.exp(jnp.sum(lq2_ref[...] * lk2_ref[...], axis=1, keepdims=True)) + lam_init)
    for r in range(rows):
        parts = []
        for t in range(nacc):
            sl = slice(t * LANES, (t + 1) * LANES)
            parts.append(acc_scr[2 * r, :, sl] / l_scr[2 * r] - lam * (acc_scr[2 * r + 1, :, sl] / l_scr[2 * r + 1]))
        o = jnp.concatenate(parts, axis=1)
        o = o * lax.rsqrt(jnp.mean(o * o, axis=-1, keepdims=True) + EPS) * sw_ref[...]
        o_ref[0, r * rg:(r + 1) * rg, :] = o * (1.0 - lam_init)


def _attn_a(q, k, v, lq1, lk1, lq2, lk2, subln_w, *, tq, tk, rows, q_off, n_valid, lam_init):
    (qa, qs), (ka, ks), (va, vs) = q, k, v
    _, B, Lq, _ = qa.shape
    Lk = ka.shape[2]
    vec = pl.BlockSpec((1, HD_A), lambda b, h, i: (0, 0))
    return pl.pallas_call(
        functools.partial(_attn_a_kernel, tq=tq, tk=tk, rows=rows, q_off=q_off, n_valid=n_valid,
                          lam_init=lam_init),
        grid=(B, H_A, Lq // tq),
        in_specs=[
            pl.BlockSpec((1, 1, tq, 2 * HD_A), lambda b, h, i: (qs, b, i, h)),
            pl.BlockSpec((1, 1, Lk, 2 * HD_A), lambda b, h, i: (ks, b, 0, h)),
            pl.BlockSpec((1, 1, Lk, DV_A), lambda b, h, i: (vs, b, 0, h)),
            vec, vec, vec, vec,
            pl.BlockSpec((1, DV_A), lambda b, h, i: (0, 0)),
        ],
        out_specs=pl.BlockSpec((1, tq, DV_A), lambda b, h, i: (b, i, h)),
        out_shape=jax.ShapeDtypeStruct((B, Lq, H_A * DV_A), F32),
        scratch_shapes=[pltpu.VMEM((2 * rows, tq // rows, LANES), F32),
                        pltpu.VMEM((2 * rows, tq // rows, LANES), F32),
                        pltpu.VMEM((2 * rows, tq // rows, DV_A), F32)],
        compiler_params=_cparams(("parallel", "parallel", "arbitrary"), 40),
        name="attn_a",
    )(qa, ka, va, lq1.reshape(1, HD_A), lk1.reshape(1, HD_A), lq2.reshape(1, HD_A), lk2.reshape(1, HD_A),
      subln_w.reshape(1, DV_A))


C_HEADS = 4


def _attn_c_kernel(q_ref, k_ref, v_ref, o_ref, acc_scr, run_scr, *, tq, tk, q_off):
    i = pl.program_id(2)
    q = q_ref[0, 0]
    first_q = q_off + i * tq
    qpos = first_q + lax.broadcasted_iota(jnp.int32, (tq, 1), 0)
    j_start = (q_off + (i + 1) * tq - 2) // tk
    acc_scr[...] = jnp.zeros(acc_scr.shape, F32)
    run_scr[...] = jnp.zeros(run_scr.shape, F32)
    gs = range(C_HEADS)
    sl = [slice(g * HD_C, (g + 1) * HD_C) for g in gs]

    def cum_weights(w):
        rj = lax.broadcasted_iota(jnp.int32, (w, w + tk), 0)
        cs = lax.broadcasted_iota(jnp.int32, (w, w + tk), 1)
        return jnp.where((cs >= w) | (rj > cs), 1.0, 0.0).astype(BF16)

    def block(base, w, masked):
        cum_w = cum_weights(w)
        z = [_dot_nt(q[:, sl[g]], k_ref[0, 0, pl.ds(base, w), sl[g]]) for g in gs]
        t = [jnp.log(1.0 + jnp.exp(-jnp.abs(z[g]))) for g in gs]
        log_beta = [jnp.minimum(z[g], 0.0) - t[g] for g in gs]
        log_1m = [jnp.minimum(-z[g], 0.0) - t[g] for g in gs]
        if masked:
            mask = (base + lax.broadcasted_iota(jnp.int32, (1, w), 1)) < qpos
            log_1m = [jnp.where(mask, log_1m[g], 0.0) for g in gs]
        parts = [_split_bf16(log_1m[g]) for g in gs]
        cum = [_dot(parts[g][0], cum_w) + _dot(parts[g][1], cum_w) for g in gs]
        run = [run_scr[g] for g in gs]
        run_w = [jnp.concatenate([run[g]] * (w // tk), axis=1) for g in gs]
        a = [jnp.exp(log_beta[g] + cum[g][:, :w] + run_w[g]) for g in gs]
        if masked:
            a = [jnp.where(mask, a[g], 0.0) for g in gs]
        new_run = [run[g] + cum[g][:, w:] for g in gs]
        for g in gs:
            acc_scr[g] += _dot(a[g].astype(BF16), v_ref[0, 0, pl.ds(base, w), sl[g]])
            run_scr[g] = new_run[g]
        top = functools.reduce(jnp.maximum, new_run)
        return (jnp.max(top) > EXP_UNDERFLOW).astype(jnp.int32)

    wide = 2 * tk

    def edge_body(c):
        base = pl.multiple_of(c[0] - tk, tk)
        return base, block(base, tk, True)

    def wide_body(c):
        base = pl.multiple_of(c[0] - wide, tk)
        return base, block(base, wide, False)

    def last_body(c):
        base = pl.multiple_of(c[0] - tk, tk)
        return base, block(base, tk, False)

    c = ((j_start + 1) * tk, jnp.int32(1))
    c = lax.while_loop(lambda c: (c[0] > 0) & (c[1] > 0) & (c[0] > first_q), edge_body, c)
    c = lax.while_loop(lambda c: (c[0] >= wide) & (c[1] > 0), wide_body, c)
    lax.while_loop(lambda c: (c[0] > 0) & (c[1] > 0), last_body, c)
    for g in gs:
        o_ref[0, :, sl[g]] = acc_scr[g]


def _attn_c(q, k, v, *, tq, tk, q_off):
    (qa, qs), (ka, ks), (va, vs) = q, k, v
    _, B, Lq, _ = qa.shape
    Lk = ka.shape[2]
    w = C_HEADS * HD_C
    assert tk == LANES
    return pl.pallas_call(
        functools.partial(_attn_c_kernel, tq=tq, tk=tk, q_off=q_off),
        grid=(B, H_C // C_HEADS, Lq // tq),
        in_specs=[
            pl.BlockSpec((1, 1, tq, w), lambda b, g, i: (qs, b, i, g)),
            pl.BlockSpec((1, 1, Lk, w), lambda b, g, i: (ks, b, 0, g)),
            pl.BlockSpec((1, 1, Lk, w), lambda b, g, i: (vs, b, 0, g)),
        ],
        out_specs=pl.BlockSpec((1, tq, w), lambda b, g, i: (b, i, g)),
        out_shape=jax.ShapeDtypeStruct((B, Lq, H_C * HD_C), F32),
        scratch_shapes=[pltpu.VMEM((C_HEADS, tq, HD_C), F32), pltpu.VMEM((C_HEADS, tq, tk), F32)],
        compiler_params=_cparams(("parallel", "parallel", "arbitrary"), 48),
        name="attn_c",
    )(qa, ka, va)


CONV_CW = 512


def _conv_kernel(cur_ref, prev_ref, hist_ref, w_ref, o_ref, xs_scr, *, tl):
    i = pl.program_id(1)
    c = pl.program_id(2)
    xs_scr[0:SUBLANES, :] = jnp.where(i == 0, hist_ref[0], prev_ref[0, 0])
    xs_scr[SUBLANES:, :] = cur_ref[0, 0]
    w = w_ref[...]
    first = SUBLANES - (CONV_W - 1)
    conv = xs_scr[first:first + tl, :] * w[0:1, :]
    for t in range(1, CONV_W):
        conv = conv + xs_scr[first + t:first + t + tl, :] * w[t:t + 1, :]
    y = conv * _sigmoid(conv)
    blocks_per_part = W_B // CONV_CW
    is_qk = c < 2 * blocks_per_part
    qk_scale = jnp.where(c < blocks_per_part, HD_B ** -0.5, 1.0)
    for s in range(CONV_CW // HD_B):
        sl = y[:, s * HD_B:(s + 1) * HD_B]
        nrm = lax.rsqrt(jnp.sum(sl * sl, axis=-1, keepdims=True) + EPS) * qk_scale
        o_ref[0, :, s * HD_B:(s + 1) * HD_B] = sl * jnp.where(is_qk, nrm, 1.0)


def _conv_qkv(fseg, hist8, conv_w, tl):
    _, B, L, _ = fseg.shape
    N = 3 * W_B
    r = tl // SUBLANES
    per = PROJ_TN // CONV_CW
    return pl.pallas_call(
        functools.partial(_conv_kernel, tl=tl),
        grid=(B, L // tl, N // CONV_CW),
        in_specs=[
            pl.BlockSpec((1, 1, tl, CONV_CW), lambda b, i, c: (F_BQKV + c // per, b, i, c % per)),
            pl.BlockSpec((1, 1, SUBLANES, CONV_CW),
                         lambda b, i, c: (F_BQKV + c // per, b, jnp.maximum(i * r - 1, 0), c % per)),
            pl.BlockSpec((1, SUBLANES, CONV_CW), lambda b, i, c: (b, 0, c)),
            pl.BlockSpec((CONV_W, CONV_CW), lambda b, i, c: (0, c)),
        ],
        out_specs=pl.BlockSpec((1, tl, CONV_CW), lambda b, i, c: (b, i, c)),
        out_shape=jax.ShapeDtypeStruct((B, L, N), F32),
        scratch_shapes=[pltpu.VMEM((tl + SUBLANES, CONV_CW), F32)],
        compiler_params=_cparams(("parallel", "parallel", "parallel"), 32),
        name="gdn_conv",
    )(fseg, fseg, hist8, conv_w)


def _gates_kernel(sm_ref, alog_ref, dtb_ref, o_ref, *, r, valid):
    blk = pl.program_id(1)
    x = sm_ref[0]
    y = x + dtb_ref[...]
    g = -jnp.exp(alog_ref[...]) * (jnp.maximum(y, 0.0) + _softplus_neg_abs(y))
    beta = _sigmoid(x)
    row = blk * r + lax.broadcasted_iota(jnp.int32, (r, 1), 0)
    live = row < valid
    g = jnp.where(live, g, 0.0)
    beta = jnp.where(live, beta, 0.0)
    ri = lax.broadcasted_iota(jnp.int32, (r, r), 0)
    ci = lax.broadcasted_iota(jnp.int32, (r, r), 1)
    tri = jnp.where(((ri >> CHUNK_SHIFT) == (ci >> CHUNK_SHIFT)) & (ci <= ri), 1.0, 0.0)
    gc = jnp.dot(tri, g, precision=lax.Precision.HIGHEST, preferred_element_type=F32)
    lane = lax.broadcasted_iota(jnp.int32, (r, LANES), 1)
    o_ref[0] = jnp.where(lane < H_B, gc, beta)


def _gdn_gates(small, a_log, dt_bias, r, valid):
    B, L, _ = small.shape
    pad = lambda v: jnp.pad(v.astype(F32), (0, LANES - H_B)).reshape(1, LANES)
    return pl.pallas_call(
        functools.partial(_gates_kernel, r=r, valid=valid),
        grid=(B, L // r),
        in_specs=[
            pl.BlockSpec((1, r, LANES), lambda b, i: (b, i, 0)),
            pl.BlockSpec((1, LANES), lambda b, i: (0, 0)),
            pl.BlockSpec((1, LANES), lambda b, i: (0, 0)),
        ],
        out_specs=pl.BlockSpec((1, r, LANES), lambda b, i: (b, i, 0)),
        out_shape=jax.ShapeDtypeStruct((B, L, LANES), F32),
        compiler_params=_cparams(("parallel", "parallel"), 32),
        name="gdn_gates",
    )(small, pad(a_log), pad(dt_bias))


def _gdn_kernel(qkv_ref, gb_ref, gbt_ref, bz_ref, nw_ref, s0_ref, o_ref, s_out_ref, s_scr, *, r):
    blk = pl.program_id(1)
    nblk = pl.num_programs(1)

    @pl.when(blk == 0)
    def _():
        s_scr[...] = s0_ref[0]

    gb = gb_ref[0]
    gbt = gbt_ref[0]
    ri = lax.broadcasted_iota(jnp.int32, (r, r), 0)
    ci = lax.broadcasted_iota(jnp.int32, (r, r), 1)
    same = (ri >> CHUNK_SHIFT) == (ci >> CHUNK_SHIFT)
    tri = same & (ci <= ri)
    strict = same & (ci < ri)
    pair = (ri >> 1) == (ci >> 1)
    off_masks = [((ri >> (lg + 1)) == (ci >> (lg + 1))) & ((ri >> lg) != (ci >> lg))
                 for lg in range(1, CHUNK_SHIFT)]
    nw = nw_ref[...]

    hs = range(H_B)
    q = [qkv_ref[0, :, h * HD_B:(h + 1) * HD_B] for h in hs]
    k = [qkv_ref[0, :, (H_B + h) * HD_B:(H_B + h + 1) * HD_B] for h in hs]
    v = [qkv_ref[0, :, (2 * H_B + h) * HD_B:(2 * H_B + h + 1) * HD_B] for h in hs]
    gc_col = [gb[:, h:h + 1] for h in hs]
    beta_col = [gb[:, H_B + h:H_B + h + 1] for h in hs]
    decay = [jnp.where(tri, jnp.exp(jnp.where(tri, gc_col[h] - gbt[h:h + 1, :], 0.0)), 0.0) for h in hs]
    kbf = [k[h].astype(BF16) for h in hs]
    kb = [k[h] * beta_col[h] for h in hs]
    m = [jnp.where(strict, _dot_nt(kb[h].astype(BF16), kbf[h]) * decay[h], 0.0) for h in hs]
    x = [-jnp.where(pair, m[h], 0.0) for h in hs]
    for off_mask in off_masks:
        off = [jnp.where(off_mask, m[h], 0.0) for h in hs]
        xb = [x[h].astype(BF16) for h in hs]
        t = [off[h] + _dot(xb[h], off[h].astype(BF16)) for h in hs]
        x = [x[h] - t[h] - _dot(t[h].astype(BF16), xb[h]) for h in hs]
    egc = [jnp.exp(gc_col[h]) for h in hs]
    rhs = [jnp.concatenate([v[h] * beta_col[h], kb[h] * egc[h]], axis=1) for h in hs]
    sol = [rhs[h] + _dot(x[h].astype(BF16), rhs[h].astype(BF16)) for h in hs]
    aqk = [jnp.where(tri, _dot_nt(q[h].astype(BF16), kbf[h]) * decay[h], 0.0).astype(BF16) for h in hs]
    q_in = [(q[h] * egc[h]).astype(BF16) for h in hs]

    s = [s_scr[h] for h in hs]
    outs = [[] for _ in hs]
    for c in range(r // CHUNK):
        lo, hi = c * CHUNK, (c + 1) * CHUNK
        g_last = [gc_col[h][hi - 1:hi, :] for h in hs]
        sb = [s[h].astype(BF16) for h in hs]
        u = [sol[h][lo:hi, :HD_B] - _dot(sol[h][lo:hi, HD_B:].astype(BF16), sb[h]) for h in hs]
        ub = [u[h].astype(BF16) for h in hs]
        for h in hs:
            outs[h].append(_dot(q_in[h][lo:hi], sb[h]) + _dot(aqk[h][lo:hi, lo:hi], ub[h]))
        k_out = [(k[h][lo:hi] * jnp.exp(g_last[h] - gc_col[h][lo:hi])).astype(BF16) for h in hs]
        s = [s[h] * jnp.exp(g_last[h]) + _dot_tn(k_out[h], ub[h]) for h in hs]
    for h in hs:
        s_scr[h] = s[h]
        o = outs[h][0] if len(outs[h]) == 1 else jnp.concatenate(outs[h], axis=0)
        o = o * lax.rsqrt(jnp.mean(o * o, axis=-1, keepdims=True) + EPS) * nw
        z = bz_ref[0, 0, :, h * HD_B:(h + 1) * HD_B].astype(F32)
        o_ref[0, :, h * HD_B:(h + 1) * HD_B] = o * (z * _sigmoid(z))

    @pl.when(blk == nblk - 1)
    def _():
        s_out_ref[0] = s_scr[...]


def _gdn(qkv, gb, bz, norm_w, s0, r):
    B, L, _ = qkv.shape
    bza, bzs = bz
    gbt = jnp.swapaxes(gb[:, :, :2 * H_B], 1, 2)
    return pl.pallas_call(
        functools.partial(_gdn_kernel, r=r),
        grid=(B, L // r),
        in_specs=[
            pl.BlockSpec((1, r, 3 * W_B), lambda b, i: (b, i, 0)),
            pl.BlockSpec((1, r, LANES), lambda b, i: (b, i, 0)),
            pl.BlockSpec((1, 2 * H_B, r), lambda b, i: (b, 0, i)),
            pl.BlockSpec((1, 1, r, W_B), lambda b, i: (bzs, b, i, 0)),
            pl.BlockSpec((1, HD_B), lambda b, i: (0, 0)),
            pl.BlockSpec((1, H_B, HD_B, HD_B), lambda b, i: (b, 0, 0, 0)),
        ],
        out_specs=[
            pl.BlockSpec((1, r, W_B), lambda b, i: (b, i, 0)),
            pl.BlockSpec((1, H_B, HD_B, HD_B), lambda b, i: (b, 0, 0, 0)),
        ],
        out_shape=[jax.ShapeDtypeStruct((B, L, W_B), F32),
                   jax.ShapeDtypeStruct((B, H_B, HD_B, HD_B), F32)],
        scratch_shapes=[pltpu.VMEM((H_B, HD_B, HD_B), F32)],
        compiler_params=_cparams(("parallel", "arbitrary"), 40),
        name="gdn_delta",
    )(qkv, gb, gbt, bza, norm_w.reshape(1, HD_B), s0)


def _merge_kernel(x_ref, oa_ref, ob_ref, oc_ref, g0_ref, g1_ref, g2_ref, w_ref, nw_ref, x1_ref, h2_ref):
    merged = (_sigmoid(g0_ref[0].astype(F32)) * oa_ref[...]
              + _sigmoid(g1_ref[0].astype(F32)) * ob_ref[...]
              + _sigmoid(g2_ref[0].astype(F32)) * oc_ref[...])
    x1 = x_ref[...] + _dot(merged.astype(BF16), w_ref[...])
    x1_ref[...] = x1
    h2 = x1 * lax.rsqrt(jnp.mean(x1 * x1, axis=-1, keepdims=True) + EPS) * nw_ref[...]
    h2_ref[...] = h2.astype(BF16)


def _merge_out(x2d, oa, ob, oc, bseg, w_out_bf, norm2_w, tm):
    T = x2d.shape[0]
    row = pl.BlockSpec((tm, D_MODEL), lambda m: (m, 0))
    gate = lambda s: pl.BlockSpec((1, tm, D_MODEL), lambda m: (B_G + s, m, 0))
    return pl.pallas_call(
        _merge_kernel,
        grid=(T // tm,),
        in_specs=[row, row, row, row, gate(0), gate(1), gate(2),
                  pl.BlockSpec((D_MODEL, D_MODEL), lambda m: (0, 0)),
                  pl.BlockSpec((1, D_MODEL), lambda m: (0, 0))],
        out_specs=[row, row],
        out_shape=[jax.ShapeDtypeStruct((T, D_MODEL), F32), jax.ShapeDtypeStruct((T, D_MODEL), BF16)],
        compiler_params=_cparams(("parallel",), 48),
        name="merge_out",
    )(x2d, oa, ob, oc, bseg, bseg, bseg, w_out_bf, norm2_w.reshape(1, D_MODEL))


MLP_FC = 1024


def _mlp_kernel(h_ref, x_ref, wu_ref, wd_ref, fw_ref, o_ref, *, final):
    h = h_ref[...]
    acc = x_ref[...]
    for c in range(D_FF // MLP_FC):
        up = _dot(h, wu_ref[:, c * MLP_FC:(c + 1) * MLP_FC])
        up = jnp.square(jnp.maximum(up, 0.0))
        acc = acc + _dot(up.astype(BF16), wd_ref[c * MLP_FC:(c + 1) * MLP_FC, :])
    if final:
        acc = acc * lax.rsqrt(jnp.mean(acc * acc, axis=-1, keepdims=True) + EPS) * fw_ref[...]
    o_ref[...] = acc


def _mlp(h2, x1, w_up_bf, w_down_bf, final_w, tm, final):
    T = x1.shape[0]
    row = pl.BlockSpec((tm, D_MODEL), lambda m: (m, 0))
    return pl.pallas_call(
        functools.partial(_mlp_kernel, final=final),
        grid=(T // tm,),
        in_specs=[row, row,
                  pl.BlockSpec((D_MODEL, D_FF), lambda m: (0, 0)),
                  pl.BlockSpec((D_FF, D_MODEL), lambda m: (0, 0)),
                  pl.BlockSpec((1, D_MODEL), lambda m: (0, 0))],
        out_specs=row,
        out_shape=jax.ShapeDtypeStruct((T, D_MODEL), F32),
        compiler_params=_cparams(("parallel",), 56),
        name="mlp",
    )(h2, x1, w_up_bf, w_down_bf, final_w.reshape(1, D_MODEL))


def _rope_tables(pos):
    half = HD_A // 2
    inv = ROPE_THETA ** (-jnp.arange(half, dtype=F32) / half)
    ang = pos.astype(F32)[:, None] * inv[None, :]
    cos, sin = jnp.cos(ang), jnp.sin(ang)
    return jnp.concatenate([cos, cos], axis=-1), jnp.concatenate([-sin, sin], axis=-1)


def _pad_rows(a, n):
    return jnp.pad(a, ((0, 0), (0, n - a.shape[1]), (0, 0)))


def _round_up(n, m):
    return -(-n // m) * m


def _layer(x, pos0, past, weights, layer_idx, final_w, final):
    (norm1_w, w_blocks, w_small, lq1, lk1, lq2, lk2, subln_w, conv_w, a_log, dt_bias,
     gdn_norm_w, w_out_bf, norm2_w, w_up_bf, w_down_bf) = weights
    B, L, _ = x.shape
    T = B * L
    prompt = past is None
    lam_init = 0.8 - 0.6 * math.exp(-0.3 * layer_idx)
    x2d = x.reshape(T, D_MODEL)

    cos_t, sin_t = _rope_tables(pos0 + jnp.arange(L))
    tm = min(1024, T)
    if L % tm:
        cos_t, sin_t = jnp.tile(cos_t, (B, 1)), jnp.tile(sin_t, (B, 1))
    fseg, bseg, small = _in_proj(x2d, norm1_w, w_blocks, w_small, layer_idx, cos_t, sin_t, tm)
    fseg4 = fseg.reshape(N_FSEG, B, L, PROJ_TN)
    bseg4 = bseg.reshape(N_BSEG, B, L, PROJ_TN)
    small = small.reshape(B, L, LANES)
    ak, av, ck, cv = fseg4[F_AK], fseg4[F_AV], fseg4[F_CK], fseg4[F_CV]

    def with_past(past_arr, seg, lk):
        past_len = past_arr.shape[1]
        allk = jnp.concatenate([past_arr.reshape(B, past_len, -1).astype(BF16), bseg4[seg]], axis=1)
        return _pad_rows(allk, lk)[None], 0

    if prompt:
        oa = _attn_a((bseg4, B_AQ), (bseg4, B_AK), (bseg4, B_AV), lq1, lk1, lq2, lk2, subln_w,
                     tq=512, tk=512, rows=2, q_off=0, n_valid=L, lam_init=lam_init)
    else:
        past_len = past[0].shape[1]
        lk = _round_up(past_len + L, LANES)
        oa = _attn_a((bseg4, B_AQ), with_past(past[0], B_AK, lk), with_past(past[1], B_AV, lk),
                     lq1, lk1, lq2, lk2, subln_w, tq=L, tk=lk, rows=1, q_off=past_len, n_valid=past_len + L,
                     lam_init=lam_init)

    hist8 = jnp.zeros((B, SUBLANES, 3 * W_B), F32)
    if not prompt:
        hist8 = hist8.at[:, SUBLANES - (CONV_W - 1):].set(past[4])
    qkv = _conv_qkv(fseg4, hist8, conv_w, tl=min(512, L))
    tail = CONV_W - 1
    assert L >= tail
    new_conv = jnp.concatenate([fseg4[F_BQKV + s, :, L - tail:] for s in range(3)], axis=-1)
    if prompt:
        r = 128
        gb = _gdn_gates(small, a_log, dt_bias, r, L)
        s0 = jnp.zeros((B, H_B, HD_B, HD_B), F32)
        ob, s_new = _gdn(qkv, gb, (bseg4, B_BZ), gdn_norm_w, s0, r)
    else:
        lp = _round_up(L, CHUNK)
        gb = _gdn_gates(_pad_rows(small, lp), a_log, dt_bias, CHUNK, L)
        bz_pad = _pad_rows(bseg4[B_BZ], lp)[None]
        ob, s_new = _gdn(_pad_rows(qkv, lp), gb, (bz_pad, 0), gdn_norm_w, past[5], CHUNK)
        ob = ob[:, :L]

    if prompt:
        oc = _attn_c((bseg4, B_CQ), (bseg4, B_CK), (bseg4, B_CV), tq=128, tk=128, q_off=0)
    else:
        past_len = past[2].shape[1]
        lk = _round_up(past_len + L, LANES)
        oc = _attn_c((bseg4, B_CQ), with_past(past[2], B_CK, lk), with_past(past[3], B_CV, lk),
                     tq=L, tk=LANES, q_off=past_len)

    tm2 = min(256, T)
    f2 = lambda a: a.reshape(T, a.shape[-1])
    x1, h2 = _merge_out(x2d, f2(oa), f2(ob), f2(oc), bseg, w_out_bf, norm2_w, tm2)
    x2 = _mlp(h2, x1, w_up_bf, w_down_bf, final_w, tm2, final)
    return x2.reshape(B, L, D_MODEL), ak, av, ck, cv, new_conv, s_new


N_ALIGNED = 6
SMALL_W = 2 * H_B
SMALL_BLOCK = N_ALIGNED * PROJ_TN // LANES


def _pack_src_block(n):
    return jnp.where(n < 3, n, jnp.where(n < 6, n + 4, jnp.where(n < 9, n - 3, jnp.where(n == 9, 6, n))))


def _pack_kernel(a_ref, b_ref, sm_ref, o_ref, osm_ref):
    n = pl.program_id(1)
    shifted = ((n >= 3) & (n < 6)) | (n >= 9)

    @pl.when(n == 0)
    def _():
        lane = lax.broadcasted_iota(jnp.int32, sm_ref.shape[1:], 1)
        osm_ref[0] = jnp.where(lane < SMALL_W, sm_ref[0], 0.0).astype(BF16)

    @pl.when(jnp.logical_not(shifted))
    def _():
        o_ref[0, 0] = a_ref[0].astype(BF16)

    @pl.when(shifted)
    def _():
        last = PROJ_TN - LANES
        o_ref[0, 0] = pltpu.roll(a_ref[0], PROJ_TN - SMALL_W, 1).astype(BF16)
        tail_a = pltpu.roll(a_ref[0, :, last:], LANES - SMALL_W, 1)
        tail_b = pltpu.roll(b_ref[0], LANES - SMALL_W, 1)
        lane = lax.broadcasted_iota(jnp.int32, tail_a.shape, 1)
        o_ref[0, 0, :, last:] = jnp.where(lane < LANES - SMALL_W, tail_a, tail_b).astype(BF16)


def _proj_weight_blocks(w_in):
    depth = w_in.shape[0]
    per = PROJ_TN // LANES
    return pl.pallas_call(
        _pack_kernel,
        grid=(depth, len(PROJ_GROUPS)),
        in_specs=[
            pl.BlockSpec((1, D_MODEL, PROJ_TN), lambda l, n: (l, 0, _pack_src_block(n))),
            pl.BlockSpec((1, D_MODEL, LANES), lambda l, n: (l, 0, (_pack_src_block(n) + 1) * per)),
            pl.BlockSpec((1, D_MODEL, LANES), lambda l, n: (l, 0, SMALL_BLOCK)),
        ],
        out_specs=[
            pl.BlockSpec((1, 1, D_MODEL, PROJ_TN), lambda l, n: (l, n, 0, 0)),
            pl.BlockSpec((1, D_MODEL, LANES), lambda l, n: (l, 0, 0)),
        ],
        out_shape=[jax.ShapeDtypeStruct((depth, len(PROJ_GROUPS), D_MODEL, PROJ_TN), BF16),
                   jax.ShapeDtypeStruct((depth, D_MODEL, LANES), BF16)],
        compiler_params=_cparams(("parallel", "arbitrary"), 40),
        name="pack_w_in",
    )(w_in, w_in, w_in)


def kernel(x_prompt, x_sample, cache_a_k, cache_a_v, cache_c_k, cache_c_v, state_b_conv, state_b_ssm,
           norm1_w, w_in, lam_q1, lam_k1, lam_q2, lam_k2, subln_w, conv_w, a_log, dt_bias,
           gdn_norm_w, w_out, norm2_w, w_up, w_down, final_norm_w):
    depth = w_in.shape[0]
    past_len = cache_a_k.shape[2]
    xp, xs = x_prompt, x_sample
    p_out = [[] for _ in range(6)]
    s_out = [[] for _ in range(6)]
    w_blocks, w_small = _proj_weight_blocks(w_in)
    for l in range(depth):
        weights = (norm1_w[l], w_blocks, w_small, lam_q1[l], lam_k1[l], lam_q2[l], lam_k2[l], subln_w[l],
                   conv_w[l], a_log[l], dt_bias[l], gdn_norm_w[l], w_out[l].astype(BF16), norm2_w[l],
                   w_up[l].astype(BF16), w_down[l].astype(BF16))
        final = l == depth - 1
        res = _layer(xp, 0, None, weights, l, final_norm_w, final)
        xp = res[0]
        for lst, a in zip(p_out, res[1:]):
            lst.append(a)
        past = (cache_a_k[l], cache_a_v[l], cache_c_k[l], cache_c_v[l], state_b_conv[l], state_b_ssm[l])
        res = _layer(xs, past_len, past, weights, l, final_norm_w, final)
        xs = res[0]
        for lst, a in zip(s_out, res[1:]):
            lst.append(a)

    def pack(lists, batch, length):
        ak, av, ck, cv, cb, sb = (jnp.stack(t) for t in lists)
        return (ak.reshape(depth, batch, length, 2 * H_A, HD_A), av.reshape(depth, batch, length, H_A, DV_A),
                ck.reshape(depth, batch, length, H_C, HD_C), cv.reshape(depth, batch, length, H_C, HD_C), cb, sb)

    bp, lp = x_prompt.shape[0], x_prompt.shape[1]
    bs, ls = x_sample.shape[0], x_sample.shape[1]
    return (xp, xs) + pack(p_out, bp, lp) + pack(s_out, bs, ls)
```

```python
import functools
import math

import jax
import jax.numpy as jnp
from jax import lax
from jax.experimental import pallas as pl
from jax.experimental.pallas import tpu as pltpu

F32 = jnp.float32
BF16 = jnp.bfloat16

D_MODEL = 1024
CHUNK = 64
CHUNK_SHIFT = 6
H_A, HD_A, DV_A = 4, 128, 256
H_B, HD_B = 8, 128
H_C, HD_C = 8, 128
W_B = H_B * HD_B
CONV_W = 4
D_FF = 4 * D_MODEL
ROPE_THETA = 10000.0
EPS = 1e-6
LANES = 128
SUBLANES = 8
MIB = 1024 * 1024
LOG2E = 1.4426950408889634
EXP_UNDERFLOW = -104.0


def _cparams(sem, vmem_mib):
    return pltpu.CompilerParams(dimension_semantics=sem, vmem_limit_bytes=vmem_mib * MIB)


def _sigmoid(x):
    return 1.0 / (1.0 + jnp.exp(-x))


def _softplus_neg_abs(x):
    return jnp.log1p(jnp.exp(-jnp.abs(x)))


def _dot(a, b):
    return jnp.dot(a, b, preferred_element_type=F32)


def _split_bf16(a):
    hi = a.astype(BF16)
    return hi, (a - hi.astype(F32)).astype(BF16)


def _dot3(a, b):
    ah, al = _split_bf16(a)
    bh, bl = _split_bf16(b)
    return _dot(ah, bh) + _dot(ah, bl) + _dot(al, bh)


def _dot_nt(a, b):
    return lax.dot_general(a, b, (((1,), (1,)), ((), ())), preferred_element_type=F32)


def _dot_tn(a, b):
    return lax.dot_general(a, b, (((0,), (0,)), ((), ())), preferred_element_type=F32)


PROJ_TN = 1024
PROJ_GROUPS = ("aq", "ak", "av", "cq", "ck", "cv", "bqkv0", "bqkv1", "bqkv2", "bz", "g0", "g1", "g2")
F_AK, F_AV, F_CK, F_CV, F_BQKV = 0, 1, 2, 3, 4
B_AQ, B_AK, B_AV, B_CQ, B_CK, B_CV, B_BZ, B_G = 0, 1, 2, 3, 4, 5, 6, 7
N_FSEG, N_BSEG = 7, 10
QA_SCALE = HD_A ** -0.5 * LOG2E
QC_SCALE = HD_C ** -0.5


def _proj_f_index(n):
    return jnp.where(n <= 1, 0, jnp.where(n <= 3, 1, jnp.minimum(n - 2, N_FSEG - 1)))


def _proj_b_index(n):
    return jnp.where(n <= 5, n, jnp.where(n <= 8, 5, n - 3))


def _proj_kernel(x_ref, nw_ref, w_ref, wsm_ref, cos_ref, sin_ref, f_ref, b_ref, sm_ref, h_scr):
    n = pl.program_id(1)

    @pl.when(n == 0)
    def _():
        x = x_ref[...]
        y = x * lax.rsqrt(jnp.mean(x * x, axis=-1, keepdims=True) + EPS) * nw_ref[...]
        hb = y.astype(BF16)
        h_scr[...] = hb
        sm_ref[...] = _dot(hb, wsm_ref[0])

    def mm():
        return _dot(h_scr[...], w_ref[0, 0])

    def rope_slabs(acc):
        cos = cos_ref[...]
        sin = sin_ref[...]
        for s in range(PROJ_TN // LANES):
            sl = acc[:, s * LANES:(s + 1) * LANES]
            yield s, sl * cos + pltpu.roll(sl, LANES // 2, 1) * sin

    @pl.when(n == 0)
    def _():
        for s, r in rope_slabs(mm()):
            b_ref[0, :, s * LANES:(s + 1) * LANES] = (r * QA_SCALE).astype(BF16)

    @pl.when(n == 1)
    def _():
        for s, r in rope_slabs(mm()):
            f_ref[0, :, s * LANES:(s + 1) * LANES] = r
            b_ref[0, :, s * LANES:(s + 1) * LANES] = r.astype(BF16)

    @pl.when((n == 2) | (n == 4) | (n == 5))
    def _():
        a = mm()
        f_ref[0] = a
        b_ref[0] = a.astype(BF16)

    @pl.when(n == 3)
    def _():
        b_ref[0] = (mm() * QC_SCALE).astype(BF16)

    @pl.when((n >= 6) & (n <= 8))
    def _():
        f_ref[0] = mm()

    @pl.when(n >= 9)
    def _():
        b_ref[0] = mm().astype(BF16)


def _in_proj(x2d, norm_w, w_blocks, w_small, layer, cos_t, sin_t, tm):
    T = x2d.shape[0]
    tbl_blocks = cos_t.shape[0] // tm
    return pl.pallas_call(
        _proj_kernel,
        grid=(T // tm, len(PROJ_GROUPS)),
        in_specs=[
            pl.BlockSpec((tm, D_MODEL), lambda m, n: (m, 0)),
            pl.BlockSpec((1, D_MODEL), lambda m, n: (0, 0)),
            pl.BlockSpec((1, 1, D_MODEL, PROJ_TN), lambda m, n: (layer, n, 0, 0)),
            pl.BlockSpec((1, D_MODEL, LANES), lambda m, n: (layer, 0, 0)),
            pl.BlockSpec((tm, LANES), lambda m, n: (m % tbl_blocks, 0)),
            pl.BlockSpec((tm, LANES), lambda m, n: (m % tbl_blocks, 0)),
        ],
        out_specs=[
            pl.BlockSpec((1, tm, PROJ_TN), lambda m, n: (_proj_f_index(n), m, 0)),
            pl.BlockSpec((1, tm, PROJ_TN), lambda m, n: (_proj_b_index(n), m, 0)),
            pl.BlockSpec((tm, LANES), lambda m, n: (m, 0)),
        ],
        out_shape=[jax.ShapeDtypeStruct((N_FSEG, T, PROJ_TN), F32),
                   jax.ShapeDtypeStruct((N_BSEG, T, PROJ_TN), BF16),
                   jax.ShapeDtypeStruct((T, LANES), F32)],
        scratch_shapes=[pltpu.VMEM((tm, D_MODEL), BF16)],
        compiler_params=_cparams(("parallel", "arbitrary"), 48),
        name="in_proj",
    )(x2d, norm_w.reshape(1, D_MODEL), w_blocks, w_small, cos_t, sin_t)


def _attn_a_kernel(q_ref, k_ref, v_ref, lq1_ref, lk1_ref, lq2_ref, lk2_ref, sw_ref, o_ref,
                   m_scr, l_scr, acc_scr, *, tq, tk, rows, q_off, n_valid, lam_init):
    i = pl.program_id(2)
    q = q_ref[0, 0]
    first_q = q_off + i * tq
    last_q = first_q + tq - 1
    vis_first = jnp.minimum(((first_q >> CHUNK_SHIFT) + 1) << CHUNK_SHIFT, n_valid)
    vis_last = jnp.minimum(((last_q >> CHUNK_SHIFT) + 1) << CHUNK_SHIFT, n_valid)
    n_full = vis_first // tk
    n_blocks = (vis_last + tk - 1) // tk

    m_scr[...] = jnp.full(m_scr.shape, -jnp.inf, F32)
    l_scr[...] = jnp.zeros(l_scr.shape, F32)
    acc_scr[...] = jnp.zeros(acc_scr.shape, F32)
    rg = tq // rows
    streams = [(r, c) for r in range(rows) for c in range(2)]
    ss = range(len(streams))
    qs = [q[r * rg:(r + 1) * rg, c * HD_A:(c + 1) * HD_A] for r, c in streams]
    qpos = [first_q + r * rg + lax.broadcasted_iota(jnp.int32, (rg, 1), 0) for r in range(rows)]
    nch = tk // LANES
    nacc = DV_A // LANES

    def block(j, masked):
        base = pl.multiple_of(j * tk, tk)
        k = k_ref[0, 0, pl.ds(base, tk), :]
        v = v_ref[0, 0, pl.ds(base, tk), :]
        kc = [k[:, c * HD_A:(c + 1) * HD_A] for c in range(2)]
        s = [_dot_nt(qs[n], kc[streams[n][1]]) for n in ss]
        if masked:
            kpos = base + lax.broadcasted_iota(jnp.int32, (1, tk), 1)
            kchunk = kpos >> CHUNK_SHIFT
            mask = [(kchunk <= (qpos[r] >> CHUNK_SHIFT)) & (kpos < n_valid) for r in range(rows)]
            s = [jnp.where(mask[streams[n][0]], s[n], -jnp.inf) for n in ss]
        chunks = [[s[n][:, t * LANES:(t + 1) * LANES] for t in range(nch)] for n in ss]
        m_prev = [m_scr[n] for n in ss]
        m_new = [jnp.maximum(m_prev[n], jnp.max(functools.reduce(jnp.maximum, chunks[n]), axis=1, keepdims=True))
                 for n in ss]
        alpha = [jnp.exp2(m_prev[n] - m_new[n]) for n in ss]
        ps = [[jnp.exp2((ch - m_new[n]).astype(BF16)) for ch in chunks[n]] for n in ss]
        for n in ss:
            tot = functools.reduce(jnp.add, ps[n]).astype(F32)
            l_scr[n] = alpha[n] * l_scr[n] + jnp.sum(tot, axis=1, keepdims=True)
            m_scr[n] = m_new[n]
        p = [jnp.concatenate(ps[n], axis=1) for n in ss]
        pv = [_dot(p[n], v) for n in ss]
        for n in ss:
            for t in range(nacc):
                sl = slice(t * LANES, (t + 1) * LANES)
                acc_scr[n, :, sl] = alpha[n] * acc_scr[n, :, sl] + pv[n][:, sl]

    def full_body(j, carry):
        block(j, False)
        return carry

    def edge_body(j, carry):
        block(j, True)
        return carry

    def pair_body(jj, carry):
        block(2 * jj, False)
        block(2 * jj + 1, False)
        return carry

    n_pairs = n_full // 2
    lax.fori_loop(0, n_pairs, pair_body, 0)
    lax.fori_loop(2 * n_pairs, n_full, full_body, 0)
    lax.fori_loop(n_full, n_blocks, edge_body, 0)

    lam = (jnp.exp(jnp.sum(lq1_ref[...] * lk1_ref[...], axis=1, keepdims=True))
           - jnp.exp(jnp.sum(lq2_ref[...] * lk2_ref[...], axis=1, keepdims=True)) + lam_init)
    for r in range(rows):
        parts = []
        for t in range(nacc):
            sl = slice(t * LANES, (t + 1) * LANES)
            parts.append(acc_scr[2 * r, :, sl] / l_scr[2 * r] - lam * (acc_scr[2 * r + 1, :, sl] / l_scr[2 * r + 1]))
        o = jnp.concatenate(parts, axis=1)
        o = o * lax.rsqrt(jnp.mean(o * o, axis=-1, keepdims=True) + EPS) * sw_ref[...]
        o_ref[0, r * rg:(r + 1) * rg, :] = o * (1.0 - lam_init)


def _attn_a(q, k, v, lq1, lk1, lq2, lk2, subln_w, *, tq, tk, rows, q_off, n_valid, lam_init):
    (qa, qs), (ka, ks), (va, vs) = q, k, v
    _, B, Lq, _ = qa.shape
    Lk = ka.shape[2]
    vec = pl.BlockSpec((1, HD_A), lambda b, h, i: (0, 0))
    return pl.pallas_call(
        functools.partial(_attn_a_kernel, tq=tq, tk=tk, rows=rows, q_off=q_off, n_valid=n_valid,
                          lam_init=lam_init),
        grid=(B, H_A, Lq // tq),
        in_specs=[
            pl.BlockSpec((1, 1, tq, 2 * HD_A), lambda b, h, i: (qs, b, i, h)),
            pl.BlockSpec((1, 1, Lk, 2 * HD_A), lambda b, h, i: (ks, b, 0, h)),
            pl.BlockSpec((1, 1, Lk, DV_A), lambda b, h, i: (vs, b, 0, h)),
            vec, vec, vec, vec,
            pl.BlockSpec((1, DV_A), lambda b, h, i: (0, 0)),
        ],
        out_specs=pl.BlockSpec((1, tq, DV_A), lambda b, h, i: (b, i, h)),
        out_shape=jax.ShapeDtypeStruct((B, Lq, H_A * DV_A), F32),
        scratch_shapes=[pltpu.VMEM((2 * rows, tq // rows, LANES), F32),
                        pltpu.VMEM((2 * rows, tq // rows, LANES), F32),
                        pltpu.VMEM((2 * rows, tq // rows, DV_A), F32)],
        compiler_params=_cparams(("parallel", "parallel", "arbitrary"), 40),
        name="attn_a",
    )(qa, ka, va, lq1.reshape(1, HD_A), lk1.reshape(1, HD_A), lq2.reshape(1, HD_A), lk2.reshape(1, HD_A),
      subln_w.reshape(1, DV_A))


C_HEADS = 4


def _attn_c_kernel(q_ref, k_ref, v_ref, o_ref, acc_scr, run_scr, *, tq, tk, q_off):
    i = pl.program_id(2)
    q = q_ref[0, 0]
    first_q = q_off + i * tq
    qpos = first_q + lax.broadcasted_iota(jnp.int32, (tq, 1), 0)
    j_start = (q_off + (i + 1) * tq - 2) // tk
    acc_scr[...] = jnp.zeros(acc_scr.shape, F32)
    run_scr[...] = jnp.zeros(run_scr.shape, F32)
    gs = range(C_HEADS)
    sl = [slice(g * HD_C, (g + 1) * HD_C) for g in gs]

    def cum_weights(w):
        rj = lax.broadcasted_iota(jnp.int32, (w, w + tk), 0)
        cs = lax.broadcasted_iota(jnp.int32, (w, w + tk), 1)
        return jnp.where((cs >= w) | (rj > cs), 1.0, 0.0).astype(BF16)

    def block(base, w, masked):
        cum_w = cum_weights(w)
        z = [_dot_nt(q[:, sl[g]], k_ref[0, 0, pl.ds(base, w), sl[g]]) for g in gs]
        t = [jnp.log(1.0 + jnp.exp(-jnp.abs(z[g]))) for g in gs]
        log_beta = [jnp.minimum(z[g], 0.0) - t[g] for g in gs]
        log_1m = [jnp.minimum(-z[g], 0.0) - t[g] for g in gs]
        if masked:
            mask = (base + lax.broadcasted_iota(jnp.int32, (1, w), 1)) < qpos
            log_1m = [jnp.where(mask, log_1m[g], 0.0) for g in gs]
        parts = [_split_bf16(log_1m[g]) for g in gs]
        cum = [_dot(parts[g][0], cum_w) + _dot(parts[g][1], cum_w) for g in gs]
        run = [run_scr[g] for g in gs]
        run_w = [jnp.concatenate([run[g]] * (w // tk), axis=1) for g in gs]
        a = [jnp.exp(log_beta[g] + cum[g][:, :w] + run_w[g]) for g in gs]
        if masked:
            a = [jnp.where(mask, a[g], 0.0) for g in gs]
        new_run = [run[g] + cum[g][:, w:] for g in gs]
        for g in gs:
            acc_scr[g] += _dot(a[g].astype(BF16), v_ref[0, 0, pl.ds(base, w), sl[g]])
            run_scr[g] = new_run[g]
        top = functools.reduce(jnp.maximum, new_run)
        return (jnp.max(top) > EXP_UNDERFLOW).astype(jnp.int32)

    wide = 2 * tk

    def edge_body(c):
        base = pl.multiple_of(c[0] - tk, tk)
        return base, block(base, tk, True)

    def wide_body(c):
        base = pl.multiple_of(c[0] - wide, tk)
        return base, block(base, wide, False)

    def last_body(c):
        base = pl.multiple_of(c[0] - tk, tk)
        return base, block(base, tk, False)

    c = ((j_start + 1) * tk, jnp.int32(1))
    c = lax.while_loop(lambda c: (c[0] > 0) & (c[1] > 0) & (c[0] > first_q), edge_body, c)
    c = lax.while_loop(lambda c: (c[0] >= wide) & (c[1] > 0), wide_body, c)
    lax.while_loop(lambda c: (c[0] > 0) & (c[1] > 0), last_body, c)
    for g in gs:
        o_ref[0, :, sl[g]] = acc_scr[g]


def _attn_c(q, k, v, *, tq, tk, q_off):
    (qa, qs), (ka, ks), (va, vs) = q, k, v
    _, B, Lq, _ = qa.shape
    Lk = ka.shape[2]
    w = C_HEADS * HD_C
    assert tk == LANES
    return pl.pallas_call(
        functools.partial(_attn_c_kernel, tq=tq, tk=tk, q_off=q_off),
        grid=(B, H_C // C_HEADS, Lq // tq),
        in_specs=[
            pl.BlockSpec((1, 1, tq, w), lambda b, g, i: (qs, b, i, g)),
            pl.BlockSpec((1, 1, Lk, w), lambda b, g, i: (ks, b, 0, g)),
            pl.BlockSpec((1, 1, Lk, w), lambda b, g, i: (vs, b, 0, g)),
        ],
        out_specs=pl.BlockSpec((1, tq, w), lambda b, g, i: (b, i, g)),
        out_shape=jax.ShapeDtypeStruct((B, Lq, H_C * HD_C), F32),
        scratch_shapes=[pltpu.VMEM((C_HEADS, tq, HD_C), F32), pltpu.VMEM((C_HEADS, tq, tk), F32)],
        compiler_params=_cparams(("parallel", "parallel", "arbitrary"), 48),
        name="attn_c",
    )(qa, ka, va)


CONV_CW = 512


def _conv_kernel(cur_ref, prev_ref, hist_ref, w_ref, o_ref, xs_scr, *, tl):
    i = pl.program_id(1)
    c = pl.program_id(2)
    xs_scr[0:SUBLANES, :] = jnp.where(i == 0, hist_ref[0], prev_ref[0, 0])
    xs_scr[SUBLANES:, :] = cur_ref[0, 0]
    w = w_ref[...]
    first = SUBLANES - (CONV_W - 1)
    conv = xs_scr[first:first + tl, :] * w[0:1, :]
    for t in range(1, CONV_W):
        conv = conv + xs_scr[first + t:first + t + tl, :] * w[t:t + 1, :]
    y = conv * _sigmoid(conv)
    blocks_per_part = W_B // CONV_CW
    is_qk = c < 2 * blocks_per_part
    qk_scale = jnp.where(c < blocks_per_part, HD_B ** -0.5, 1.0)
    for s in range(CONV_CW // HD_B):
        sl = y[:, s * HD_B:(s + 1) * HD_B]
        nrm = lax.rsqrt(jnp.sum(sl * sl, axis=-1, keepdims=True) + EPS) * qk_scale
        o_ref[0, :, s * HD_B:(s + 1) * HD_B] = sl * jnp.where(is_qk, nrm, 1.0)


def _conv_qkv(fseg, hist8, conv_w, tl):
    _, B, L, _ = fseg.shape
    N = 3 * W_B
    r = tl // SUBLANES
    per = PROJ_TN // CONV_CW
    return pl.pallas_call(
        functools.partial(_conv_kernel, tl=tl),
        grid=(B, L // tl, N // CONV_CW),
        in_specs=[
            pl.BlockSpec((1, 1, tl, CONV_CW), lambda b, i, c: (F_BQKV + c // per, b, i, c % per)),
            pl.BlockSpec((1, 1, SUBLANES, CONV_CW),
                         lambda b, i, c: (F_BQKV + c // per, b, jnp.maximum(i * r - 1, 0), c % per)),
            pl.BlockSpec((1, SUBLANES, CONV_CW), lambda b, i, c: (b, 0, c)),
            pl.BlockSpec((CONV_W, CONV_CW), lambda b, i, c: (0, c)),
        ],
        out_specs=pl.BlockSpec((1, tl, CONV_CW), lambda b, i, c: (b, i, c)),
        out_shape=jax.ShapeDtypeStruct((B, L, N), F32),
        scratch_shapes=[pltpu.VMEM((tl + SUBLANES, CONV_CW), F32)],
        compiler_params=_cparams(("parallel", "parallel", "parallel"), 32),
        name="gdn_conv",
    )(fseg, fseg, hist8, conv_w)


def _gates_kernel(sm_ref, alog_ref, dtb_ref, o_ref, *, r, valid):
    blk = pl.program_id(1)
    x = sm_ref[0]
    y = x + dtb_ref[...]
    g = -jnp.exp(alog_ref[...]) * (jnp.maximum(y, 0.0) + _softplus_neg_abs(y))
    beta = _sigmoid(x)
    row = blk * r + lax.broadcasted_iota(jnp.int32, (r, 1), 0)
    live = row < valid
    g = jnp.where(live, g, 0.0)
    beta = jnp.where(live, beta, 0.0)
    ri = lax.broadcasted_iota(jnp.int32, (r, r), 0)
    ci = lax.broadcasted_iota(jnp.int32, (r, r), 1)
    tri = jnp.where(((ri >> CHUNK_SHIFT) == (ci >> CHUNK_SHIFT)) & (ci <= ri), 1.0, 0.0)
    gc = jnp.dot(tri, g, precision=lax.Precision.HIGHEST, preferred_element_type=F32)
    lane = lax.broadcasted_iota(jnp.int32, (r, LANES), 1)
    o_ref[0] = jnp.where(lane < H_B, gc, beta)


def _gdn_gates(small, a_log, dt_bias, r, valid):
    B, L, _ = small.shape
    pad = lambda v: jnp.pad(v.astype(F32), (0, LANES - H_B)).reshape(1, LANES)
    return pl.pallas_call(
        functools.partial(_gates_kernel, r=r, valid=valid),
        grid=(B, L // r),
        in_specs=[
            pl.BlockSpec((1, r, LANES), lambda b, i: (b, i, 0)),
            pl.BlockSpec((1, LANES), lambda b, i: (0, 0)),
            pl.BlockSpec((1, LANES), lambda b, i: (0, 0)),
        ],
        out_specs=pl.BlockSpec((1, r, LANES), lambda b, i: (b, i, 0)),
        out_shape=jax.ShapeDtypeStruct((B, L, LANES), F32),
        compiler_params=_cparams(("parallel", "parallel"), 32),
        name="gdn_gates",
    )(small, pad(a_log), pad(dt_bias))


def _gdn_kernel(qkv_ref, gb_ref, gbt_ref, bz_ref, nw_ref, s0_ref, o_ref, s_out_ref, s_scr, *, r):
    blk = pl.program_id(1)
    nblk = pl.num_programs(1)

    @pl.when(blk == 0)
    def _():
        s_scr[...] = s0_ref[0]

    gb = gb_ref[0]
    gbt = gbt_ref[0]
    ri = lax.broadcasted_iota(jnp.int32, (r, r), 0)
    ci = lax.broadcasted_iota(jnp.int32, (r, r), 1)
    same = (ri >> CHUNK_SHIFT) == (ci >> CHUNK_SHIFT)
    tri = same & (ci <= ri)
    strict = same & (ci < ri)
    pair = (ri >> 1) == (ci >> 1)
    off_masks = [((ri >> (lg + 1)) == (ci >> (lg + 1))) & ((ri >> lg) != (ci >> lg))
                 for lg in range(1, CHUNK_SHIFT)]
    nw = nw_ref[...]

    hs = range(H_B)
    q = [qkv_ref[0, :, h * HD_B:(h + 1) * HD_B] for h in hs]
    k = [qkv_ref[0, :, (H_B + h) * HD_B:(H_B + h + 1) * HD_B] for h in hs]
    v = [qkv_ref[0, :, (2 * H_B + h) * HD_B:(2 * H_B + h + 1) * HD_B] for h in hs]
    gc_col = [gb[:, h:h + 1] for h in hs]
    beta_col = [gb[:, H_B + h:H_B + h + 1] for h in hs]
    decay = [jnp.where(tri, jnp.exp(jnp.where(tri, gc_col[h] - gbt[h:h + 1, :], 0.0)), 0.0) for h in hs]
    kbf = [k[h].astype(BF16) for h in hs]
    kb = [k[h] * beta_col[h] for h in hs]
    m = [jnp.where(strict, _dot_nt(kb[h].astype(BF16), kbf[h]) * decay[h], 0.0) for h in hs]
    x = [-jnp.where(pair, m[h], 0.0) for h in hs]
    for off_mask in off_masks:
        off = [jnp.where(off_mask, m[h], 0.0) for h in hs]
        xb = [x[h].astype(BF16) for h in hs]
        t = [off[h] + _dot(xb[h], off[h].astype(BF16)) for h in hs]
        x = [x[h] - t[h] - _dot(t[h].astype(BF16), xb[h]) for h in hs]
    egc = [jnp.exp(gc_col[h]) for h in hs]
    rhs = [jnp.concatenate([v[h] * beta_col[h], kb[h] * egc[h]], axis=1) for h in hs]
    sol = [rhs[h] + _dot(x[h].astype(BF16), rhs[h].astype(BF16)) for h in hs]
    aqk = [jnp.where(tri, _dot_nt(q[h].astype(BF16), kbf[h]) * decay[h], 0.0).astype(BF16) for h in hs]
    q_in = [(q[h] * egc[h]).astype(BF16) for h in hs]

    s = [s_scr[h] for h in hs]
    outs = [[] for _ in hs]
    for c in range(r // CHUNK):
        lo, hi = c * CHUNK, (c + 1) * CHUNK
        g_last = [gc_col[h][hi - 1:hi, :] for h in hs]
        sb = [s[h].astype(BF16) for h in hs]
        u = [sol[h][lo:hi, :HD_B] - _dot(sol[h][lo:hi, HD_B:].astype(BF16), sb[h]) for h in hs]
        ub = [u[h].astype(BF16) for h in hs]
        for h in hs:
            outs[h].append(_dot(q_in[h][lo:hi], sb[h]) + _dot(aqk[h][lo:hi, lo:hi], ub[h]))
        k_out = [(k[h][lo:hi] * jnp.exp(g_last[h] - gc_col[h][lo:hi])).astype(BF16) for h in hs]
        s = [s[h] * jnp.exp(g_last[h]) + _dot_tn(k_out[h], ub[h]) for h in hs]
    for h in hs:
        s_scr[h] = s[h]
        o = outs[h][0] if len(outs[h]) == 1 else jnp.concatenate(outs[h], axis=0)
        o = o * lax.rsqrt(jnp.mean(o * o, axis=-1, keepdims=True) + EPS) * nw
        z = bz_ref[0, 0, :, h * HD_B:(h + 1) * HD_B].astype(F32)
        o_ref[0, :, h * HD_B:(h + 1) * HD_B] = o * (z * _sigmoid(z))

    @pl.when(blk == nblk - 1)
    def _():
        s_out_ref[0] = s_scr[...]


def _gdn(qkv, gb, bz, norm_w, s0, r):
    B, L, _ = qkv.shape
    bza, bzs = bz
    gbt = jnp.swapaxes(gb[:, :, :2 * H_B], 1, 2)
    return pl.pallas_call(
        functools.partial(_gdn_kernel, r=r),
        grid=(B, L // r),
        in_specs=[
            pl.BlockSpec((1, r, 3 * W_B), lambda b, i: (b, i, 0)),
            pl.BlockSpec((1, r, LANES), lambda b, i: (b, i, 0)),
            pl.BlockSpec((1, 2 * H_B, r), lambda b, i: (b, 0, i)),
            pl.BlockSpec((1, 1, r, W_B), lambda b, i: (bzs, b, i, 0)),
            pl.BlockSpec((1, HD_B), lambda b, i: (0, 0)),
            pl.BlockSpec((1, H_B, HD_B, HD_B), lambda b, i: (b, 0, 0, 0)),
        ],
        out_specs=[
            pl.BlockSpec((1, r, W_B), lambda b, i: (b, i, 0)),
            pl.BlockSpec((1, H_B, HD_B, HD_B), lambda b, i: (b, 0, 0, 0)),
        ],
        out_shape=[jax.ShapeDtypeStruct((B, L, W_B), F32),
                   jax.ShapeDtypeStruct((B, H_B, HD_B, HD_B), F32)],
        scratch_shapes=[pltpu.VMEM((H_B, HD_B, HD_B), F32)],
        compiler_params=_cparams(("parallel", "arbitrary"), 40),
        name="gdn_delta",
    )(qkv, gb, gbt, bza, norm_w.reshape(1, HD_B), s0)


def _merge_kernel(x_ref, oa_ref, ob_ref, oc_ref, g0_ref, g1_ref, g2_ref, w_ref, nw_ref, x1_ref, h2_ref):
    merged = (_sigmoid(g0_ref[0].astype(F32)) * oa_ref[...]
              + _sigmoid(g1_ref[0].astype(F32)) * ob_ref[...]
              + _sigmoid(g2_ref[0].astype(F32)) * oc_ref[...])
    x1 = x_ref[...] + _dot(merged.astype(BF16), w_ref[...])
    x1_ref[...] = x1
    h2 = x1 * lax.rsqrt(jnp.mean(x1 * x1, axis=-1, keepdims=True) + EPS) * nw_ref[...]
    h2_ref[...] = h2.astype(BF16)


def _merge_out(x2d, oa, ob, oc, bseg, w_out_bf, norm2_w, tm):
    T = x2d.shape[0]
    row = pl.BlockSpec((tm, D_MODEL), lambda m: (m, 0))
    gate = lambda s: pl.BlockSpec((1, tm, D_MODEL), lambda m: (B_G + s, m, 0))
    return pl.pallas_call(
        _merge_kernel,
        grid=(T // tm,),
        in_specs=[row, row, row, row, gate(0), gate(1), gate(2),
                  pl.BlockSpec((D_MODEL, D_MODEL), lambda m: (0, 0)),
                  pl.BlockSpec((1, D_MODEL), lambda m: (0, 0))],
        out_specs=[row, row],
        out_shape=[jax.ShapeDtypeStruct((T, D_MODEL), F32), jax.ShapeDtypeStruct((T, D_MODEL), BF16)],
        compiler_params=_cparams(("parallel",), 48),
        name="merge_out",
    )(x2d, oa, ob, oc, bseg, bseg, bseg, w_out_bf, norm2_w.reshape(1, D_MODEL))


MLP_FC = 1024


def _mlp_kernel(h_ref, x_ref, wu_ref, wd_ref, fw_ref, o_ref, *, final):
    h = h_ref[...]
    acc = x_ref[...]
    for c in range(D_FF // MLP_FC):
        up = _dot(h, wu_ref[:, c * MLP_FC:(c + 1) * MLP_FC])
        up = jnp.square(jnp.maximum(up, 0.0))
        acc = acc + _dot(up.astype(BF16), wd_ref[c * MLP_FC:(c + 1) * MLP_FC, :])
    if final:
        acc = acc * lax.rsqrt(jnp.mean(acc * acc, axis=-1, keepdims=True) + EPS) * fw_ref[...]
    o_ref[...] = acc


def _mlp(h2, x1, w_up_bf, w_down_bf, final_w, tm, final):
    T = x1.shape[0]
    row = pl.BlockSpec((tm, D_MODEL), lambda m: (m, 0))
    return pl.pallas_call(
        functools.partial(_mlp_kernel, final=final),
        grid=(T // tm,),
        in_specs=[row, row,
                  pl.BlockSpec((D_MODEL, D_FF), lambda m: (0, 0)),
                  pl.BlockSpec((D_FF, D_MODEL), lambda m: (0, 0)),
                  pl.BlockSpec((1, D_MODEL), lambda m: (0, 0))],
        out_specs=row,
        out_shape=jax.ShapeDtypeStruct((T, D_MODEL), F32),
        compiler_params=_cparams(("parallel",), 56),
        name="mlp",
    )(h2, x1, w_up_bf, w_down_bf, final_w.reshape(1, D_MODEL))


N_LEAF = 4


def _leaves_kernel(*refs, depth):
    f_refs = refs[:depth]
    ak_ref, av_ref, ck_ref, cv_ref = refs[depth:]
    d = pl.program_id(0)
    for l in range(depth):
        @pl.when(d == l)
        def _(f_ref=f_refs[l]):
            for seg, o_ref in ((F_AK, ak_ref), (F_AV, av_ref), (F_CK, ck_ref), (F_CV, cv_ref)):
                o_ref[0] = f_ref[seg].reshape(o_ref.shape[1:])


def _cache_leaves(fsegs):
    depth = len(fsegs)
    T = fsegs[0].shape[1]
    tm = min(256, T)
    heads = ((2 * H_A, HD_A), (H_A, DV_A), (H_C, HD_C), (H_C, HD_C))
    return pl.pallas_call(
        functools.partial(_leaves_kernel, depth=depth),
        grid=(depth, T // tm),
        in_specs=[pl.BlockSpec((N_LEAF, tm, PROJ_TN), lambda d, m, l=l: (0, jnp.where(d == l, m, 0), 0))
                  for l in range(depth)],
        out_specs=[pl.BlockSpec((1, tm) + hd, lambda d, m: (d, m, 0, 0)) for hd in heads],
        out_shape=[jax.ShapeDtypeStruct((depth, T) + hd, F32) for hd in heads],
        compiler_params=_cparams(("arbitrary", "arbitrary"), 48),
        name="cache_leaves",
    )(*fsegs)


def _rope_tables(pos):
    half = HD_A // 2
    inv = ROPE_THETA ** (-jnp.arange(half, dtype=F32) / half)
    ang = pos.astype(F32)[:, None] * inv[None, :]
    cos, sin = jnp.cos(ang), jnp.sin(ang)
    return jnp.concatenate([cos, cos], axis=-1), jnp.concatenate([-sin, sin], axis=-1)


def _pad_rows(a, n):
    return jnp.pad(a, ((0, 0), (0, n - a.shape[1]), (0, 0)))


def _round_up(n, m):
    return -(-n // m) * m


def _layer(x, pos0, past, weights, layer_idx, final_w, final):
    (norm1_w, w_blocks, w_small, lq1, lk1, lq2, lk2, subln_w, conv_w, a_log, dt_bias,
     gdn_norm_w, w_out_bf, norm2_w, w_up_bf, w_down_bf) = weights
    B, L, _ = x.shape
    T = B * L
    prompt = past is None
    lam_init = 0.8 - 0.6 * math.exp(-0.3 * layer_idx)
    x2d = x.reshape(T, D_MODEL)

    cos_t, sin_t = _rope_tables(pos0 + jnp.arange(L))
    tm = min(1024, T)
    if L % tm:
        cos_t, sin_t = jnp.tile(cos_t, (B, 1)), jnp.tile(sin_t, (B, 1))
    fseg, bseg, small = _in_proj(x2d, norm1_w, w_blocks, w_small, layer_idx, cos_t, sin_t, tm)
    fseg4 = fseg.reshape(N_FSEG, B, L, PROJ_TN)
    bseg4 = bseg.reshape(N_BSEG, B, L, PROJ_TN)
    small = small.reshape(B, L, LANES)

    def with_past(past_arr, seg, lk):
        past_len = past_arr.shape[1]
        allk = jnp.concatenate([past_arr.reshape(B, past_len, -1).astype(BF16), bseg4[seg]], axis=1)
        return _pad_rows(allk, lk)[None], 0

    if prompt:
        oa = _attn_a((bseg4, B_AQ), (bseg4, B_AK), (bseg4, B_AV), lq1, lk1, lq2, lk2, subln_w,
                     tq=512, tk=512, rows=2, q_off=0, n_valid=L, lam_init=lam_init)
    else:
        past_len = past[0].shape[1]
        lk = _round_up(past_len + L, LANES)
        oa = _attn_a((bseg4, B_AQ), with_past(past[0], B_AK, lk), with_past(past[1], B_AV, lk),
                     lq1, lk1, lq2, lk2, subln_w, tq=L, tk=lk, rows=1, q_off=past_len, n_valid=past_len + L,
                     lam_init=lam_init)

    hist8 = jnp.zeros((B, SUBLANES, 3 * W_B), F32)
    if not prompt:
        hist8 = hist8.at[:, SUBLANES - (CONV_W - 1):].set(past[4])
    qkv = _conv_qkv(fseg4, hist8, conv_w, tl=min(512, L))
    tail = CONV_W - 1
    assert L >= tail
    new_conv = jnp.concatenate([fseg4[F_BQKV + s, :, L - tail:] for s in range(3)], axis=-1)
    if prompt:
        r = 128
        gb = _gdn_gates(small, a_log, dt_bias, r, L)
        s0 = jnp.zeros((B, H_B, HD_B, HD_B), F32)
        ob, s_new = _gdn(qkv, gb, (bseg4, B_BZ), gdn_norm_w, s0, r)
    else:
        lp = _round_up(L, CHUNK)
        gb = _gdn_gates(_pad_rows(small, lp), a_log, dt_bias, CHUNK, L)
        bz_pad = _pad_rows(bseg4[B_BZ], lp)[None]
        ob, s_new = _gdn(_pad_rows(qkv, lp), gb, (bz_pad, 0), gdn_norm_w, past[5], CHUNK)
        ob = ob[:, :L]

    if prompt:
        oc = _attn_c((bseg4, B_CQ), (bseg4, B_CK), (bseg4, B_CV), tq=128, tk=128, q_off=0)
    else:
        past_len = past[2].shape[1]
        lk = _round_up(past_len + L, LANES)
        oc = _attn_c((bseg4, B_CQ), with_past(past[2], B_CK, lk), with_past(past[3], B_CV, lk),
                     tq=L, tk=LANES, q_off=past_len)

    tm2 = min(256, T)
    f2 = lambda a: a.reshape(T, a.shape[-1])
    x1, h2 = _merge_out(x2d, f2(oa), f2(ob), f2(oc), bseg, w_out_bf, norm2_w, tm2)
    x2 = _mlp(h2, x1, w_up_bf, w_down_bf, final_w, tm2, final)
    return x2.reshape(B, L, D_MODEL), fseg, new_conv, s_new


N_ALIGNED = 6
SMALL_W = 2 * H_B
SMALL_BLOCK = N_ALIGNED * PROJ_TN // LANES


def _pack_src_block(n):
    return jnp.where(n < 3, n, jnp.where(n < 6, n + 4, jnp.where(n < 9, n - 3, jnp.where(n == 9, 6, n))))


def _pack_kernel(a_ref, b_ref, sm_ref, o_ref, osm_ref):
    n = pl.program_id(1)
    shifted = ((n >= 3) & (n < 6)) | (n >= 9)

    @pl.when(n == 0)
    def _():
        lane = lax.broadcasted_iota(jnp.int32, sm_ref.shape[1:], 1)
        osm_ref[0] = jnp.where(lane < SMALL_W, sm_ref[0], 0.0).astype(BF16)

    @pl.when(jnp.logical_not(shifted))
    def _():
        o_ref[0, 0] = a_ref[0].astype(BF16)

    @pl.when(shifted)
    def _():
        last = PROJ_TN - LANES
        o_ref[0, 0] = pltpu.roll(a_ref[0], PROJ_TN - SMALL_W, 1).astype(BF16)
        tail_a = pltpu.roll(a_ref[0, :, last:], LANES - SMALL_W, 1)
        tail_b = pltpu.roll(b_ref[0], LANES - SMALL_W, 1)
        lane = lax.broadcasted_iota(jnp.int32, tail_a.shape, 1)
        o_ref[0, 0, :, last:] = jnp.where(lane < LANES - SMALL_W, tail_a, tail_b).astype(BF16)


def _proj_weight_blocks(w_in):
    depth = w_in.shape[0]
    per = PROJ_TN // LANES
    return pl.pallas_call(
        _pack_kernel,
        grid=(depth, len(PROJ_GROUPS)),
        in_specs=[
            pl.BlockSpec((1, D_MODEL, PROJ_TN), lambda l, n: (l, 0, _pack_src_block(n))),
            pl.BlockSpec((1, D_MODEL, LANES), lambda l, n: (l, 0, (_pack_src_block(n) + 1) * per)),
            pl.BlockSpec((1, D_MODEL, LANES), lambda l, n: (l, 0, SMALL_BLOCK)),
        ],
        out_specs=[
            pl.BlockSpec((1, 1, D_MODEL, PROJ_TN), lambda l, n: (l, n, 0, 0)),
            pl.BlockSpec((1, D_MODEL, LANES), lambda l, n: (l, 0, 0)),
        ],
        out_shape=[jax.ShapeDtypeStruct((depth, len(PROJ_GROUPS), D_MODEL, PROJ_TN), BF16),
                   jax.ShapeDtypeStruct((depth, D_MODEL, LANES), BF16)],
        compiler_params=_cparams(("parallel", "arbitrary"), 40),
        name="pack_w_in",
    )(w_in, w_in, w_in)


def kernel(x_prompt, x_sample, cache_a_k, cache_a_v, cache_c_k, cache_c_v, state_b_conv, state_b_ssm,
           norm1_w, w_in, lam_q1, lam_k1, lam_q2, lam_k2, subln_w, conv_w, a_log, dt_bias,
           gdn_norm_w, w_out, norm2_w, w_up, w_down, final_norm_w):
    depth = w_in.shape[0]
    past_len = cache_a_k.shape[2]
    xp, xs = x_prompt, x_sample
    p_out = [[] for _ in range(3)]
    s_out = [[] for _ in range(3)]
    w_blocks, w_small = _proj_weight_blocks(w_in)
    for l in range(depth):
        weights = (norm1_w[l], w_blocks, w_small, lam_q1[l], lam_k1[l], lam_q2[l], lam_k2[l], subln_w[l],
                   conv_w[l], a_log[l], dt_bias[l], gdn_norm_w[l], w_out[l].astype(BF16), norm2_w[l],
                   w_up[l].astype(BF16), w_down[l].astype(BF16))
        final = l == depth - 1
        res = _layer(xp, 0, None, weights, l, final_norm_w, final)
        xp = res[0]
        for lst, a in zip(p_out, res[1:]):
            lst.append(a)
        past = (cache_a_k[l], cache_a_v[l], cache_c_k[l], cache_c_v[l], state_b_conv[l], state_b_ssm[l])
        res = _layer(xs, past_len, past, weights, l, final_norm_w, final)
        xs = res[0]
        for lst, a in zip(s_out, res[1:]):
            lst.append(a)

    def pack(lists, batch, length):
        fsegs, cb, sb = lists[0], jnp.stack(lists[1]), jnp.stack(lists[2])
        ak, av, ck, cv = _cache_leaves(fsegs)
        lead = (depth, batch, length)
        return (ak.reshape(lead + ak.shape[2:]), av.reshape(lead + av.shape[2:]),
                ck.reshape(lead + ck.shape[2:]), cv.reshape(lead + cv.shape[2:]), cb, sb)

    bp, lp = x_prompt.shape[0], x_prompt.shape[1]
    bs, ls = x_sample.shape[0], x_sample.shape[1]
    return (xp, xs) + pack(p_out, bp, lp) + pack(s_out, bs, ls)
```

```python
import functools
import math

import jax
import jax.numpy as jnp
from jax import lax
from jax.experimental import pallas as pl
from jax.experimental.pallas import tpu as pltpu

F32 = jnp.float32
BF16 = jnp.bfloat16

D_MODEL = 1024
CHUNK = 64
CHUNK_SHIFT = 6
H_A, HD_A, DV_A = 4, 128, 256
H_B, HD_B = 8, 128
H_C, HD_C = 8, 128
W_B = H_B * HD_B
CONV_W = 4
D_FF = 4 * D_MODEL
ROPE_THETA = 10000.0
EPS = 1e-6
LANES = 128
SUBLANES = 8
MIB = 1024 * 1024
LOG2E = 1.4426950408889634
EXP_UNDERFLOW = -104.0


def _cparams(sem, vmem_mib):
    return pltpu.CompilerParams(dimension_semantics=sem, vmem_limit_bytes=vmem_mib * MIB)


def _sigmoid(x):
    return 1.0 / (1.0 + jnp.exp(-x))


def _softplus_neg_abs(x):
    return jnp.log1p(jnp.exp(-jnp.abs(x)))


def _dot(a, b):
    return jnp.dot(a, b, preferred_element_type=F32)


def _split_bf16(a):
    hi = a.astype(BF16)
    return hi, (a - hi.astype(F32)).astype(BF16)


def _dot3(a, b):
    ah, al = _split_bf16(a)
    bh, bl = _split_bf16(b)
    return _dot(ah, bh) + _dot(ah, bl) + _dot(al, bh)


def _dot_nt(a, b):
    return lax.dot_general(a, b, (((1,), (1,)), ((), ())), preferred_element_type=F32)


def _dot_tn(a, b):
    return lax.dot_general(a, b, (((0,), (0,)), ((), ())), preferred_element_type=F32)


PROJ_TN = 1024
PROJ_GROUPS = ("aq", "ak", "av", "cq", "ck", "cv", "bqkv0", "bqkv1", "bqkv2", "bz", "g0", "g1", "g2")
N_BQKV = 3
B_AQ, B_AK, B_AV, B_CQ, B_CK, B_CV, B_BZ, B_G = 0, 1, 2, 3, 4, 5, 6, 7
N_BSEG = 10
CACHE_HEADS = ((2 * H_A, HD_A), (H_A, DV_A), (H_C, HD_C), (H_C, HD_C))
QA_SCALE = HD_A ** -0.5 * LOG2E
QC_SCALE = HD_C ** -0.5


def _proj_f_index(n):
    return jnp.clip(n - 6, 0, N_BQKV - 1)


def _proj_b_index(n):
    return jnp.where(n <= 5, n, jnp.where(n <= 8, 5, n - 3))


def _proj_kernel(x_ref, nw_ref, w_ref, wsm_ref, cos_ref, sin_ref, *refs):
    ak_ref, av_ref, ck_ref, cv_ref, f_ref, b_ref, sm_ref, h_scr = refs[len(CACHE_HEADS):]
    n = pl.program_id(1)

    @pl.when(n == 0)
    def _():
        x = x_ref[...]
        y = x * lax.rsqrt(jnp.mean(x * x, axis=-1, keepdims=True) + EPS) * nw_ref[...]
        hb = y.astype(BF16)
        h_scr[...] = hb
        sm_ref[...] = _dot(hb, wsm_ref[0])

    def mm():
        return _dot(h_scr[...], w_ref[0, 0])

    def rope(acc):
        cos = cos_ref[...]
        sin = sin_ref[...]
        slabs = [acc[:, s * LANES:(s + 1) * LANES] for s in range(PROJ_TN // LANES)]
        return jnp.concatenate([sl * cos + pltpu.roll(sl, LANES // 2, 1) * sin for sl in slabs], axis=1)

    def cache_out(o_ref, a):
        o_ref[0] = a.reshape(o_ref.shape[1:])
        b_ref[0] = a.astype(BF16)

    @pl.when(n == 0)
    def _():
        b_ref[0] = (rope(mm()) * QA_SCALE).astype(BF16)

    @pl.when(n == 1)
    def _():
        cache_out(ak_ref, rope(mm()))

    @pl.when(n == 2)
    def _():
        cache_out(av_ref, mm())

    @pl.when(n == 3)
    def _():
        b_ref[0] = (mm() * QC_SCALE).astype(BF16)

    @pl.when(n == 4)
    def _():
        cache_out(ck_ref, mm())

    @pl.when(n == 5)
    def _():
        cache_out(cv_ref, mm())

    @pl.when((n >= 6) & (n <= 8))
    def _():
        f_ref[0] = mm()

    @pl.when(n >= 9)
    def _():
        b_ref[0] = mm().astype(BF16)


def _in_proj(x2d, norm_w, w_blocks, w_small, layer, cos_t, sin_t, caches, tm):
    T = x2d.shape[0]
    tbl_blocks = cos_t.shape[0] // tm
    n_fixed = 6
    cache_specs = [pl.BlockSpec((1, tm) + hd, lambda m, n: (layer, m, 0, 0)) for hd in CACHE_HEADS]
    outs = pl.pallas_call(
        _proj_kernel,
        grid=(T // tm, len(PROJ_GROUPS)),
        in_specs=[
            pl.BlockSpec((tm, D_MODEL), lambda m, n: (m, 0)),
            pl.BlockSpec((1, D_MODEL), lambda m, n: (0, 0)),
            pl.BlockSpec((1, 1, D_MODEL, PROJ_TN), lambda m, n: (layer, n, 0, 0)),
            pl.BlockSpec((1, D_MODEL, LANES), lambda m, n: (layer, 0, 0)),
            pl.BlockSpec((tm, LANES), lambda m, n: (m % tbl_blocks, 0)),
            pl.BlockSpec((tm, LANES), lambda m, n: (m % tbl_blocks, 0)),
        ] + [pl.BlockSpec(memory_space=pl.ANY)] * len(CACHE_HEADS),
        out_specs=cache_specs + [
            pl.BlockSpec((1, tm, PROJ_TN), lambda m, n: (_proj_f_index(n), m, 0)),
            pl.BlockSpec((1, tm, PROJ_TN), lambda m, n: (_proj_b_index(n), m, 0)),
            pl.BlockSpec((tm, LANES), lambda m, n: (m, 0)),
        ],
        out_shape=[jax.ShapeDtypeStruct(c.shape, c.dtype) for c in caches] + [
            jax.ShapeDtypeStruct((N_BQKV, T, PROJ_TN), F32),
            jax.ShapeDtypeStruct((N_BSEG, T, PROJ_TN), BF16),
            jax.ShapeDtypeStruct((T, LANES), F32)],
        input_output_aliases={n_fixed + i: i for i in range(len(CACHE_HEADS))},
        scratch_shapes=[pltpu.VMEM((tm, D_MODEL), BF16)],
        compiler_params=_cparams(("parallel", "arbitrary"), 52),
        name="in_proj",
    )(x2d, norm_w.reshape(1, D_MODEL), w_blocks, w_small, cos_t, sin_t, *caches)
    return outs[:len(CACHE_HEADS)], outs[-3], outs[-2], outs[-1]


def _attn_a_kernel(q_ref, k_ref, v_ref, lq1_ref, lk1_ref, lq2_ref, lk2_ref, sw_ref, o_ref,
                   m_scr, l_scr, acc_scr, *, tq, tk, rows, q_off, n_valid, lam_init):
    i = pl.program_id(2)
    q = q_ref[0, 0]
    first_q = q_off + i * tq
    last_q = first_q + tq - 1
    vis_first = jnp.minimum(((first_q >> CHUNK_SHIFT) + 1) << CHUNK_SHIFT, n_valid)
    vis_last = jnp.minimum(((last_q >> CHUNK_SHIFT) + 1) << CHUNK_SHIFT, n_valid)
    n_full = vis_first // tk
    n_blocks = (vis_last + tk - 1) // tk

    m_scr[...] = jnp.full(m_scr.shape, -jnp.inf, F32)
    l_scr[...] = jnp.zeros(l_scr.shape, F32)
    acc_scr[...] = jnp.zeros(acc_scr.shape, F32)
    rg = tq // rows
    streams = [(r, c) for r in range(rows) for c in range(2)]
    ss = range(len(streams))
    qs = [q[r * rg:(r + 1) * rg, c * HD_A:(c + 1) * HD_A] for r, c in streams]
    qpos = [first_q + r * rg + lax.broadcasted_iota(jnp.int32, (rg, 1), 0) for r in range(rows)]
    nch = tk // LANES
    nacc = DV_A // LANES

    def block(j, masked):
        base = pl.multiple_of(j * tk, tk)
        k = k_ref[0, 0, pl.ds(base, tk), :]
        v = v_ref[0, 0, pl.ds(base, tk), :]
        kc = [k[:, c * HD_A:(c + 1) * HD_A] for c in range(2)]
        s = [_dot_nt(qs[n], kc[streams[n][1]]) for n in ss]
        if masked:
            kpos = base + lax.broadcasted_iota(jnp.int32, (1, tk), 1)
            kchunk = kpos >> CHUNK_SHIFT
            mask = [(kchunk <= (qpos[r] >> CHUNK_SHIFT)) & (kpos < n_valid) for r in range(rows)]
            s = [jnp.where(mask[streams[n][0]], s[n], -jnp.inf) for n in ss]
        chunks = [[s[n][:, t * LANES:(t + 1) * LANES] for t in range(nch)] for n in ss]
        m_prev = [m_scr[n] for n in ss]
        m_new = [jnp.maximum(m_prev[n], jnp.max(functools.reduce(jnp.maximum, chunks[n]), axis=1, keepdims=True))
                 for n in ss]
        alpha = [jnp.exp2(m_prev[n] - m_new[n]) for n in ss]
        ps = [[jnp.exp2((ch - m_new[n]).astype(BF16)) for ch in chunks[n]] for n in ss]
        for n in ss:
            tot = functools.reduce(jnp.add, ps[n]).astype(F32)
            l_scr[n] = alpha[n] * l_scr[n] + jnp.sum(tot, axis=1, keepdims=True)
            m_scr[n] = m_new[n]
        p = [jnp.concatenate(ps[n], axis=1) for n in ss]
        pv = [_dot(p[n], v) for n in ss]
        for n in ss:
            for t in range(nacc):
                sl = slice(t * LANES, (t + 1) * LANES)
                acc_scr[n, :, sl] = alpha[n] * acc_scr[n, :, sl] + pv[n][:, sl]

    def full_body(j, carry):
        block(j, False)
        return carry

    def edge_body(j, carry):
        block(j, True)
        return carry

    def pair_body(jj, carry):
        block(2 * jj, False)
        block(2 * jj + 1, False)
        return carry

    n_pairs = n_full // 2
    lax.fori_loop(0, n_pairs, pair_body, 0)
    lax.fori_loop(2 * n_pairs, n_full, full_body, 0)
    lax.fori_loop(n_full, n_blocks, edge_body, 0)

    lam = (jnp.exp(jnp.sum(lq1_ref[...] * lk1_ref[...], axis=1, keepdims=True))
           - jnp.exp(jnp.sum(lq2_ref[...] * lk2_ref[...], axis=1, keepdims=True)) + lam_init)
    for r in range(rows):
        parts = []
        for t in range(nacc):
            sl = slice(t * LANES, (t + 1) * LANES)
            parts.append(acc_scr[2 * r, :, sl] / l_scr[2 * r] - lam * (acc_scr[2 * r + 1, :, sl] / l_scr[2 * r + 1]))
        o = jnp.concatenate(parts, axis=1)
        o = o * lax.rsqrt(jnp.mean(o * o, axis=-1, keepdims=True) + EPS) * sw_ref[...]
        o_ref[0, r * rg:(r + 1) * rg, :] = o * (1.0 - lam_init)


def _attn_a(q, k, v, lq1, lk1, lq2, lk2, subln_w, *, tq, tk, rows, q_off, n_valid, lam_init):
    (qa, qs), (ka, ks), (va, vs) = q, k, v
    _, B, Lq, _ = qa.shape
    Lk = ka.shape[2]
    vec = pl.BlockSpec((1, HD_A), lambda b, h, i: (0, 0))
    return pl.pallas_call(
        functools.partial(_attn_a_kernel, tq=tq, tk=tk, rows=rows, q_off=q_off, n_valid=n_valid,
                          lam_init=lam_init),
        grid=(B, H_A, Lq // tq),
        in_specs=[
            pl.BlockSpec((1, 1, tq, 2 * HD_A), lambda b, h, i: (qs, b, i, h)),
            pl.BlockSpec((1, 1, Lk, 2 * HD_A), lambda b, h, i: (ks, b, 0, h)),
            pl.BlockSpec((1, 1, Lk, DV_A), lambda b, h, i: (vs, b, 0, h)),
            vec, vec, vec, vec,
            pl.BlockSpec((1, DV_A), lambda b, h, i: (0, 0)),
        ],
        out_specs=pl.BlockSpec((1, tq, DV_A), lambda b, h, i: (b, i, h)),
        out_shape=jax.ShapeDtypeStruct((B, Lq, H_A * DV_A), F32),
        scratch_shapes=[pltpu.VMEM((2 * rows, tq // rows, LANES), F32),
                        pltpu.VMEM((2 * rows, tq // rows, LANES), F32),
                        pltpu.VMEM((2 * rows, tq // rows, DV_A), F32)],
        compiler_params=_cparams(("parallel", "parallel", "arbitrary"), 40),
        name="attn_a",
    )(qa, ka, va, lq1.reshape(1, HD_A), lk1.reshape(1, HD_A), lq2.reshape(1, HD_A), lk2.reshape(1, HD_A),
      subln_w.reshape(1, DV_A))


C_HEADS = 4


def _attn_c_kernel(q_ref, k_ref, v_ref, o_ref, acc_scr, run_scr, *, tq, tk, q_off):
    i = pl.program_id(2)
    q = q_ref[0, 0]
    first_q = q_off + i * tq
    qpos = first_q + lax.broadcasted_iota(jnp.int32, (tq, 1), 0)
    j_start = (q_off + (i + 1) * tq - 2) // tk
    acc_scr[...] = jnp.zeros(acc_scr.shape, F32)
    run_scr[...] = jnp.zeros(run_scr.shape, F32)
    gs = range(C_HEADS)
    sl = [slice(g * HD_C, (g + 1) * HD_C) for g in gs]

    def cum_weights(w):
        rj = lax.broadcasted_iota(jnp.int32, (w, w + tk), 0)
        cs = lax.broadcasted_iota(jnp.int32, (w, w + tk), 1)
        return jnp.where((cs >= w) | (rj > cs), 1.0, 0.0).astype(BF16)

    def pre(base, w, masked):
        cum_w = cum_weights(w)
        z = [_dot_nt(q[:, sl[g]], k_ref[0, 0, pl.ds(base, w), sl[g]]) for g in gs]
        t = [jnp.log(1.0 + jnp.exp(-jnp.abs(z[g]))) for g in gs]
        log_beta = [jnp.minimum(z[g], 0.0) - t[g] for g in gs]
        log_1m = [jnp.minimum(-z[g], 0.0) - t[g] for g in gs]
        mask = None
        if masked:
            mask = (base + lax.broadcasted_iota(jnp.int32, (1, w), 1)) < qpos
            log_1m = [jnp.where(mask, log_1m[g], 0.0) for g in gs]
        parts = [_split_bf16(log_1m[g]) for g in gs]
        cum = [_dot(parts[g][0], cum_w) + _dot(parts[g][1], cum_w) for g in gs]
        return base, w, mask, log_beta, cum

    def post(state):
        base, w, mask, log_beta, cum = state
        run = [run_scr[g] for g in gs]
        run_w = [jnp.concatenate([run[g]] * (w // tk), axis=1) for g in gs]
        a = [jnp.exp(log_beta[g] + cum[g][:, :w] + run_w[g]) for g in gs]
        if mask is not None:
            a = [jnp.where(mask, a[g], 0.0) for g in gs]
        new_run = [run[g] + cum[g][:, w:] for g in gs]
        for g in gs:
            acc_scr[g] += _dot(a[g].astype(BF16), v_ref[0, 0, pl.ds(base, w), sl[g]])
            run_scr[g] = new_run[g]
        top = functools.reduce(jnp.maximum, new_run)
        return (jnp.max(top) > EXP_UNDERFLOW).astype(jnp.int32)

    def block(base, w, masked):
        return post(pre(base, w, masked))

    wide = 2 * tk
    end0 = (j_start + 1) * tk

    def fused_start(_):
        base_e = pl.multiple_of(end0 - tk, tk)
        base_w = pl.multiple_of(end0 - tk - wide, tk)
        edge = pre(base_e, tk, True)
        left = pre(base_w, wide, False)
        post(edge)
        return base_w, post(left)

    def edge_body(c):
        base = pl.multiple_of(c[0] - tk, tk)
        return base, block(base, tk, True)

    def wide_body(c):
        base = pl.multiple_of(c[0] - wide, tk)
        return base, block(base, wide, False)

    def last_body(c):
        base = pl.multiple_of(c[0] - tk, tk)
        return base, block(base, tk, False)

    one_edge_then_wide = (end0 - tk <= first_q) & (end0 - tk >= wide)
    c = lax.cond(one_edge_then_wide, fused_start, lambda _: (end0, jnp.int32(1)), 0)
    c = lax.while_loop(lambda c: (c[0] > 0) & (c[1] > 0) & (c[0] > first_q), edge_body, c)
    c = lax.while_loop(lambda c: (c[0] >= wide) & (c[1] > 0), wide_body, c)
    lax.while_loop(lambda c: (c[0] > 0) & (c[1] > 0), last_body, c)
    for g in gs:
        o_ref[0, :, sl[g]] = acc_scr[g]


def _attn_c(q, k, v, *, tq, tk, q_off):
    (qa, qs), (ka, ks), (va, vs) = q, k, v
    _, B, Lq, _ = qa.shape
    Lk = ka.shape[2]
    w = C_HEADS * HD_C
    assert tk == LANES
    return pl.pallas_call(
        functools.partial(_attn_c_kernel, tq=tq, tk=tk, q_off=q_off),
        grid=(B, H_C // C_HEADS, Lq // tq),
        in_specs=[
            pl.BlockSpec((1, 1, tq, w), lambda b, g, i: (qs, b, i, g)),
            pl.BlockSpec((1, 1, Lk, w), lambda b, g, i: (ks, b, 0, g)),
            pl.BlockSpec((1, 1, Lk, w), lambda b, g, i: (vs, b, 0, g)),
        ],
        out_specs=pl.BlockSpec((1, tq, w), lambda b, g, i: (b, i, g)),
        out_shape=jax.ShapeDtypeStruct((B, Lq, H_C * HD_C), F32),
        scratch_shapes=[pltpu.VMEM((C_HEADS, tq, HD_C), F32), pltpu.VMEM((C_HEADS, tq, tk), F32)],
        compiler_params=_cparams(("parallel", "parallel", "arbitrary"), 48),
        name="attn_c",
    )(qa, ka, va)


CONV_CW = 512


def _conv_kernel(cur_ref, prev_ref, hist_ref, w_ref, o_ref, xs_scr, *, tl):
    i = pl.program_id(1)
    c = pl.program_id(2)
    xs_scr[0:SUBLANES, :] = jnp.where(i == 0, hist_ref[0], prev_ref[0, 0])
    xs_scr[SUBLANES:, :] = cur_ref[0, 0]
    w = w_ref[...]
    first = SUBLANES - (CONV_W - 1)
    conv = xs_scr[first:first + tl, :] * w[0:1, :]
    for t in range(1, CONV_W):
        conv = conv + xs_scr[first + t:first + t + tl, :] * w[t:t + 1, :]
    y = conv * _sigmoid(conv)
    blocks_per_part = W_B // CONV_CW
    is_qk = c < 2 * blocks_per_part
    qk_scale = jnp.where(c < blocks_per_part, HD_B ** -0.5, 1.0)
    for s in range(CONV_CW // HD_B):
        sl = y[:, s * HD_B:(s + 1) * HD_B]
        nrm = lax.rsqrt(jnp.sum(sl * sl, axis=-1, keepdims=True) + EPS) * qk_scale
        o_ref[0, :, s * HD_B:(s + 1) * HD_B] = sl * jnp.where(is_qk, nrm, 1.0)


def _conv_qkv(fseg, hist8, conv_w, tl):
    _, B, L, _ = fseg.shape
    N = 3 * W_B
    r = tl // SUBLANES
    per = PROJ_TN // CONV_CW
    return pl.pallas_call(
        functools.partial(_conv_kernel, tl=tl),
        grid=(B, L // tl, N // CONV_CW),
        in_specs=[
            pl.BlockSpec((1, 1, tl, CONV_CW), lambda b, i, c: (c // per, b, i, c % per)),
            pl.BlockSpec((1, 1, SUBLANES, CONV_CW),
                         lambda b, i, c: (c // per, b, jnp.maximum(i * r - 1, 0), c % per)),
            pl.BlockSpec((1, SUBLANES, CONV_CW), lambda b, i, c: (b, 0, c)),
            pl.BlockSpec((CONV_W, CONV_CW), lambda b, i, c: (0, c)),
        ],
        out_specs=pl.BlockSpec((1, tl, CONV_CW), lambda b, i, c: (b, i, c)),
        out_shape=jax.ShapeDtypeStruct((B, L, N), F32),
        scratch_shapes=[pltpu.VMEM((tl + SUBLANES, CONV_CW), F32)],
        compiler_params=_cparams(("parallel", "parallel", "parallel"), 32),
        name="gdn_conv",
    )(fseg, fseg, hist8, conv_w)


def _gates_kernel(sm_ref, alog_ref, dtb_ref, o_ref, *, r, valid):
    blk = pl.program_id(1)
    x = sm_ref[0]
    y = x + dtb_ref[...]
    g = -jnp.exp(alog_ref[...]) * (jnp.maximum(y, 0.0) + _softplus_neg_abs(y))
    beta = _sigmoid(x)
    row = blk * r + lax.broadcasted_iota(jnp.int32, (r, 1), 0)
    live = row < valid
    g = jnp.where(live, g, 0.0)
    beta = jnp.where(live, beta, 0.0)
    ri = lax.broadcasted_iota(jnp.int32, (r, r), 0)
    ci = lax.broadcasted_iota(jnp.int32, (r, r), 1)
    tri = jnp.where(((ri >> CHUNK_SHIFT) == (ci >> CHUNK_SHIFT)) & (ci <= ri), 1.0, 0.0)
    gc = jnp.dot(tri, g, precision=lax.Precision.HIGHEST, preferred_element_type=F32)
    lane = lax.broadcasted_iota(jnp.int32, (r, LANES), 1)
    o_ref[0] = jnp.where(lane < H_B, gc, beta)


def _gdn_gates(small, a_log, dt_bias, r, valid):
    B, L, _ = small.shape
    pad = lambda v: jnp.pad(v.astype(F32), (0, LANES - H_B)).reshape(1, LANES)
    return pl.pallas_call(
        functools.partial(_gates_kernel, r=r, valid=valid),
        grid=(B, L // r),
        in_specs=[
            pl.BlockSpec((1, r, LANES), lambda b, i: (b, i, 0)),
            pl.BlockSpec((1, LANES), lambda b, i: (0, 0)),
            pl.BlockSpec((1, LANES), lambda b, i: (0, 0)),
        ],
        out_specs=pl.BlockSpec((1, r, LANES), lambda b, i: (b, i, 0)),
        out_shape=jax.ShapeDtypeStruct((B, L, LANES), F32),
        compiler_params=_cparams(("parallel", "parallel"), 32),
        name="gdn_gates",
    )(small, pad(a_log), pad(dt_bias))


def _gdn_kernel(qkv_ref, gb_ref, gbt_ref, bz_ref, nw_ref, s0_ref, o_ref, s_out_ref, s_scr, *, r):
    blk = pl.program_id(1)
    nblk = pl.num_programs(1)

    @pl.when(blk == 0)
    def _():
        s_scr[...] = s0_ref[0]

    gb = gb_ref[0]
    gbt = gbt_ref[0]
    ri = lax.broadcasted_iota(jnp.int32, (r, r), 0)
    ci = lax.broadcasted_iota(jnp.int32, (r, r), 1)
    same = (ri >> CHUNK_SHIFT) == (ci >> CHUNK_SHIFT)
    tri = same & (ci <= ri)
    strict = same & (ci < ri)
    pair = (ri >> 1) == (ci >> 1)
    off_masks = [((ri >> (lg + 1)) == (ci >> (lg + 1))) & ((ri >> lg) != (ci >> lg))
                 for lg in range(1, CHUNK_SHIFT)]
    nw = nw_ref[...]

    hs = range(H_B)
    q = [qkv_ref[0, :, h * HD_B:(h + 1) * HD_B] for h in hs]
    k = [qkv_ref[0, :, (H_B + h) * HD_B:(H_B + h + 1) * HD_B] for h in hs]
    v = [qkv_ref[0, :, (2 * H_B + h) * HD_B:(2 * H_B + h + 1) * HD_B] for h in hs]
    gc_col = [gb[:, h:h + 1] for h in hs]
    beta_col = [gb[:, H_B + h:H_B + h + 1] for h in hs]
    decay = [jnp.where(tri, jnp.exp(jnp.where(tri, gc_col[h] - gbt[h:h + 1, :], 0.0)), 0.0) for h in hs]
    kbf = [k[h].astype(BF16) for h in hs]
    kb = [k[h] * beta_col[h] for h in hs]
    m = [jnp.where(strict, _dot_nt(kb[h].astype(BF16), kbf[h]) * decay[h], 0.0) for h in hs]
    x = [-jnp.where(pair, m[h], 0.0) for h in hs]
    for off_mask in off_masks:
        off = [jnp.where(off_mask, m[h], 0.0) for h in hs]
        xb = [x[h].astype(BF16) for h in hs]
        t = [off[h] + _dot(xb[h], off[h].astype(BF16)) for h in hs]
        x = [x[h] - t[h] - _dot(t[h].astype(BF16), xb[h]) for h in hs]
    egc = [jnp.exp(gc_col[h]) for h in hs]
    rhs = [jnp.concatenate([v[h] * beta_col[h], kb[h] * egc[h]], axis=1) for h in hs]
    sol = [rhs[h] + _dot(x[h].astype(BF16), rhs[h].astype(BF16)) for h in hs]
    aqk = [jnp.where(tri, _dot_nt(q[h].astype(BF16), kbf[h]) * decay[h], 0.0).astype(BF16) for h in hs]
    q_in = [(q[h] * egc[h]).astype(BF16) for h in hs]

    s = [s_scr[h] for h in hs]
    outs = [[] for _ in hs]
    for c in range(r // CHUNK):
        lo, hi = c * CHUNK, (c + 1) * CHUNK
        g_last = [gc_col[h][hi - 1:hi, :] for h in hs]
        sb = [s[h].astype(BF16) for h in hs]
        u = [sol[h][lo:hi, :HD_B] - _dot(sol[h][lo:hi, HD_B:].astype(BF16), sb[h]) for h in hs]
        ub = [u[h].astype(BF16) for h in hs]
        for h in hs:
            outs[h].append(_dot(q_in[h][lo:hi], sb[h]) + _dot(aqk[h][lo:hi, lo:hi], ub[h]))
        k_out = [(k[h][lo:hi] * jnp.exp(g_last[h] - gc_col[h][lo:hi])).astype(BF16) for h in hs]
        s = [s[h] * jnp.exp(g_last[h]) + _dot_tn(k_out[h], ub[h]) for h in hs]
    for h in hs:
        s_scr[h] = s[h]
        o = outs[h][0] if len(outs[h]) == 1 else jnp.concatenate(outs[h], axis=0)
        o = o * lax.rsqrt(jnp.mean(o * o, axis=-1, keepdims=True) + EPS) * nw
        z = bz_ref[0, 0, :, h * HD_B:(h + 1) * HD_B].astype(F32)
        o_ref[0, :, h * HD_B:(h + 1) * HD_B] = o * (z * _sigmoid(z))

    @pl.when(blk == nblk - 1)
    def _():
        s_out_ref[0] = s_scr[...]


def _gdn(qkv, gb, bz, norm_w, s0, r):
    B, L, _ = qkv.shape
    bza, bzs = bz
    gbt = jnp.swapaxes(gb[:, :, :2 * H_B], 1, 2)
    return pl.pallas_call(
        functools.partial(_gdn_kernel, r=r),
        grid=(B, L // r),
        in_specs=[
            pl.BlockSpec((1, r, 3 * W_B), lambda b, i: (b, i, 0)),
            pl.BlockSpec((1, r, LANES), lambda b, i: (b, i, 0)),
            pl.BlockSpec((1, 2 * H_B, r), lambda b, i: (b, 0, i)),
            pl.BlockSpec((1, 1, r, W_B), lambda b, i: (bzs, b, i, 0)),
            pl.BlockSpec((1, HD_B), lambda b, i: (0, 0)),
            pl.BlockSpec((1, H_B, HD_B, HD_B), lambda b, i: (b, 0, 0, 0)),
        ],
        out_specs=[
            pl.BlockSpec((1, r, W_B), lambda b, i: (b, i, 0)),
            pl.BlockSpec((1, H_B, HD_B, HD_B), lambda b, i: (b, 0, 0, 0)),
        ],
        out_shape=[jax.ShapeDtypeStruct((B, L, W_B), F32),
                   jax.ShapeDtypeStruct((B, H_B, HD_B, HD_B), F32)],
        scratch_shapes=[pltpu.VMEM((H_B, HD_B, HD_B), F32)],
        compiler_params=_cparams(("parallel", "arbitrary"), 40),
        name="gdn_delta",
    )(qkv, gb, gbt, bza, norm_w.reshape(1, HD_B), s0)


def _merge_kernel(x_ref, oa_ref, ob_ref, oc_ref, g0_ref, g1_ref, g2_ref, w_ref, nw_ref, x1_ref, h2_ref):
    merged = (_sigmoid(g0_ref[0].astype(F32)) * oa_ref[...]
              + _sigmoid(g1_ref[0].astype(F32)) * ob_ref[...]
              + _sigmoid(g2_ref[0].astype(F32)) * oc_ref[...])
    x1 = x_ref[...] + _dot(merged.astype(BF16), w_ref[...])
    x1_ref[...] = x1
    h2 = x1 * lax.rsqrt(jnp.mean(x1 * x1, axis=-1, keepdims=True) + EPS) * nw_ref[...]
    h2_ref[...] = h2.astype(BF16)


def _merge_out(x2d, oa, ob, oc, bseg, w_out_bf, norm2_w, tm):
    T = x2d.shape[0]
    row = pl.BlockSpec((tm, D_MODEL), lambda m: (m, 0))
    gate = lambda s: pl.BlockSpec((1, tm, D_MODEL), lambda m: (B_G + s, m, 0))
    return pl.pallas_call(
        _merge_kernel,
        grid=(T // tm,),
        in_specs=[row, row, row, row, gate(0), gate(1), gate(2),
                  pl.BlockSpec((D_MODEL, D_MODEL), lambda m: (0, 0)),
                  pl.BlockSpec((1, D_MODEL), lambda m: (0, 0))],
        out_specs=[row, row],
        out_shape=[jax.ShapeDtypeStruct((T, D_MODEL), F32), jax.ShapeDtypeStruct((T, D_MODEL), BF16)],
        compiler_params=_cparams(("parallel",), 48),
        name="merge_out",
    )(x2d, oa, ob, oc, bseg, bseg, bseg, w_out_bf, norm2_w.reshape(1, D_MODEL))


MLP_FC = 1024


def _mlp_kernel(h_ref, x_ref, wu_ref, wd_ref, fw_ref, o_ref, *, final):
    h = h_ref[...]
    acc = x_ref[...]
    for c in range(D_FF // MLP_FC):
        up = _dot(h, wu_ref[:, c * MLP_FC:(c + 1) * MLP_FC])
        up = jnp.square(jnp.maximum(up, 0.0))
        acc = acc + _dot(up.astype(BF16), wd_ref[c * MLP_FC:(c + 1) * MLP_FC, :])
    if final:
        acc = acc * lax.rsqrt(jnp.mean(acc * acc, axis=-1, keepdims=True) + EPS) * fw_ref[...]
    o_ref[...] = acc


def _mlp(h2, x1, w_up_bf, w_down_bf, final_w, tm, final):
    T = x1.shape[0]
    row = pl.BlockSpec((tm, D_MODEL), lambda m: (m, 0))
    return pl.pallas_call(
        functools.partial(_mlp_kernel, final=final),
        grid=(T // tm,),
        in_specs=[row, row,
                  pl.BlockSpec((D_MODEL, D_FF), lambda m: (0, 0)),
                  pl.BlockSpec((D_FF, D_MODEL), lambda m: (0, 0)),
                  pl.BlockSpec((1, D_MODEL), lambda m: (0, 0))],
        out_specs=row,
        out_shape=jax.ShapeDtypeStruct((T, D_MODEL), F32),
        compiler_params=_cparams(("parallel",), 56),
        name="mlp",
    )(h2, x1, w_up_bf, w_down_bf, final_w.reshape(1, D_MODEL))


def _rope_tables(pos):
    half = HD_A // 2
    inv = ROPE_THETA ** (-jnp.arange(half, dtype=F32) / half)
    ang = pos.astype(F32)[:, None] * inv[None, :]
    cos, sin = jnp.cos(ang), jnp.sin(ang)
    return jnp.concatenate([cos, cos], axis=-1), jnp.concatenate([-sin, sin], axis=-1)


def _pad_rows(a, n):
    return jnp.pad(a, ((0, 0), (0, n - a.shape[1]), (0, 0)))


def _round_up(n, m):
    return -(-n // m) * m


def _layer(x, pos0, past, weights, layer_idx, final_w, final, caches):
    (norm1_w, w_blocks, w_small, lq1, lk1, lq2, lk2, subln_w, conv_w, a_log, dt_bias,
     gdn_norm_w, w_out_bf, norm2_w, w_up_bf, w_down_bf) = weights
    B, L, _ = x.shape
    T = B * L
    prompt = past is None
    lam_init = 0.8 - 0.6 * math.exp(-0.3 * layer_idx)
    x2d = x.reshape(T, D_MODEL)

    cos_t, sin_t = _rope_tables(pos0 + jnp.arange(L))
    tm = min(512, T)
    if L % tm:
        cos_t, sin_t = jnp.tile(cos_t, (B, 1)), jnp.tile(sin_t, (B, 1))
    caches, fseg, bseg, small = _in_proj(x2d, norm1_w, w_blocks, w_small, layer_idx, cos_t, sin_t, caches, tm)
    fseg4 = fseg.reshape(N_BQKV, B, L, PROJ_TN)
    bseg4 = bseg.reshape(N_BSEG, B, L, PROJ_TN)
    small = small.reshape(B, L, LANES)

    def with_past(past_arr, seg, lk):
        past_len = past_arr.shape[1]
        allk = jnp.concatenate([past_arr.reshape(B, past_len, -1).astype(BF16), bseg4[seg]], axis=1)
        return _pad_rows(allk, lk)[None], 0

    if prompt:
        oa = _attn_a((bseg4, B_AQ), (bseg4, B_AK), (bseg4, B_AV), lq1, lk1, lq2, lk2, subln_w,
                     tq=512, tk=512, rows=2, q_off=0, n_valid=L, lam_init=lam_init)
    else:
        past_len = past[0].shape[1]
        lk = _round_up(past_len + L, LANES)
        oa = _attn_a((bseg4, B_AQ), with_past(past[0], B_AK, lk), with_past(past[1], B_AV, lk),
                     lq1, lk1, lq2, lk2, subln_w, tq=L, tk=lk, rows=1, q_off=past_len, n_valid=past_len + L,
                     lam_init=lam_init)

    hist8 = jnp.zeros((B, SUBLANES, 3 * W_B), F32)
    if not prompt:
        hist8 = hist8.at[:, SUBLANES - (CONV_W - 1):].set(past[4])
    qkv = _conv_qkv(fseg4, hist8, conv_w, tl=min(512, L))
    tail = CONV_W - 1
    assert L >= tail
    new_conv = jnp.concatenate([fseg4[s, :, L - tail:] for s in range(N_BQKV)], axis=-1)
    if prompt:
        r = 128
        gb = _gdn_gates(small, a_log, dt_bias, r, L)
        s0 = jnp.zeros((B, H_B, HD_B, HD_B), F32)
        ob, s_new = _gdn(qkv, gb, (bseg4, B_BZ), gdn_norm_w, s0, r)
    else:
        lp = _round_up(L, CHUNK)
        gb = _gdn_gates(_pad_rows(small, lp), a_log, dt_bias, CHUNK, L)
        bz_pad = _pad_rows(bseg4[B_BZ], lp)[None]
        ob, s_new = _gdn(_pad_rows(qkv, lp), gb, (bz_pad, 0), gdn_norm_w, past[5], CHUNK)
        ob = ob[:, :L]

    if prompt:
        oc = _attn_c((bseg4, B_CQ), (bseg4, B_CK), (bseg4, B_CV), tq=128, tk=128, q_off=0)
    else:
        past_len = past[2].shape[1]
        lk = _round_up(past_len + L, LANES)
        oc = _attn_c((bseg4, B_CQ), with_past(past[2], B_CK, lk), with_past(past[3], B_CV, lk),
                     tq=L, tk=LANES, q_off=past_len)

    tm2 = min(256, T)
    f2 = lambda a: a.reshape(T, a.shape[-1])
    x1, h2 = _merge_out(x2d, f2(oa), f2(ob), f2(oc), bseg, w_out_bf, norm2_w, tm2)
    x2 = _mlp(h2, x1, w_up_bf, w_down_bf, final_w, tm2, final)
    return x2.reshape(B, L, D_MODEL), caches, new_conv, s_new


N_ALIGNED = 6
SMALL_W = 2 * H_B
SMALL_BLOCK = N_ALIGNED * PROJ_TN // LANES


def _pack_src_block(n):
    return jnp.where(n < 3, n, jnp.where(n < 6, n + 4, jnp.where(n < 9, n - 3, jnp.where(n == 9, 6, n))))


def _pack_kernel(a_ref, b_ref, sm_ref, o_ref, osm_ref):
    n = pl.program_id(1)
    shifted = ((n >= 3) & (n < 6)) | (n >= 9)

    @pl.when(n == 0)
    def _():
        lane = lax.broadcasted_iota(jnp.int32, sm_ref.shape[1:], 1)
        osm_ref[0] = jnp.where(lane < SMALL_W, sm_ref[0], 0.0).astype(BF16)

    @pl.when(jnp.logical_not(shifted))
    def _():
        o_ref[0, 0] = a_ref[0].astype(BF16)

    @pl.when(shifted)
    def _():
        last = PROJ_TN - LANES
        o_ref[0, 0] = pltpu.roll(a_ref[0], PROJ_TN - SMALL_W, 1).astype(BF16)
        tail_a = pltpu.roll(a_ref[0, :, last:], LANES - SMALL_W, 1)
        tail_b = pltpu.roll(b_ref[0], LANES - SMALL_W, 1)
        lane = lax.broadcasted_iota(jnp.int32, tail_a.shape, 1)
        o_ref[0, 0, :, last:] = jnp.where(lane < LANES - SMALL_W, tail_a, tail_b).astype(BF16)


def _proj_weight_blocks(w_in):
    depth = w_in.shape[0]
    per = PROJ_TN // LANES
    return pl.pallas_call(
        _pack_kernel,
        grid=(depth, len(PROJ_GROUPS)),
        in_specs=[
            pl.BlockSpec((1, D_MODEL, PROJ_TN), lambda l, n: (l, 0, _pack_src_block(n))),
            pl.BlockSpec((1, D_MODEL, LANES), lambda l, n: (l, 0, (_pack_src_block(n) + 1) * per)),
            pl.BlockSpec((1, D_MODEL, LANES), lambda l, n: (l, 0, SMALL_BLOCK)),
        ],
        out_specs=[
            pl.BlockSpec((1, 1, D_MODEL, PROJ_TN), lambda l, n: (l, n, 0, 0)),
            pl.BlockSpec((1, D_MODEL, LANES), lambda l, n: (l, 0, 0)),
        ],
        out_shape=[jax.ShapeDtypeStruct((depth, len(PROJ_GROUPS), D_MODEL, PROJ_TN), BF16),
                   jax.ShapeDtypeStruct((depth, D_MODEL, LANES), BF16)],
        compiler_params=_cparams(("parallel", "arbitrary"), 40),
        name="pack_w_in",
    )(w_in, w_in, w_in)


def kernel(x_prompt, x_sample, cache_a_k, cache_a_v, cache_c_k, cache_c_v, state_b_conv, state_b_ssm,
           norm1_w, w_in, lam_q1, lam_k1, lam_q2, lam_k2, subln_w, conv_w, a_log, dt_bias,
           gdn_norm_w, w_out, norm2_w, w_up, w_down, final_norm_w):
    depth = w_in.shape[0]
    past_len = cache_a_k.shape[2]
    xp, xs = x_prompt, x_sample
    bp, lp = x_prompt.shape[0], x_prompt.shape[1]
    bs, ls = x_sample.shape[0], x_sample.shape[1]
    p_out = [[] for _ in range(2)]
    s_out = [[] for _ in range(2)]
    p_caches = [lax.empty((depth, bp * lp) + hd, F32) for hd in CACHE_HEADS]
    s_caches = [lax.empty((depth, bs * ls) + hd, F32) for hd in CACHE_HEADS]
    w_blocks, w_small = _proj_weight_blocks(w_in)
    for l in range(depth):
        weights = (norm1_w[l], w_blocks, w_small, lam_q1[l], lam_k1[l], lam_q2[l], lam_k2[l], subln_w[l],
                   conv_w[l], a_log[l], dt_bias[l], gdn_norm_w[l], w_out[l].astype(BF16), norm2_w[l],
                   w_up[l].astype(BF16), w_down[l].astype(BF16))
        final = l == depth - 1
        xp, p_caches, *rest = _layer(xp, 0, None, weights, l, final_norm_w, final, p_caches)
        for lst, a in zip(p_out, rest):
            lst.append(a)
        past = (cache_a_k[l], cache_a_v[l], cache_c_k[l], cache_c_v[l], state_b_conv[l], state_b_ssm[l])
        xs, s_caches, *rest = _layer(xs, past_len, past, weights, l, final_norm_w, final, s_caches)
        for lst, a in zip(s_out, rest):
            lst.append(a)

    def pack(caches, lists, batch, length):
        lead = (depth, batch, length)
        return tuple(c.reshape(lead + c.shape[2:]) for c in caches) + (jnp.stack(lists[0]), jnp.stack(lists[1]))

    return (xp, xs) + pack(p_caches, p_out, bp, lp) + pack(s_caches, s_out, bs, ls)
```

```python
import functools
import math

import jax
import jax.numpy as jnp
from jax import lax
from jax.experimental import pallas as pl
from jax.experimental.pallas import tpu as pltpu

F32 = jnp.float32
BF16 = jnp.bfloat16

D_MODEL = 1024
CHUNK = 64
CHUNK_SHIFT = 6
H_A, HD_A, DV_A = 4, 128, 256
H_B, HD_B = 8, 128
H_C, HD_C = 8, 128
W_B = H_B * HD_B
CONV_W = 4
D_FF = 4 * D_MODEL
ROPE_THETA = 10000.0
EPS = 1e-6
LANES = 128
SUBLANES = 8
MIB = 1024 * 1024
LOG2E = 1.4426950408889634
EXP_UNDERFLOW = -104.0


def _cparams(sem, vmem_mib):
    return pltpu.CompilerParams(dimension_semantics=sem, vmem_limit_bytes=vmem_mib * MIB)


def _sigmoid(x):
    return 1.0 / (1.0 + jnp.exp(-x))


def _softplus_neg_abs(x):
    return jnp.log1p(jnp.exp(-jnp.abs(x)))


def _dot(a, b):
    return jnp.dot(a, b, preferred_element_type=F32)


def _split_bf16(a):
    hi = a.astype(BF16)
    return hi, (a - hi.astype(F32)).astype(BF16)


def _dot3(a, b):
    ah, al = _split_bf16(a)
    bh, bl = _split_bf16(b)
    return _dot(ah, bh) + _dot(ah, bl) + _dot(al, bh)


def _dot_nt(a, b):
    return lax.dot_general(a, b, (((1,), (1,)), ((), ())), preferred_element_type=F32)


def _dot_tn(a, b):
    return lax.dot_general(a, b, (((0,), (0,)), ((), ())), preferred_element_type=F32)


PROJ_TN = 1024
PROJ_GROUPS = ("aq", "ak", "av", "cq", "ck", "cv", "bqkv0", "bqkv1", "bqkv2", "bz", "g0", "g1", "g2")
N_BQKV = 3
B_AQ, B_AK, B_AV, B_CQ, B_CK, B_CV, B_BZ, B_G = 0, 1, 2, 3, 4, 5, 6, 7
N_BSEG = 10
CACHE_HEADS = ((2 * H_A, HD_A), (H_A, DV_A), (H_C, HD_C), (H_C, HD_C))
QA_SCALE = HD_A ** -0.5 * LOG2E
QC_SCALE = HD_C ** -0.5


def _proj_f_index(n):
    return jnp.clip(n - 6, 0, N_BQKV - 1)


def _proj_b_index(n):
    return jnp.where(n <= 5, n, jnp.where(n <= 8, 5, n - 3))


def _proj_kernel(x_ref, nw_ref, w_ref, wsm_ref, cos_ref, sin_ref, *refs):
    ak_ref, av_ref, ck_ref, cv_ref, f_ref, b_ref, sm_ref, h_scr = refs[len(CACHE_HEADS):]
    n = pl.program_id(1)

    @pl.when(n == 0)
    def _():
        x = x_ref[...]
        y = x * lax.rsqrt(jnp.mean(x * x, axis=-1, keepdims=True) + EPS) * nw_ref[...]
        hb = y.astype(BF16)
        h_scr[...] = hb
        sm_ref[...] = _dot(hb, wsm_ref[0])

    def mm():
        return _dot(h_scr[...], w_ref[0, 0])

    def rope(acc):
        cos = cos_ref[...]
        sin = sin_ref[...]
        slabs = [acc[:, s * LANES:(s + 1) * LANES] for s in range(PROJ_TN // LANES)]
        return jnp.concatenate([sl * cos + pltpu.roll(sl, LANES // 2, 1) * sin for sl in slabs], axis=1)

    def cache_out(o_ref, a):
        o_ref[0] = a.reshape(o_ref.shape[1:])
        b_ref[0] = a.astype(BF16)

    @pl.when(n == 0)
    def _():
        b_ref[0] = (rope(mm()) * QA_SCALE).astype(BF16)

    @pl.when(n == 1)
    def _():
        cache_out(ak_ref, rope(mm()))

    @pl.when(n == 2)
    def _():
        cache_out(av_ref, mm())

    @pl.when(n == 3)
    def _():
        b_ref[0] = (mm() * QC_SCALE).astype(BF16)

    @pl.when(n == 4)
    def _():
        cache_out(ck_ref, mm())

    @pl.when(n == 5)
    def _():
        cache_out(cv_ref, mm())

    @pl.when((n >= 6) & (n <= 8))
    def _():
        f_ref[0] = mm()

    @pl.when(n >= 9)
    def _():
        b_ref[0] = mm().astype(BF16)


def _in_proj(x2d, norm_w, w_blocks, w_small, layer, cos_t, sin_t, caches, tm):
    T = x2d.shape[0]
    tbl_blocks = cos_t.shape[0] // tm
    n_fixed = 6
    cache_specs = [pl.BlockSpec((1, tm) + hd, lambda m, n: (layer, m, 0, 0), pipeline_mode=pl.Buffered(1))
                   for hd in CACHE_HEADS]
    outs = pl.pallas_call(
        _proj_kernel,
        grid=(T // tm, len(PROJ_GROUPS)),
        in_specs=[
            pl.BlockSpec((tm, D_MODEL), lambda m, n: (m, 0)),
            pl.BlockSpec((1, D_MODEL), lambda m, n: (0, 0)),
            pl.BlockSpec((1, 1, D_MODEL, PROJ_TN), lambda m, n: (layer, n, 0, 0)),
            pl.BlockSpec((1, D_MODEL, LANES), lambda m, n: (layer, 0, 0)),
            pl.BlockSpec((tm, LANES), lambda m, n: (m % tbl_blocks, 0)),
            pl.BlockSpec((tm, LANES), lambda m, n: (m % tbl_blocks, 0)),
        ] + [pl.BlockSpec(memory_space=pl.ANY)] * len(CACHE_HEADS),
        out_specs=cache_specs + [
            pl.BlockSpec((1, tm, PROJ_TN), lambda m, n: (_proj_f_index(n), m, 0)),
            pl.BlockSpec((1, tm, PROJ_TN), lambda m, n: (_proj_b_index(n), m, 0)),
            pl.BlockSpec((tm, LANES), lambda m, n: (m, 0)),
        ],
        out_shape=[jax.ShapeDtypeStruct(c.shape, c.dtype) for c in caches] + [
            jax.ShapeDtypeStruct((N_BQKV, T, PROJ_TN), F32),
            jax.ShapeDtypeStruct((N_BSEG, T, PROJ_TN), BF16),
            jax.ShapeDtypeStruct((T, LANES), F32)],
        input_output_aliases={n_fixed + i: i for i in range(len(CACHE_HEADS))},
        scratch_shapes=[pltpu.VMEM((tm, D_MODEL), BF16)],
        compiler_params=_cparams(("parallel", "arbitrary"), 52),
        name="in_proj",
    )(x2d, norm_w.reshape(1, D_MODEL), w_blocks, w_small, cos_t, sin_t, *caches)
    return outs[:len(CACHE_HEADS)], outs[-3], outs[-2], outs[-1]


def _attn_a_kernel(q_ref, k_ref, v_ref, lq1_ref, lk1_ref, lq2_ref, lk2_ref, sw_ref, o_ref,
                   m_scr, l_scr, acc_scr, *, tq, tk, rows, q_off, n_valid, lam_init):
    i = pl.program_id(2)
    q = q_ref[0, 0]
    first_q = q_off + i * tq
    last_q = first_q + tq - 1
    vis_first = jnp.minimum(((first_q >> CHUNK_SHIFT) + 1) << CHUNK_SHIFT, n_valid)
    vis_last = jnp.minimum(((last_q >> CHUNK_SHIFT) + 1) << CHUNK_SHIFT, n_valid)
    n_full = vis_first // tk
    n_blocks = (vis_last + tk - 1) // tk

    m_scr[...] = jnp.full(m_scr.shape, -jnp.inf, F32)
    l_scr[...] = jnp.zeros(l_scr.shape, F32)
    acc_scr[...] = jnp.zeros(acc_scr.shape, F32)
    rg = tq // rows
    streams = [(r, c) for r in range(rows) for c in range(2)]
    ss = range(len(streams))
    qs = [q[r * rg:(r + 1) * rg, c * HD_A:(c + 1) * HD_A] for r, c in streams]
    qpos = [first_q + r * rg + lax.broadcasted_iota(jnp.int32, (rg, 1), 0) for r in range(rows)]
    nch = tk // LANES
    nacc = DV_A // LANES

    def block(j, masked):
        base = pl.multiple_of(j * tk, tk)
        k = k_ref[0, 0, pl.ds(base, tk), :]
        v = v_ref[0, 0, pl.ds(base, tk), :]
        kc = [k[:, c * HD_A:(c + 1) * HD_A] for c in range(2)]
        s = [_dot_nt(qs[n], kc[streams[n][1]]) for n in ss]
        if masked:
            kpos = base + lax.broadcasted_iota(jnp.int32, (1, tk), 1)
            kchunk = kpos >> CHUNK_SHIFT
            mask = [(kchunk <= (qpos[r] >> CHUNK_SHIFT)) & (kpos < n_valid) for r in range(rows)]
            s = [jnp.where(mask[streams[n][0]], s[n], -jnp.inf) for n in ss]
        chunks = [[s[n][:, t * LANES:(t + 1) * LANES] for t in range(nch)] for n in ss]
        m_prev = [m_scr[n] for n in ss]
        m_new = [jnp.maximum(m_prev[n], jnp.max(functools.reduce(jnp.maximum, chunks[n]), axis=1, keepdims=True))
                 for n in ss]
        alpha = [jnp.exp2(m_prev[n] - m_new[n]) for n in ss]
        ps = [[jnp.exp2((ch - m_new[n]).astype(BF16)) for ch in chunks[n]] for n in ss]
        for n in ss:
            tot = functools.reduce(jnp.add, ps[n]).astype(F32)
            l_scr[n] = alpha[n] * l_scr[n] + jnp.sum(tot, axis=1, keepdims=True)
            m_scr[n] = m_new[n]
        p = [jnp.concatenate(ps[n], axis=1) for n in ss]
        pv = [_dot(p[n], v) for n in ss]
        for n in ss:
            for t in range(nacc):
                sl = slice(t * LANES, (t + 1) * LANES)
                acc_scr[n, :, sl] = alpha[n] * acc_scr[n, :, sl] + pv[n][:, sl]

    def full_body(j, carry):
        block(j, False)
        return carry

    def edge_body(j, carry):
        block(j, True)
        return carry

    def pair_body(jj, carry):
        block(2 * jj, False)
        block(2 * jj + 1, False)
        return carry

    n_pairs = n_full // 2
    lax.fori_loop(0, n_pairs, pair_body, 0)
    lax.fori_loop(2 * n_pairs, n_full, full_body, 0)
    lax.fori_loop(n_full, n_blocks, edge_body, 0)

    lam = (jnp.exp(jnp.sum(lq1_ref[...] * lk1_ref[...], axis=1, keepdims=True))
           - jnp.exp(jnp.sum(lq2_ref[...] * lk2_ref[...], axis=1, keepdims=True)) + lam_init)
    for r in range(rows):
        parts = []
        for t in range(nacc):
            sl = slice(t * LANES, (t + 1) * LANES)
            parts.append(acc_scr[2 * r, :, sl] / l_scr[2 * r] - lam * (acc_scr[2 * r + 1, :, sl] / l_scr[2 * r + 1]))
        o = jnp.concatenate(parts, axis=1)
        o = o * lax.rsqrt(jnp.mean(o * o, axis=-1, keepdims=True) + EPS) * sw_ref[...]
        o_ref[0, r * rg:(r + 1) * rg, :] = o * (1.0 - lam_init)


def _attn_a(q, k, v, lq1, lk1, lq2, lk2, subln_w, *, tq, tk, rows, q_off, n_valid, lam_init):
    (qa, qs), (ka, ks), (va, vs) = q, k, v
    _, B, Lq, _ = qa.shape
    Lk = ka.shape[2]
    vec = pl.BlockSpec((1, HD_A), lambda b, h, i: (0, 0))
    return pl.pallas_call(
        functools.partial(_attn_a_kernel, tq=tq, tk=tk, rows=rows, q_off=q_off, n_valid=n_valid,
                          lam_init=lam_init),
        grid=(B, H_A, Lq // tq),
        in_specs=[
            pl.BlockSpec((1, 1, tq, 2 * HD_A), lambda b, h, i: (qs, b, i, h)),
            pl.BlockSpec((1, 1, Lk, 2 * HD_A), lambda b, h, i: (ks, b, 0, h)),
            pl.BlockSpec((1, 1, Lk, DV_A), lambda b, h, i: (vs, b, 0, h)),
            vec, vec, vec, vec,
            pl.BlockSpec((1, DV_A), lambda b, h, i: (0, 0)),
        ],
        out_specs=pl.BlockSpec((1, tq, DV_A), lambda b, h, i: (b, i, h)),
        out_shape=jax.ShapeDtypeStruct((B, Lq, H_A * DV_A), F32),
        scratch_shapes=[pltpu.VMEM((2 * rows, tq // rows, LANES), F32),
                        pltpu.VMEM((2 * rows, tq // rows, LANES), F32),
                        pltpu.VMEM((2 * rows, tq // rows, DV_A), F32)],
        compiler_params=_cparams(("parallel", "parallel", "arbitrary"), 40),
        name="attn_a",
    )(qa, ka, va, lq1.reshape(1, HD_A), lk1.reshape(1, HD_A), lq2.reshape(1, HD_A), lk2.reshape(1, HD_A),
      subln_w.reshape(1, DV_A))


C_HEADS = 4


def _attn_c_kernel(q_ref, k_ref, v_ref, o_ref, acc_scr, run_scr, *, tq, tk, q_off):
    i = pl.program_id(2)
    q = q_ref[0, 0]
    first_q = q_off + i * tq
    qpos = first_q + lax.broadcasted_iota(jnp.int32, (tq, 1), 0)
    j_start = (q_off + (i + 1) * tq - 2) // tk
    acc_scr[...] = jnp.zeros(acc_scr.shape, F32)
    run_scr[...] = jnp.zeros(run_scr.shape, F32)
    gs = range(C_HEADS)
    sl = [slice(g * HD_C, (g + 1) * HD_C) for g in gs]

    def cum_weights(w):
        rj = lax.broadcasted_iota(jnp.int32, (w, w + tk), 0)
        cs = lax.broadcasted_iota(jnp.int32, (w, w + tk), 1)
        return jnp.where((cs >= w) | (rj > cs), 1.0, 0.0).astype(BF16)

    def pre(base, w, masked):
        cum_w = cum_weights(w)
        z = [_dot_nt(q[:, sl[g]], k_ref[0, 0, pl.ds(base, w), sl[g]]) for g in gs]
        t = [jnp.log(1.0 + jnp.exp(-jnp.abs(z[g]))) for g in gs]
        log_beta = [jnp.minimum(z[g], 0.0) - t[g] for g in gs]
        log_1m = [jnp.minimum(-z[g], 0.0) - t[g] for g in gs]
        mask = None
        if masked:
            mask = (base + lax.broadcasted_iota(jnp.int32, (1, w), 1)) < qpos
            log_1m = [jnp.where(mask, log_1m[g], 0.0) for g in gs]
        parts = [_split_bf16(log_1m[g]) for g in gs]
        cum = [_dot(parts[g][0], cum_w) + _dot(parts[g][1], cum_w) for g in gs]
        return base, w, mask, log_beta, cum

    def post(state):
        base, w, mask, log_beta, cum = state
        run = [run_scr[g] for g in gs]
        run_w = [jnp.concatenate([run[g]] * (w // tk), axis=1) for g in gs]
        a = [jnp.exp(log_beta[g] + cum[g][:, :w] + run_w[g]) for g in gs]
        if mask is not None:
            a = [jnp.where(mask, a[g], 0.0) for g in gs]
        new_run = [run[g] + cum[g][:, w:] for g in gs]
        for g in gs:
            acc_scr[g] += _dot(a[g].astype(BF16), v_ref[0, 0, pl.ds(base, w), sl[g]])
            run_scr[g] = new_run[g]
        top = functools.reduce(jnp.maximum, new_run)
        return (jnp.max(top) > EXP_UNDERFLOW).astype(jnp.int32)

    def block(base, w, masked):
        return post(pre(base, w, masked))

    wide = 2 * tk
    end0 = (j_start + 1) * tk

    def fused_start(_):
        base_e = pl.multiple_of(end0 - tk, tk)
        base_w = pl.multiple_of(end0 - tk - wide, tk)
        edge = pre(base_e, tk, True)
        left = pre(base_w, wide, False)
        post(edge)
        return base_w, post(left)

    def edge_body(c):
        base = pl.multiple_of(c[0] - tk, tk)
        return base, block(base, tk, True)

    def wide_body(c):
        base = pl.multiple_of(c[0] - wide, tk)
        return base, block(base, wide, False)

    def last_body(c):
        base = pl.multiple_of(c[0] - tk, tk)
        return base, block(base, tk, False)

    one_edge_then_wide = (end0 - tk <= first_q) & (end0 - tk >= wide)
    c = lax.cond(one_edge_then_wide, fused_start, lambda _: (end0, jnp.int32(1)), 0)
    c = lax.while_loop(lambda c: (c[0] > 0) & (c[1] > 0) & (c[0] > first_q), edge_body, c)
    c = lax.while_loop(lambda c: (c[0] >= wide) & (c[1] > 0), wide_body, c)
    lax.while_loop(lambda c: (c[0] > 0) & (c[1] > 0), last_body, c)
    for g in gs:
        o_ref[0, :, sl[g]] = acc_scr[g]


def _attn_c(q, k, v, *, tq, tk, q_off):
    (qa, qs), (ka, ks), (va, vs) = q, k, v
    _, B, Lq, _ = qa.shape
    Lk = ka.shape[2]
    w = C_HEADS * HD_C
    assert tk == LANES
    return pl.pallas_call(
        functools.partial(_attn_c_kernel, tq=tq, tk=tk, q_off=q_off),
        grid=(B, H_C // C_HEADS, Lq // tq),
        in_specs=[
            pl.BlockSpec((1, 1, tq, w), lambda b, g, i: (qs, b, i, g)),
            pl.BlockSpec((1, 1, Lk, w), lambda b, g, i: (ks, b, 0, g)),
            pl.BlockSpec((1, 1, Lk, w), lambda b, g, i: (vs, b, 0, g)),
        ],
        out_specs=pl.BlockSpec((1, tq, w), lambda b, g, i: (b, i, g)),
        out_shape=jax.ShapeDtypeStruct((B, Lq, H_C * HD_C), F32),
        scratch_shapes=[pltpu.VMEM((C_HEADS, tq, HD_C), F32), pltpu.VMEM((C_HEADS, tq, tk), F32)],
        compiler_params=_cparams(("parallel", "parallel", "arbitrary"), 48),
        name="attn_c",
    )(qa, ka, va)


CONV_CW = 512


def _conv_kernel(cur_ref, prev_ref, hist_ref, w_ref, o_ref, xs_scr, *, tl):
    i = pl.program_id(1)
    c = pl.program_id(2)
    xs_scr[0:SUBLANES, :] = jnp.where(i == 0, hist_ref[0], prev_ref[0, 0])
    xs_scr[SUBLANES:, :] = cur_ref[0, 0]
    w = w_ref[...]
    first = SUBLANES - (CONV_W - 1)
    conv = xs_scr[first:first + tl, :] * w[0:1, :]
    for t in range(1, CONV_W):
        conv = conv + xs_scr[first + t:first + t + tl, :] * w[t:t + 1, :]
    y = conv * _sigmoid(conv)
    blocks_per_part = W_B // CONV_CW
    is_qk = c < 2 * blocks_per_part
    qk_scale = jnp.where(c < blocks_per_part, HD_B ** -0.5, 1.0)
    for s in range(CONV_CW // HD_B):
        sl = y[:, s * HD_B:(s + 1) * HD_B]
        nrm = lax.rsqrt(jnp.sum(sl * sl, axis=-1, keepdims=True) + EPS) * qk_scale
        o_ref[0, :, s * HD_B:(s + 1) * HD_B] = sl * jnp.where(is_qk, nrm, 1.0)


def _conv_qkv(fseg, hist8, conv_w, tl):
    _, B, L, _ = fseg.shape
    N = 3 * W_B
    r = tl // SUBLANES
    per = PROJ_TN // CONV_CW
    return pl.pallas_call(
        functools.partial(_conv_kernel, tl=tl),
        grid=(B, L // tl, N // CONV_CW),
        in_specs=[
            pl.BlockSpec((1, 1, tl, CONV_CW), lambda b, i, c: (c // per, b, i, c % per)),
            pl.BlockSpec((1, 1, SUBLANES, CONV_CW),
                         lambda b, i, c: (c // per, b, jnp.maximum(i * r - 1, 0), c % per)),
            pl.BlockSpec((1, SUBLANES, CONV_CW), lambda b, i, c: (b, 0, c)),
            pl.BlockSpec((CONV_W, CONV_CW), lambda b, i, c: (0, c)),
        ],
        out_specs=pl.BlockSpec((1, tl, CONV_CW), lambda b, i, c: (b, i, c)),
        out_shape=jax.ShapeDtypeStruct((B, L, N), F32),
        scratch_shapes=[pltpu.VMEM((tl + SUBLANES, CONV_CW), F32)],
        compiler_params=_cparams(("parallel", "parallel", "parallel"), 32),
        name="gdn_conv",
    )(fseg, fseg, hist8, conv_w)


def _gates_kernel(sm_ref, alog_ref, dtb_ref, o_ref, *, r, valid):
    blk = pl.program_id(1)
    x = sm_ref[0]
    y = x + dtb_ref[...]
    g = -jnp.exp(alog_ref[...]) * (jnp.maximum(y, 0.0) + _softplus_neg_abs(y))
    beta = _sigmoid(x)
    row = blk * r + lax.broadcasted_iota(jnp.int32, (r, 1), 0)
    live = row < valid
    g = jnp.where(live, g, 0.0)
    beta = jnp.where(live, beta, 0.0)
    ri = lax.broadcasted_iota(jnp.int32, (r, r), 0)
    ci = lax.broadcasted_iota(jnp.int32, (r, r), 1)
    tri = jnp.where(((ri >> CHUNK_SHIFT) == (ci >> CHUNK_SHIFT)) & (ci <= ri), 1.0, 0.0)
    gc = jnp.dot(tri, g, precision=lax.Precision.HIGHEST, preferred_element_type=F32)
    lane = lax.broadcasted_iota(jnp.int32, (r, LANES), 1)
    o_ref[0] = jnp.where(lane < H_B, gc, beta)


def _gdn_gates(small, a_log, dt_bias, r, valid):
    B, L, _ = small.shape
    pad = lambda v: jnp.pad(v.astype(F32), (0, LANES - H_B)).reshape(1, LANES)
    return pl.pallas_call(
        functools.partial(_gates_kernel, r=r, valid=valid),
        grid=(B, L // r),
        in_specs=[
            pl.BlockSpec((1, r, LANES), lambda b, i: (b, i, 0)),
            pl.BlockSpec((1, LANES), lambda b, i: (0, 0)),
            pl.BlockSpec((1, LANES), lambda b, i: (0, 0)),
        ],
        out_specs=pl.BlockSpec((1, r, LANES), lambda b, i: (b, i, 0)),
        out_shape=jax.ShapeDtypeStruct((B, L, LANES), F32),
        compiler_params=_cparams(("parallel", "parallel"), 32),
        name="gdn_gates",
    )(small, pad(a_log), pad(dt_bias))


def _gdn_kernel(qkv_ref, gb_ref, gbt_ref, bz_ref, nw_ref, s0_ref, o_ref, s_out_ref, s_scr, *, r):
    blk = pl.program_id(1)
    nblk = pl.num_programs(1)

    @pl.when(blk == 0)
    def _():
        s_scr[...] = s0_ref[0]

    gb = gb_ref[0]
    gbt = gbt_ref[0]
    ri = lax.broadcasted_iota(jnp.int32, (r, r), 0)
    ci = lax.broadcasted_iota(jnp.int32, (r, r), 1)
    same = (ri >> CHUNK_SHIFT) == (ci >> CHUNK_SHIFT)
    tri = same & (ci <= ri)
    strict = same & (ci < ri)
    pair = (ri >> 1) == (ci >> 1)
    off_masks = [((ri >> (lg + 1)) == (ci >> (lg + 1))) & ((ri >> lg) != (ci >> lg))
                 for lg in range(1, CHUNK_SHIFT)]
    nw = nw_ref[...]

    hs = range(H_B)
    q = [qkv_ref[0, :, h * HD_B:(h + 1) * HD_B] for h in hs]
    k = [qkv_ref[0, :, (H_B + h) * HD_B:(H_B + h + 1) * HD_B] for h in hs]
    v = [qkv_ref[0, :, (2 * H_B + h) * HD_B:(2 * H_B + h + 1) * HD_B] for h in hs]
    gc_col = [gb[:, h:h + 1] for h in hs]
    beta_col = [gb[:, H_B + h:H_B + h + 1] for h in hs]
    decay = [jnp.where(tri, jnp.exp(jnp.where(tri, gc_col[h] - gbt[h:h + 1, :], 0.0)), 0.0) for h in hs]
    kbf = [k[h].astype(BF16) for h in hs]
    kb = [k[h] * beta_col[h] for h in hs]
    m = [jnp.where(strict, _dot_nt(kb[h].astype(BF16), kbf[h]) * decay[h], 0.0) for h in hs]
    x = [-jnp.where(pair, m[h], 0.0) for h in hs]
    for off_mask in off_masks:
        off = [jnp.where(off_mask, m[h], 0.0) for h in hs]
        xb = [x[h].astype(BF16) for h in hs]
        t = [off[h] + _dot(xb[h], off[h].astype(BF16)) for h in hs]
        x = [x[h] - t[h] - _dot(t[h].astype(BF16), xb[h]) for h in hs]
    egc = [jnp.exp(gc_col[h]) for h in hs]
    rhs = [jnp.concatenate([v[h] * beta_col[h], kb[h] * egc[h]], axis=1) for h in hs]
    sol = [rhs[h] + _dot(x[h].astype(BF16), rhs[h].astype(BF16)) for h in hs]
    aqk = [jnp.where(tri, _dot_nt(q[h].astype(BF16), kbf[h]) * decay[h], 0.0).astype(BF16) for h in hs]
    q_in = [(q[h] * egc[h]).astype(BF16) for h in hs]

    s = [s_scr[h] for h in hs]
    outs = [[] for _ in hs]
    for c in range(r // CHUNK):
        lo, hi = c * CHUNK, (c + 1) * CHUNK
        g_last = [gc_col[h][hi - 1:hi, :] for h in hs]
        sb = [s[h].astype(BF16) for h in hs]
        u = [sol[h][lo:hi, :HD_B] - _dot(sol[h][lo:hi, HD_B:].astype(BF16), sb[h]) for h in hs]
        ub = [u[h].astype(BF16) for h in hs]
        for h in hs:
            outs[h].append(_dot(q_in[h][lo:hi], sb[h]) + _dot(aqk[h][lo:hi, lo:hi], ub[h]))
        k_out = [(k[h][lo:hi] * jnp.exp(g_last[h] - gc_col[h][lo:hi])).astype(BF16) for h in hs]
        s = [s[h] * jnp.exp(g_last[h]) + _dot_tn(k_out[h], ub[h]) for h in hs]
    for h in hs:
        s_scr[h] = s[h]
        o = outs[h][0] if len(outs[h]) == 1 else jnp.concatenate(outs[h], axis=0)
        o = o * lax.rsqrt(jnp.mean(o * o, axis=-1, keepdims=True) + EPS) * nw
        z = bz_ref[0, 0, :, h * HD_B:(h + 1) * HD_B].astype(F32)
        o_ref[0, :, h * HD_B:(h + 1) * HD_B] = o * (z * _sigmoid(z))

    @pl.when(blk == nblk - 1)
    def _():
        s_out_ref[0] = s_scr[...]


def _gdn(qkv, gb, bz, norm_w, s0, r):
    B, L, _ = qkv.shape
    bza, bzs = bz
    gbt = jnp.swapaxes(gb[:, :, :2 * H_B], 1, 2)
    return pl.pallas_call(
        functools.partial(_gdn_kernel, r=r),
        grid=(B, L // r),
        in_specs=[
            pl.BlockSpec((1, r, 3 * W_B), lambda b, i: (b, i, 0)),
            pl.BlockSpec((1, r, LANES), lambda b, i: (b, i, 0)),
            pl.BlockSpec((1, 2 * H_B, r), lambda b, i: (b, 0, i)),
            pl.BlockSpec((1, 1, r, W_B), lambda b, i: (bzs, b, i, 0)),
            pl.BlockSpec((1, HD_B), lambda b, i: (0, 0)),
            pl.BlockSpec((1, H_B, HD_B, HD_B), lambda b, i: (b, 0, 0, 0)),
        ],
        out_specs=[
            pl.BlockSpec((1, r, W_B), lambda b, i: (b, i, 0)),
            pl.BlockSpec((1, H_B, HD_B, HD_B), lambda b, i: (b, 0, 0, 0)),
        ],
        out_shape=[jax.ShapeDtypeStruct((B, L, W_B), F32),
                   jax.ShapeDtypeStruct((B, H_B, HD_B, HD_B), F32)],
        scratch_shapes=[pltpu.VMEM((H_B, HD_B, HD_B), F32)],
        compiler_params=_cparams(("parallel", "arbitrary"), 40),
        name="gdn_delta",
    )(qkv, gb, gbt, bza, norm_w.reshape(1, HD_B), s0)


def _merge_kernel(x_ref, oa_ref, ob_ref, oc_ref, g0_ref, g1_ref, g2_ref, w_ref, nw_ref, x1_ref, h2_ref):
    merged = (_sigmoid(g0_ref[0].astype(F32)) * oa_ref[...]
              + _sigmoid(g1_ref[0].astype(F32)) * ob_ref[...]
              + _sigmoid(g2_ref[0].astype(F32)) * oc_ref[...])
    x1 = x_ref[...] + _dot(merged.astype(BF16), w_ref[...])
    x1_ref[...] = x1
    h2 = x1 * lax.rsqrt(jnp.mean(x1 * x1, axis=-1, keepdims=True) + EPS) * nw_ref[...]
    h2_ref[...] = h2.astype(BF16)


def _merge_out(x2d, oa, ob, oc, bseg, w_out_bf, norm2_w, tm):
    T = x2d.shape[0]
    row = pl.BlockSpec((tm, D_MODEL), lambda m: (m, 0))
    gate = lambda s: pl.BlockSpec((1, tm, D_MODEL), lambda m: (B_G + s, m, 0))
    return pl.pallas_call(
        _merge_kernel,
        grid=(T // tm,),
        in_specs=[row, row, row, row, gate(0), gate(1), gate(2),
                  pl.BlockSpec((D_MODEL, D_MODEL), lambda m: (0, 0)),
                  pl.BlockSpec((1, D_MODEL), lambda m: (0, 0))],
        out_specs=[row, row],
        out_shape=[jax.ShapeDtypeStruct((T, D_MODEL), F32), jax.ShapeDtypeStruct((T, D_MODEL), BF16)],
        compiler_params=_cparams(("parallel",), 48),
        name="merge_out",
    )(x2d, oa, ob, oc, bseg, bseg, bseg, w_out_bf, norm2_w.reshape(1, D_MODEL))


MLP_FC = 1024


def _mlp_kernel(h_ref, x_ref, wu_ref, wd_ref, fw_ref, o_ref, *, final):
    h = h_ref[...]
    acc = x_ref[...]
    for c in range(D_FF // MLP_FC):
        up = _dot(h, wu_ref[:, c * MLP_FC:(c + 1) * MLP_FC])
        up = jnp.square(jnp.maximum(up, 0.0))
        acc = acc + _dot(up.astype(BF16), wd_ref[c * MLP_FC:(c + 1) * MLP_FC, :])
    if final:
        acc = acc * lax.rsqrt(jnp.mean(acc * acc, axis=-1, keepdims=True) + EPS) * fw_ref[...]
    o_ref[...] = acc


def _mlp(h2, x1, w_up_bf, w_down_bf, final_w, tm, final):
    T = x1.shape[0]
    row = pl.BlockSpec((tm, D_MODEL), lambda m: (m, 0))
    return pl.pallas_call(
        functools.partial(_mlp_kernel, final=final),
        grid=(T // tm,),
        in_specs=[row, row,
                  pl.BlockSpec((D_MODEL, D_FF), lambda m: (0, 0)),
                  pl.BlockSpec((D_FF, D_MODEL), lambda m: (0, 0)),
                  pl.BlockSpec((1, D_MODEL), lambda m: (0, 0))],
        out_specs=row,
        out_shape=jax.ShapeDtypeStruct((T, D_MODEL), F32),
        compiler_params=_cparams(("parallel",), 56),
        name="mlp",
    )(h2, x1, w_up_bf, w_down_bf, final_w.reshape(1, D_MODEL))


def _rope_tables(pos):
    half = HD_A // 2
    inv = ROPE_THETA ** (-jnp.arange(half, dtype=F32) / half)
    ang = pos.astype(F32)[:, None] * inv[None, :]
    cos, sin = jnp.cos(ang), jnp.sin(ang)
    return jnp.concatenate([cos, cos], axis=-1), jnp.concatenate([-sin, sin], axis=-1)


def _pad_rows(a, n):
    return jnp.pad(a, ((0, 0), (0, n - a.shape[1]), (0, 0)))


def _round_up(n, m):
    return -(-n // m) * m


def _layer(x, pos0, past, weights, layer_idx, final_w, final, caches):
    (norm1_w, w_blocks, w_small, lq1, lk1, lq2, lk2, subln_w, conv_w, a_log, dt_bias,
     gdn_norm_w, w_out_bf, norm2_w, w_up_bf, w_down_bf) = weights
    B, L, _ = x.shape
    T = B * L
    prompt = past is None
    lam_init = 0.8 - 0.6 * math.exp(-0.3 * layer_idx)
    x2d = x.reshape(T, D_MODEL)

    cos_t, sin_t = _rope_tables(pos0 + jnp.arange(L))
    tm = min(1024, T)
    if L % tm:
        cos_t, sin_t = jnp.tile(cos_t, (B, 1)), jnp.tile(sin_t, (B, 1))
    caches, fseg, bseg, small = _in_proj(x2d, norm1_w, w_blocks, w_small, layer_idx, cos_t, sin_t, caches, tm)
    fseg4 = fseg.reshape(N_BQKV, B, L, PROJ_TN)
    bseg4 = bseg.reshape(N_BSEG, B, L, PROJ_TN)
    small = small.reshape(B, L, LANES)

    def with_past(past_arr, seg, lk):
        past_len = past_arr.shape[1]
        allk = jnp.concatenate([past_arr.reshape(B, past_len, -1).astype(BF16), bseg4[seg]], axis=1)
        return _pad_rows(allk, lk)[None], 0

    if prompt:
        oa = _attn_a((bseg4, B_AQ), (bseg4, B_AK), (bseg4, B_AV), lq1, lk1, lq2, lk2, subln_w,
                     tq=512, tk=512, rows=2, q_off=0, n_valid=L, lam_init=lam_init)
    else:
        past_len = past[0].shape[1]
        lk = _round_up(past_len + L, LANES)
        oa = _attn_a((bseg4, B_AQ), with_past(past[0], B_AK, lk), with_past(past[1], B_AV, lk),
                     lq1, lk1, lq2, lk2, subln_w, tq=L, tk=lk, rows=1, q_off=past_len, n_valid=past_len + L,
                     lam_init=lam_init)

    hist8 = jnp.zeros((B, SUBLANES, 3 * W_B), F32)
    if not prompt:
        hist8 = hist8.at[:, SUBLANES - (CONV_W - 1):].set(past[4])
    qkv = _conv_qkv(fseg4, hist8, conv_w, tl=min(512, L))
    tail = CONV_W - 1
    assert L >= tail
    new_conv = jnp.concatenate([fseg4[s, :, L - tail:] for s in range(N_BQKV)], axis=-1)
    if prompt:
        r = 128
        gb = _gdn_gates(small, a_log, dt_bias, min(512, L), L)
        s0 = jnp.zeros((B, H_B, HD_B, HD_B), F32)
        ob, s_new = _gdn(qkv, gb, (bseg4, B_BZ), gdn_norm_w, s0, r)
    else:
        lp = _round_up(L, CHUNK)
        gb = _gdn_gates(_pad_rows(small, lp), a_log, dt_bias, CHUNK, L)
        bz_pad = _pad_rows(bseg4[B_BZ], lp)[None]
        ob, s_new = _gdn(_pad_rows(qkv, lp), gb, (bz_pad, 0), gdn_norm_w, past[5], CHUNK)
        ob = ob[:, :L]

    if prompt:
        oc = _attn_c((bseg4, B_CQ), (bseg4, B_CK), (bseg4, B_CV), tq=128, tk=128, q_off=0)
    else:
        past_len = past[2].shape[1]
        lk = _round_up(past_len + L, LANES)
        oc = _attn_c((bseg4, B_CQ), with_past(past[2], B_CK, lk), with_past(past[3], B_CV, lk),
                     tq=L, tk=LANES, q_off=past_len)

    tm2 = min(256, T)
    f2 = lambda a: a.reshape(T, a.shape[-1])
    x1, h2 = _merge_out(x2d, f2(oa), f2(ob), f2(oc), bseg, w_out_bf, norm2_w, tm2)
    x2 = _mlp(h2, x1, w_up_bf, w_down_bf, final_w, tm2, final)
    return x2.reshape(B, L, D_MODEL), caches, new_conv, s_new


N_ALIGNED = 6
SMALL_W = 2 * H_B
SMALL_BLOCK = N_ALIGNED * PROJ_TN // LANES


def _pack_src_block(n):
    return jnp.where(n < 3, n, jnp.where(n < 6, n + 4, jnp.where(n < 9, n - 3, jnp.where(n == 9, 6, n))))


def _pack_kernel(a_ref, b_ref, sm_ref, o_ref, osm_ref):
    n = pl.program_id(1)
    shifted = ((n >= 3) & (n < 6)) | (n >= 9)

    @pl.when(n == 0)
    def _():
        lane = lax.broadcasted_iota(jnp.int32, sm_ref.shape[1:], 1)
        osm_ref[0] = jnp.where(lane < SMALL_W, sm_ref[0], 0.0).astype(BF16)

    @pl.when(jnp.logical_not(shifted))
    def _():
        o_ref[0, 0] = a_ref[0].astype(BF16)

    @pl.when(shifted)
    def _():
        last = PROJ_TN - LANES
        o_ref[0, 0] = pltpu.roll(a_ref[0], PROJ_TN - SMALL_W, 1).astype(BF16)
        tail_a = pltpu.roll(a_ref[0, :, last:], LANES - SMALL_W, 1)
        tail_b = pltpu.roll(b_ref[0], LANES - SMALL_W, 1)
        lane = lax.broadcasted_iota(jnp.int32, tail_a.shape, 1)
        o_ref[0, 0, :, last:] = jnp.where(lane < LANES - SMALL_W, tail_a, tail_b).astype(BF16)


def _proj_weight_blocks(w_in):
    depth = w_in.shape[0]
    per = PROJ_TN // LANES
    return pl.pallas_call(
        _pack_kernel,
        grid=(depth, len(PROJ_GROUPS)),
        in_specs=[
            pl.BlockSpec((1, D_MODEL, PROJ_TN), lambda l, n: (l, 0, _pack_src_block(n))),
            pl.BlockSpec((1, D_MODEL, LANES), lambda l, n: (l, 0, (_pack_src_block(n) + 1) * per)),
            pl.BlockSpec((1, D_MODEL, LANES), lambda l, n: (l, 0, SMALL_BLOCK)),
        ],
        out_specs=[
            pl.BlockSpec((1, 1, D_MODEL, PROJ_TN), lambda l, n: (l, n, 0, 0)),
            pl.BlockSpec((1, D_MODEL, LANES), lambda l, n: (l, 0, 0)),
        ],
        out_shape=[jax.ShapeDtypeStruct((depth, len(PROJ_GROUPS), D_MODEL, PROJ_TN), BF16),
                   jax.ShapeDtypeStruct((depth, D_MODEL, LANES), BF16)],
        compiler_params=_cparams(("parallel", "arbitrary"), 40),
        name="pack_w_in",
    )(w_in, w_in, w_in)


def kernel(x_prompt, x_sample, cache_a_k, cache_a_v, cache_c_k, cache_c_v, state_b_conv, state_b_ssm,
           norm1_w, w_in, lam_q1, lam_k1, lam_q2, lam_k2, subln_w, conv_w, a_log, dt_bias,
           gdn_norm_w, w_out, norm2_w, w_up, w_down, final_norm_w):
    depth = w_in.shape[0]
    past_len = cache_a_k.shape[2]
    xp, xs = x_prompt, x_sample
    bp, lp = x_prompt.shape[0], x_prompt.shape[1]
    bs, ls = x_sample.shape[0], x_sample.shape[1]
    p_out = [[] for _ in range(2)]
    s_out = [[] for _ in range(2)]
    p_caches = [lax.empty((depth, bp * lp) + hd, F32) for hd in CACHE_HEADS]
    s_caches = [lax.empty((depth, bs * ls) + hd, F32) for hd in CACHE_HEADS]
    w_blocks, w_small = _proj_weight_blocks(w_in)
    for l in range(depth):
        weights = (norm1_w[l], w_blocks, w_small, lam_q1[l], lam_k1[l], lam_q2[l], lam_k2[l], subln_w[l],
                   conv_w[l], a_log[l], dt_bias[l], gdn_norm_w[l], w_out[l].astype(BF16), norm2_w[l],
                   w_up[l].astype(BF16), w_down[l].astype(BF16))
        final = l == depth - 1
        xp, p_caches, *rest = _layer(xp, 0, None, weights, l, final_norm_w, final, p_caches)
        for lst, a in zip(p_out, rest):
            lst.append(a)
        past = (cache_a_k[l], cache_a_v[l], cache_c_k[l], cache_c_v[l], state_b_conv[l], state_b_ssm[l])
        xs, s_caches, *rest = _layer(xs, past_len, past, weights, l, final_norm_w, final, s_caches)
        for lst, a in zip(s_out, rest):
            lst.append(a)

    def pack(caches, lists, batch, length):
        lead = (depth, batch, length)
        return tuple(c.reshape(lead + c.shape[2:]) for c in caches) + (jnp.stack(lists[0]), jnp.stack(lists[1]))

    return (xp, xs) + pack(p_caches, p_out, bp, lp) + pack(s_caches, s_out, bs, ls)
```

```python
import functools
import math

import jax
import jax.numpy as jnp
from jax import lax
from jax.experimental import pallas as pl
from jax.experimental.pallas import tpu as pltpu

F32 = jnp.float32
BF16 = jnp.bfloat16

D_MODEL = 1024
CHUNK = 64
CHUNK_SHIFT = 6
H_A, HD_A, DV_A = 4, 128, 256
H_B, HD_B = 8, 128
H_C, HD_C = 8, 128
W_B = H_B * HD_B
CONV_W = 4
D_FF = 4 * D_MODEL
ROPE_THETA = 10000.0
EPS = 1e-6
LANES = 128
SUBLANES = 8
MIB = 1024 * 1024
LOG2E = 1.4426950408889634
EXP_UNDERFLOW = -104.0


def _cparams(sem, vmem_mib):
    return pltpu.CompilerParams(dimension_semantics=sem, vmem_limit_bytes=vmem_mib * MIB)


def _sigmoid(x):
    return 1.0 / (1.0 + jnp.exp(-x))


def _softplus_neg_abs(x):
    return jnp.log1p(jnp.exp(-jnp.abs(x)))


def _dot(a, b):
    return jnp.dot(a, b, preferred_element_type=F32)


def _split_bf16(a):
    hi = a.astype(BF16)
    return hi, (a - hi.astype(F32)).astype(BF16)


def _dot3(a, b):
    ah, al = _split_bf16(a)
    bh, bl = _split_bf16(b)
    return _dot(ah, bh) + _dot(ah, bl) + _dot(al, bh)


def _dot_nt(a, b):
    return lax.dot_general(a, b, (((1,), (1,)), ((), ())), preferred_element_type=F32)


def _dot_tn(a, b):
    return lax.dot_general(a, b, (((0,), (0,)), ((), ())), preferred_element_type=F32)


PROJ_TN = 1024
PROJ_GROUPS = ("aq", "ak", "av", "cq", "ck", "cv", "bqkv0", "bqkv1", "bqkv2", "bz", "g0", "g1", "g2")
N_BQKV = 3
B_AQ, B_AK, B_AV, B_CQ, B_CK, B_CV, B_BZ, B_G = 0, 1, 2, 3, 4, 5, 6, 7
N_BSEG = 10
CACHE_HEADS = ((2 * H_A, HD_A), (H_A, DV_A), (H_C, HD_C), (H_C, HD_C))
QA_SCALE = HD_A ** -0.5 * LOG2E
QC_SCALE = HD_C ** -0.5


def _proj_f_index(n):
    return jnp.clip(n - 6, 0, N_BQKV - 1)


def _proj_b_index(n):
    return jnp.where(n <= 5, n, jnp.where(n <= 8, 5, n - 3))


def _proj_kernel(x_ref, nw_ref, w_ref, wsm_ref, cos_ref, sin_ref, *refs):
    ak_ref, av_ref, ck_ref, cv_ref, f_ref, b_ref, sm_ref, h_scr = refs[len(CACHE_HEADS):]
    n = pl.program_id(1)

    @pl.when(n == 0)
    def _():
        x = x_ref[...]
        y = x * lax.rsqrt(jnp.mean(x * x, axis=-1, keepdims=True) + EPS) * nw_ref[...]
        hb = y.astype(BF16)
        h_scr[...] = hb
        sm_ref[...] = _dot(hb, wsm_ref[0])

    def mm():
        return _dot(h_scr[...], w_ref[0, 0])

    def rope(acc):
        cos = cos_ref[...]
        sin = sin_ref[...]
        slabs = [acc[:, s * LANES:(s + 1) * LANES] for s in range(PROJ_TN // LANES)]
        return jnp.concatenate([sl * cos + pltpu.roll(sl, LANES // 2, 1) * sin for sl in slabs], axis=1)

    def cache_out(o_ref, a):
        o_ref[0] = a.reshape(o_ref.shape[1:])
        b_ref[0] = a.astype(BF16)

    @pl.when(n == 0)
    def _():
        b_ref[0] = (rope(mm()) * QA_SCALE).astype(BF16)

    @pl.when(n == 1)
    def _():
        cache_out(ak_ref, rope(mm()))

    @pl.when(n == 2)
    def _():
        cache_out(av_ref, mm())

    @pl.when(n == 3)
    def _():
        b_ref[0] = (mm() * QC_SCALE).astype(BF16)

    @pl.when(n == 4)
    def _():
        cache_out(ck_ref, mm())

    @pl.when(n == 5)
    def _():
        cache_out(cv_ref, mm())

    @pl.when((n >= 6) & (n <= 8))
    def _():
        f_ref[0] = mm()

    @pl.when(n >= 9)
    def _():
        b_ref[0] = mm().astype(BF16)


def _in_proj(x2d, norm_w, w_blocks, w_small, layer, cos_t, sin_t, caches, tm):
    T = x2d.shape[0]
    tbl_blocks = cos_t.shape[0] // tm
    n_fixed = 6
    cache_specs = [pl.BlockSpec((1, tm) + hd, lambda m, n: (layer, m, 0, 0), pipeline_mode=pl.Buffered(1))
                   for hd in CACHE_HEADS]
    outs = pl.pallas_call(
        _proj_kernel,
        grid=(T // tm, len(PROJ_GROUPS)),
        in_specs=[
            pl.BlockSpec((tm, D_MODEL), lambda m, n: (m, 0)),
            pl.BlockSpec((1, D_MODEL), lambda m, n: (0, 0)),
            pl.BlockSpec((1, 1, D_MODEL, PROJ_TN), lambda m, n: (layer, n, 0, 0)),
            pl.BlockSpec((1, D_MODEL, LANES), lambda m, n: (layer, 0, 0)),
            pl.BlockSpec((tm, LANES), lambda m, n: (m % tbl_blocks, 0)),
            pl.BlockSpec((tm, LANES), lambda m, n: (m % tbl_blocks, 0)),
        ] + [pl.BlockSpec(memory_space=pl.ANY)] * len(CACHE_HEADS),
        out_specs=cache_specs + [
            pl.BlockSpec((1, tm, PROJ_TN), lambda m, n: (_proj_f_index(n), m, 0)),
            pl.BlockSpec((1, tm, PROJ_TN), lambda m, n: (_proj_b_index(n), m, 0)),
            pl.BlockSpec((tm, LANES), lambda m, n: (m, 0)),
        ],
        out_shape=[jax.ShapeDtypeStruct(c.shape, c.dtype) for c in caches] + [
            jax.ShapeDtypeStruct((N_BQKV, T, PROJ_TN), F32),
            jax.ShapeDtypeStruct((N_BSEG, T, PROJ_TN), BF16),
            jax.ShapeDtypeStruct((T, LANES), F32)],
        input_output_aliases={n_fixed + i: i for i in range(len(CACHE_HEADS))},
        scratch_shapes=[pltpu.VMEM((tm, D_MODEL), BF16)],
        compiler_params=_cparams(("parallel", "arbitrary"), 52),
        name="in_proj",
    )(x2d, norm_w.reshape(1, D_MODEL), w_blocks, w_small, cos_t, sin_t, *caches)
    return outs[:len(CACHE_HEADS)], outs[-3], outs[-2], outs[-1]


def _attn_a_kernel(q_ref, k_ref, v_ref, lq1_ref, lk1_ref, lq2_ref, lk2_ref, sw_ref, o_ref,
                   m_scr, l_scr, acc_scr, *, tq, tk, rows, q_off, n_valid, lam_init):
    i = pl.program_id(2)
    q = q_ref[0, 0]
    first_q = q_off + i * tq
    last_q = first_q + tq - 1
    vis_first = jnp.minimum(((first_q >> CHUNK_SHIFT) + 1) << CHUNK_SHIFT, n_valid)
    vis_last = jnp.minimum(((last_q >> CHUNK_SHIFT) + 1) << CHUNK_SHIFT, n_valid)
    n_full = vis_first // tk
    n_blocks = (vis_last + tk - 1) // tk

    m_scr[...] = jnp.full(m_scr.shape, -jnp.inf, F32)
    l_scr[...] = jnp.zeros(l_scr.shape, F32)
    acc_scr[...] = jnp.zeros(acc_scr.shape, F32)
    rg = tq // rows
    streams = [(r, c) for r in range(rows) for c in range(2)]
    ss = range(len(streams))
    qs = [q[r * rg:(r + 1) * rg, c * HD_A:(c + 1) * HD_A] for r, c in streams]
    qpos = [first_q + r * rg + lax.broadcasted_iota(jnp.int32, (rg, 1), 0) for r in range(rows)]
    nch = tk // LANES
    nacc = DV_A // LANES

    def block(j, masked):
        base = pl.multiple_of(j * tk, tk)
        k = k_ref[0, 0, pl.ds(base, tk), :]
        v = v_ref[0, 0, pl.ds(base, tk), :]
        kc = [k[:, c * HD_A:(c + 1) * HD_A] for c in range(2)]
        s = [_dot_nt(qs[n], kc[streams[n][1]]) for n in ss]
        if masked:
            kpos = base + lax.broadcasted_iota(jnp.int32, (1, tk), 1)
            kchunk = kpos >> CHUNK_SHIFT
            mask = [(kchunk <= (qpos[r] >> CHUNK_SHIFT)) & (kpos < n_valid) for r in range(rows)]
            s = [jnp.where(mask[streams[n][0]], s[n], -jnp.inf) for n in ss]
        chunks = [[s[n][:, t * LANES:(t + 1) * LANES] for t in range(nch)] for n in ss]
        m_prev = [m_scr[n] for n in ss]
        m_new = [jnp.maximum(m_prev[n], jnp.max(functools.reduce(jnp.maximum, chunks[n]), axis=1, keepdims=True))
                 for n in ss]
        alpha = [jnp.exp2(m_prev[n] - m_new[n]) for n in ss]
        ps = [[jnp.exp2((ch - m_new[n]).astype(BF16)) for ch in chunks[n]] for n in ss]
        for n in ss:
            tot = functools.reduce(jnp.add, ps[n]).astype(F32)
            l_scr[n] = alpha[n] * l_scr[n] + jnp.sum(tot, axis=1, keepdims=True)
            m_scr[n] = m_new[n]
        p = [jnp.concatenate(ps[n], axis=1) for n in ss]
        pv = [_dot(p[n], v) for n in ss]
        for n in ss:
            for t in range(nacc):
                sl = slice(t * LANES, (t + 1) * LANES)
                acc_scr[n, :, sl] = alpha[n] * acc_scr[n, :, sl] + pv[n][:, sl]

    def full_body(j, carry):
        block(j, False)
        return carry

    def edge_body(j, carry):
        block(j, True)
        return carry

    def pair_body(jj, carry):
        block(2 * jj, False)
        block(2 * jj + 1, False)
        return carry

    n_pairs = n_full // 2
    lax.fori_loop(0, n_pairs, pair_body, 0)
    lax.fori_loop(2 * n_pairs, n_full, full_body, 0)
    lax.fori_loop(n_full, n_blocks, edge_body, 0)

    lam = (jnp.exp(jnp.sum(lq1_ref[...] * lk1_ref[...], axis=1, keepdims=True))
           - jnp.exp(jnp.sum(lq2_ref[...] * lk2_ref[...], axis=1, keepdims=True)) + lam_init)
    for r in range(rows):
        parts = []
        for t in range(nacc):
            sl = slice(t * LANES, (t + 1) * LANES)
            parts.append(acc_scr[2 * r, :, sl] / l_scr[2 * r] - lam * (acc_scr[2 * r + 1, :, sl] / l_scr[2 * r + 1]))
        o = jnp.concatenate(parts, axis=1)
        o = o * lax.rsqrt(jnp.mean(o * o, axis=-1, keepdims=True) + EPS) * sw_ref[...]
        o_ref[0, r * rg:(r + 1) * rg, :] = o * (1.0 - lam_init)


def _attn_a(q, k, v, lq1, lk1, lq2, lk2, subln_w, *, tq, tk, rows, q_off, n_valid, lam_init):
    (qa, qs), (ka, ks), (va, vs) = q, k, v
    _, B, Lq, _ = qa.shape
    Lk = ka.shape[2]
    vec = pl.BlockSpec((1, HD_A), lambda b, h, i: (0, 0))
    return pl.pallas_call(
        functools.partial(_attn_a_kernel, tq=tq, tk=tk, rows=rows, q_off=q_off, n_valid=n_valid,
                          lam_init=lam_init),
        grid=(B, H_A, Lq // tq),
        in_specs=[
            pl.BlockSpec((1, 1, tq, 2 * HD_A), lambda b, h, i: (qs, b, i, h)),
            pl.BlockSpec((1, 1, Lk, 2 * HD_A), lambda b, h, i: (ks, b, 0, h)),
            pl.BlockSpec((1, 1, Lk, DV_A), lambda b, h, i: (vs, b, 0, h)),
            vec, vec, vec, vec,
            pl.BlockSpec((1, DV_A), lambda b, h, i: (0, 0)),
        ],
        out_specs=pl.BlockSpec((1, tq, DV_A), lambda b, h, i: (b, i, h)),
        out_shape=jax.ShapeDtypeStruct((B, Lq, H_A * DV_A), F32),
        scratch_shapes=[pltpu.VMEM((2 * rows, tq // rows, LANES), F32),
                        pltpu.VMEM((2 * rows, tq // rows, LANES), F32),
                        pltpu.VMEM((2 * rows, tq // rows, DV_A), F32)],
        compiler_params=_cparams(("parallel", "parallel", "arbitrary"), 40),
        name="attn_a",
    )(qa, ka, va, lq1.reshape(1, HD_A), lk1.reshape(1, HD_A), lq2.reshape(1, HD_A), lk2.reshape(1, HD_A),
      subln_w.reshape(1, DV_A))


C_HEADS = 4


def _attn_c_kernel(q_ref, k_ref, v_ref, o_ref, acc_scr, run_scr, *, tq, tk, q_off):
    i = pl.program_id(2)
    q = q_ref[0, 0]
    first_q = q_off + i * tq
    qpos = first_q + lax.broadcasted_iota(jnp.int32, (tq, 1), 0)
    j_start = (q_off + (i + 1) * tq - 2) // tk
    acc_scr[...] = jnp.zeros(acc_scr.shape, F32)
    run_scr[...] = jnp.zeros(run_scr.shape, F32)
    gs = range(C_HEADS)
    sl = [slice(g * HD_C, (g + 1) * HD_C) for g in gs]

    def cum_weights(w):
        rj = lax.broadcasted_iota(jnp.int32, (w, w + tk), 0)
        cs = lax.broadcasted_iota(jnp.int32, (w, w + tk), 1)
        return jnp.where((cs >= w) | (rj > cs), 1.0, 0.0).astype(BF16)

    def pre(base, w, masked):
        cum_w = cum_weights(w)
        z = [_dot_nt(q[:, sl[g]], k_ref[0, 0, pl.ds(base, w), sl[g]]) for g in gs]
        t = [jnp.log(1.0 + jnp.exp(-jnp.abs(z[g]))) for g in gs]
        log_beta = [jnp.minimum(z[g], 0.0) - t[g] for g in gs]
        log_1m = [jnp.minimum(-z[g], 0.0) - t[g] for g in gs]
        mask = None
        if masked:
            mask = (base + lax.broadcasted_iota(jnp.int32, (1, w), 1)) < qpos
            log_1m = [jnp.where(mask, log_1m[g], 0.0) for g in gs]
        parts = [_split_bf16(log_1m[g]) for g in gs]
        cum = [_dot(parts[g][0], cum_w) + _dot(parts[g][1], cum_w) for g in gs]
        return base, w, mask, log_beta, cum

    def post(state):
        base, w, mask, log_beta, cum = state
        run = [run_scr[g] for g in gs]
        run_w = [jnp.concatenate([run[g]] * (w // tk), axis=1) for g in gs]
        a = [jnp.exp(log_beta[g] + cum[g][:, :w] + run_w[g]) for g in gs]
        if mask is not None:
            a = [jnp.where(mask, a[g], 0.0) for g in gs]
        new_run = [run[g] + cum[g][:, w:] for g in gs]
        for g in gs:
            acc_scr[g] += _dot(a[g].astype(BF16), v_ref[0, 0, pl.ds(base, w), sl[g]])
            run_scr[g] = new_run[g]
        top = functools.reduce(jnp.maximum, new_run)
        return (jnp.max(top) > EXP_UNDERFLOW).astype(jnp.int32)

    def block(base, w, masked):
        return post(pre(base, w, masked))

    wide = 2 * tk
    end0 = (j_start + 1) * tk

    def fused_start(_):
        base_e = pl.multiple_of(end0 - tk, tk)
        base_w = pl.multiple_of(end0 - tk - wide, tk)
        edge = pre(base_e, tk, True)
        left = pre(base_w, wide, False)
        post(edge)
        return base_w, post(left)

    def edge_body(c):
        base = pl.multiple_of(c[0] - tk, tk)
        return base, block(base, tk, True)

    def wide_body(c):
        base = pl.multiple_of(c[0] - wide, tk)
        return base, block(base, wide, False)

    def last_body(c):
        base = pl.multiple_of(c[0] - tk, tk)
        return base, block(base, tk, False)

    one_edge_then_wide = (end0 - tk <= first_q) & (end0 - tk >= wide)
    c = lax.cond(one_edge_then_wide, fused_start, lambda _: (end0, jnp.int32(1)), 0)
    c = lax.while_loop(lambda c: (c[0] > 0) & (c[1] > 0) & (c[0] > first_q), edge_body, c)
    c = lax.while_loop(lambda c: (c[0] >= wide) & (c[1] > 0), wide_body, c)
    lax.while_loop(lambda c: (c[0] > 0) & (c[1] > 0), last_body, c)
    for g in gs:
        o_ref[0, :, sl[g]] = acc_scr[g]


def _attn_c(q, k, v, *, tq, tk, q_off):
    (qa, qs), (ka, ks), (va, vs) = q, k, v
    _, B, Lq, _ = qa.shape
    Lk = ka.shape[2]
    w = C_HEADS * HD_C
    assert tk == LANES
    return pl.pallas_call(
        functools.partial(_attn_c_kernel, tq=tq, tk=tk, q_off=q_off),
        grid=(B, H_C // C_HEADS, Lq // tq),
        in_specs=[
            pl.BlockSpec((1, 1, tq, w), lambda b, g, i: (qs, b, i, g)),
            pl.BlockSpec((1, 1, Lk, w), lambda b, g, i: (ks, b, 0, g)),
            pl.BlockSpec((1, 1, Lk, w), lambda b, g, i: (vs, b, 0, g)),
        ],
        out_specs=pl.BlockSpec((1, tq, w), lambda b, g, i: (b, i, g)),
        out_shape=jax.ShapeDtypeStruct((B, Lq, H_C * HD_C), F32),
        scratch_shapes=[pltpu.VMEM((C_HEADS, tq, HD_C), F32), pltpu.VMEM((C_HEADS, tq, tk), F32)],
        compiler_params=_cparams(("parallel", "parallel", "arbitrary"), 48),
        name="attn_c",
    )(qa, ka, va)


def _gates_kernel(sm_ref, alog_ref, dtb_ref, o_ref, *, r, valid):
    blk = pl.program_id(1)
    x = sm_ref[0]
    y = x + dtb_ref[...]
    g = -jnp.exp(alog_ref[...]) * (jnp.maximum(y, 0.0) + _softplus_neg_abs(y))
    beta = _sigmoid(x)
    row = blk * r + lax.broadcasted_iota(jnp.int32, (r, 1), 0)
    live = row < valid
    g = jnp.where(live, g, 0.0)
    beta = jnp.where(live, beta, 0.0)
    ri = lax.broadcasted_iota(jnp.int32, (r, r), 0)
    ci = lax.broadcasted_iota(jnp.int32, (r, r), 1)
    tri = jnp.where(((ri >> CHUNK_SHIFT) == (ci >> CHUNK_SHIFT)) & (ci <= ri), 1.0, 0.0)
    gc = jnp.dot(tri, g, precision=lax.Precision.HIGHEST, preferred_element_type=F32)
    lane = lax.broadcasted_iota(jnp.int32, (r, LANES), 1)
    o_ref[0] = jnp.where(lane < H_B, gc, beta)


def _gdn_gates(small, a_log, dt_bias, r, valid):
    B, L, _ = small.shape
    pad = lambda v: jnp.pad(v.astype(F32), (0, LANES - H_B)).reshape(1, LANES)
    return pl.pallas_call(
        functools.partial(_gates_kernel, r=r, valid=valid),
        grid=(B, L // r),
        in_specs=[
            pl.BlockSpec((1, r, LANES), lambda b, i: (b, i, 0)),
            pl.BlockSpec((1, LANES), lambda b, i: (0, 0)),
            pl.BlockSpec((1, LANES), lambda b, i: (0, 0)),
        ],
        out_specs=pl.BlockSpec((1, r, LANES), lambda b, i: (b, i, 0)),
        out_shape=jax.ShapeDtypeStruct((B, L, LANES), F32),
        compiler_params=_cparams(("parallel", "parallel"), 32),
        name="gdn_gates",
    )(small, pad(a_log), pad(dt_bias))


def _gdn_kernel(raw0_ref, raw1_ref, raw2_ref, hist_ref, cw_ref, gb_ref, gbt_ref, bz_ref, nw_ref, s0_ref,
                o_ref, s_out_ref, s_scr, xs_scr, *, r):
    blk = pl.program_id(1)
    nblk = pl.num_programs(1)

    @pl.when(blk == 0)
    def _():
        s_scr[...] = s0_ref[0]
        xs_scr[0:SUBLANES, :] = hist_ref[0]

    @pl.when(blk > 0)
    def _():
        xs_scr[0:SUBLANES, :] = xs_scr[r:r + SUBLANES, :]

    for part, raw_ref in enumerate((raw0_ref, raw1_ref, raw2_ref)):
        xs_scr[SUBLANES:, part * PROJ_TN:(part + 1) * PROJ_TN] = raw_ref[0, 0]

    def conv_head(col, l2_scale):
        first = SUBLANES - (CONV_W - 1)
        cols = slice(col * HD_B, (col + 1) * HD_B)
        acc = xs_scr[first:first + r, cols] * cw_ref[0:1, cols]
        for t in range(1, CONV_W):
            acc = acc + xs_scr[first + t:first + t + r, cols] * cw_ref[t:t + 1, cols]
        y = acc * _sigmoid(acc)
        if l2_scale is None:
            return y
        return y * (lax.rsqrt(jnp.sum(y * y, axis=-1, keepdims=True) + EPS) * l2_scale)

    gb = gb_ref[0]
    gbt = gbt_ref[0]
    ri = lax.broadcasted_iota(jnp.int32, (r, r), 0)
    ci = lax.broadcasted_iota(jnp.int32, (r, r), 1)
    same = (ri >> CHUNK_SHIFT) == (ci >> CHUNK_SHIFT)
    tri = same & (ci <= ri)
    strict = same & (ci < ri)
    pair = (ri >> 1) == (ci >> 1)
    off_masks = [((ri >> (lg + 1)) == (ci >> (lg + 1))) & ((ri >> lg) != (ci >> lg))
                 for lg in range(1, CHUNK_SHIFT)]
    nw = nw_ref[...]

    hs = range(H_B)
    q = [conv_head(h, HD_B ** -0.5) for h in hs]
    k = [conv_head(H_B + h, 1.0) for h in hs]
    v = [conv_head(2 * H_B + h, None) for h in hs]
    gc_col = [gb[:, h:h + 1] for h in hs]
    beta_col = [gb[:, H_B + h:H_B + h + 1] for h in hs]
    decay = [jnp.where(tri, jnp.exp(jnp.where(tri, gc_col[h] - gbt[h:h + 1, :], 0.0)), 0.0) for h in hs]
    kbf = [k[h].astype(BF16) for h in hs]
    kb = [k[h] * beta_col[h] for h in hs]
    m = [jnp.where(strict, _dot_nt(kb[h].astype(BF16), kbf[h]) * decay[h], 0.0) for h in hs]
    x = [-jnp.where(pair, m[h], 0.0) for h in hs]
    for off_mask in off_masks:
        off = [jnp.where(off_mask, m[h], 0.0) for h in hs]
        xb = [x[h].astype(BF16) for h in hs]
        t = [off[h] + _dot(xb[h], off[h].astype(BF16)) for h in hs]
        x = [x[h] - t[h] - _dot(t[h].astype(BF16), xb[h]) for h in hs]
    egc = [jnp.exp(gc_col[h]) for h in hs]
    rhs = [jnp.concatenate([v[h] * beta_col[h], kb[h] * egc[h]], axis=1) for h in hs]
    sol = [rhs[h] + _dot(x[h].astype(BF16), rhs[h].astype(BF16)) for h in hs]
    aqk = [jnp.where(tri, _dot_nt(q[h].astype(BF16), kbf[h]) * decay[h], 0.0).astype(BF16) for h in hs]
    q_in = [(q[h] * egc[h]).astype(BF16) for h in hs]

    s = [s_scr[h] for h in hs]
    outs = [[] for _ in hs]
    for c in range(r // CHUNK):
        lo, hi = c * CHUNK, (c + 1) * CHUNK
        g_last = [gc_col[h][hi - 1:hi, :] for h in hs]
        sb = [s[h].astype(BF16) for h in hs]
        u = [sol[h][lo:hi, :HD_B] - _dot(sol[h][lo:hi, HD_B:].astype(BF16), sb[h]) for h in hs]
        ub = [u[h].astype(BF16) for h in hs]
        for h in hs:
            outs[h].append(_dot(q_in[h][lo:hi], sb[h]) + _dot(aqk[h][lo:hi, lo:hi], ub[h]))
        k_out = [(k[h][lo:hi] * jnp.exp(g_last[h] - gc_col[h][lo:hi])).astype(BF16) for h in hs]
        s = [s[h] * jnp.exp(g_last[h]) + _dot_tn(k_out[h], ub[h]) for h in hs]
    for h in hs:
        s_scr[h] = s[h]
        o = outs[h][0] if len(outs[h]) == 1 else jnp.concatenate(outs[h], axis=0)
        o = o * lax.rsqrt(jnp.mean(o * o, axis=-1, keepdims=True) + EPS) * nw
        z = bz_ref[0, 0, :, h * HD_B:(h + 1) * HD_B].astype(F32)
        o_ref[0, :, h * HD_B:(h + 1) * HD_B] = o * (z * _sigmoid(z))

    @pl.when(blk == nblk - 1)
    def _():
        s_out_ref[0] = s_scr[...]


def _gdn(raw, hist8, conv_w, gb, bz, norm_w, s0, r):
    _, B, L, _ = raw.shape
    bza, bzs = bz
    gbt = jnp.swapaxes(gb[:, :, :2 * H_B], 1, 2)
    raw_spec = lambda part: pl.BlockSpec((1, 1, r, PROJ_TN), lambda b, i: (part, b, i, 0))
    return pl.pallas_call(
        functools.partial(_gdn_kernel, r=r),
        grid=(B, L // r),
        in_specs=[
            raw_spec(0), raw_spec(1), raw_spec(2),
            pl.BlockSpec((1, SUBLANES, 3 * W_B), lambda b, i: (b, 0, 0)),
            pl.BlockSpec((CONV_W, 3 * W_B), lambda b, i: (0, 0)),
            pl.BlockSpec((1, r, LANES), lambda b, i: (b, i, 0)),
            pl.BlockSpec((1, 2 * H_B, r), lambda b, i: (b, 0, i)),
            pl.BlockSpec((1, 1, r, W_B), lambda b, i: (bzs, b, i, 0)),
            pl.BlockSpec((1, HD_B), lambda b, i: (0, 0)),
            pl.BlockSpec((1, H_B, HD_B, HD_B), lambda b, i: (b, 0, 0, 0)),
        ],
        out_specs=[
            pl.BlockSpec((1, r, W_B), lambda b, i: (b, i, 0)),
            pl.BlockSpec((1, H_B, HD_B, HD_B), lambda b, i: (b, 0, 0, 0)),
        ],
        out_shape=[jax.ShapeDtypeStruct((B, L, W_B), F32),
                   jax.ShapeDtypeStruct((B, H_B, HD_B, HD_B), F32)],
        scratch_shapes=[pltpu.VMEM((H_B, HD_B, HD_B), F32), pltpu.VMEM((r + SUBLANES, 3 * W_B), F32)],
        compiler_params=_cparams(("parallel", "arbitrary"), 40),
        name="gdn_delta",
    )(raw, raw, raw, hist8, conv_w, gb, gbt, bza, norm_w.reshape(1, HD_B), s0)


MLP_FC = 1024


def _merge_mlp_kernel(x_ref, oa_ref, ob_ref, oc_ref, g0_ref, g1_ref, g2_ref, wo_ref, n2_ref, wu_ref, wd_ref,
                      fw_ref, o_ref, *, final):
    merged = (_sigmoid(g0_ref[0].astype(F32)) * oa_ref[...]
              + _sigmoid(g1_ref[0].astype(F32)) * ob_ref[...]
              + _sigmoid(g2_ref[0].astype(F32)) * oc_ref[...])
    x1 = x_ref[...] + _dot(merged.astype(BF16), wo_ref[...])
    h = (x1 * lax.rsqrt(jnp.mean(x1 * x1, axis=-1, keepdims=True) + EPS) * n2_ref[...]).astype(BF16)
    acc = x1
    for c in range(D_FF // MLP_FC):
        up = _dot(h, wu_ref[:, c * MLP_FC:(c + 1) * MLP_FC])
        up = jnp.square(jnp.maximum(up, 0.0))
        acc = acc + _dot(up.astype(BF16), wd_ref[c * MLP_FC:(c + 1) * MLP_FC, :])
    if final:
        acc = acc * lax.rsqrt(jnp.mean(acc * acc, axis=-1, keepdims=True) + EPS) * fw_ref[...]
    o_ref[...] = acc


def _merge_mlp(x2d, oa, ob, oc, bseg, w_out_bf, norm2_w, w_up_bf, w_down_bf, final_w, tm, final):
    T = x2d.shape[0]
    row = pl.BlockSpec((tm, D_MODEL), lambda m: (m, 0))
    gate = lambda s: pl.BlockSpec((1, tm, D_MODEL), lambda m: (B_G + s, m, 0))
    const = lambda shape: pl.BlockSpec(shape, lambda m: (0, 0), pipeline_mode=pl.Buffered(1))
    return pl.pallas_call(
        functools.partial(_merge_mlp_kernel, final=final),
        grid=(T // tm,),
        in_specs=[row, row, row, row, gate(0), gate(1), gate(2),
                  const((D_MODEL, D_MODEL)), const((1, D_MODEL)),
                  const((D_MODEL, D_FF)), const((D_FF, D_MODEL)), const((1, D_MODEL))],
        out_specs=row,
        out_shape=jax.ShapeDtypeStruct((T, D_MODEL), F32),
        compiler_params=_cparams(("parallel",), 52),
        name="merge_mlp",
    )(x2d, oa, ob, oc, bseg, bseg, bseg, w_out_bf, norm2_w.reshape(1, D_MODEL), w_up_bf, w_down_bf,
      final_w.reshape(1, D_MODEL))


def _rope_tables(pos):
    half = HD_A // 2
    inv = ROPE_THETA ** (-jnp.arange(half, dtype=F32) / half)
    ang = pos.astype(F32)[:, None] * inv[None, :]
    cos, sin = jnp.cos(ang), jnp.sin(ang)
    return jnp.concatenate([cos, cos], axis=-1), jnp.concatenate([-sin, sin], axis=-1)


def _pad_rows(a, n):
    return jnp.pad(a, ((0, 0), (0, n - a.shape[1]), (0, 0)))


def _round_up(n, m):
    return -(-n // m) * m


def _layer(x, pos0, past, weights, layer_idx, final_w, final, caches):
    (norm1_w, w_blocks, w_small, lq1, lk1, lq2, lk2, subln_w, conv_w, a_log, dt_bias,
     gdn_norm_w, w_out_bf, norm2_w, w_up_bf, w_down_bf) = weights
    B, L, _ = x.shape
    T = B * L
    prompt = past is None
    lam_init = 0.8 - 0.6 * math.exp(-0.3 * layer_idx)
    x2d = x.reshape(T, D_MODEL)

    cos_t, sin_t = _rope_tables(pos0 + jnp.arange(L))
    tm = min(1024, T)
    if L % tm:
        cos_t, sin_t = jnp.tile(cos_t, (B, 1)), jnp.tile(sin_t, (B, 1))
    caches, fseg, bseg, small = _in_proj(x2d, norm1_w, w_blocks, w_small, layer_idx, cos_t, sin_t, caches, tm)
    fseg4 = fseg.reshape(N_BQKV, B, L, PROJ_TN)
    bseg4 = bseg.reshape(N_BSEG, B, L, PROJ_TN)
    small = small.reshape(B, L, LANES)

    def with_past(past_arr, seg, lk):
        past_len = past_arr.shape[1]
        allk = jnp.concatenate([past_arr.reshape(B, past_len, -1).astype(BF16), bseg4[seg]], axis=1)
        return _pad_rows(allk, lk)[None], 0

    if prompt:
        oa = _attn_a((bseg4, B_AQ), (bseg4, B_AK), (bseg4, B_AV), lq1, lk1, lq2, lk2, subln_w,
                     tq=512, tk=512, rows=2, q_off=0, n_valid=L, lam_init=lam_init)
    else:
        past_len = past[0].shape[1]
        lk = _round_up(past_len + L, LANES)
        oa = _attn_a((bseg4, B_AQ), with_past(past[0], B_AK, lk), with_past(past[1], B_AV, lk),
                     lq1, lk1, lq2, lk2, subln_w, tq=L, tk=lk, rows=1, q_off=past_len, n_valid=past_len + L,
                     lam_init=lam_init)

    hist8 = jnp.zeros((B, SUBLANES, 3 * W_B), F32)
    if not prompt:
        hist8 = hist8.at[:, SUBLANES - (CONV_W - 1):].set(past[4])
    tail = CONV_W - 1
    assert L >= tail
    new_conv = jnp.concatenate([fseg4[s, :, L - tail:] for s in range(N_BQKV)], axis=-1)
    if prompt:
        r = 128
        gb = _gdn_gates(small, a_log, dt_bias, min(512, L), L)
        s0 = jnp.zeros((B, H_B, HD_B, HD_B), F32)
        ob, s_new = _gdn(fseg4, hist8, conv_w, gb, (bseg4, B_BZ), gdn_norm_w, s0, r)
    else:
        lp = _round_up(L, CHUNK)
        gb = _gdn_gates(_pad_rows(small, lp), a_log, dt_bias, CHUNK, L)
        bz_pad = _pad_rows(bseg4[B_BZ], lp)[None]
        raw_pad = jnp.pad(fseg4, ((0, 0), (0, 0), (0, lp - L), (0, 0)))
        ob, s_new = _gdn(raw_pad, hist8, conv_w, gb, (bz_pad, 0), gdn_norm_w, past[5], CHUNK)
        ob = ob[:, :L]

    if prompt:
        oc = _attn_c((bseg4, B_CQ), (bseg4, B_CK), (bseg4, B_CV), tq=128, tk=128, q_off=0)
    else:
        past_len = past[2].shape[1]
        lk = _round_up(past_len + L, LANES)
        oc = _attn_c((bseg4, B_CQ), with_past(past[2], B_CK, lk), with_past(past[3], B_CV, lk),
                     tq=L, tk=LANES, q_off=past_len)

    tm2 = min(256, T)
    f2 = lambda a: a.reshape(T, a.shape[-1])
    x2 = _merge_mlp(x2d, f2(oa), f2(ob), f2(oc), bseg, w_out_bf, norm2_w, w_up_bf, w_down_bf, final_w, tm2, final)
    return x2.reshape(B, L, D_MODEL), caches, new_conv, s_new


N_ALIGNED = 6
SMALL_W = 2 * H_B
SMALL_BLOCK = N_ALIGNED * PROJ_TN // LANES


def _pack_src_block(n):
    return jnp.where(n < 3, n, jnp.where(n < 6, n + 4, jnp.where(n < 9, n - 3, jnp.where(n == 9, 6, n))))


def _pack_kernel(a_ref, b_ref, sm_ref, o_ref, osm_ref):
    n = pl.program_id(1)
    shifted = ((n >= 3) & (n < 6)) | (n >= 9)

    @pl.when(n == 0)
    def _():
        lane = lax.broadcasted_iota(jnp.int32, sm_ref.shape[1:], 1)
        osm_ref[0] = jnp.where(lane < SMALL_W, sm_ref[0], 0.0).astype(BF16)

    @pl.when(jnp.logical_not(shifted))
    def _():
        o_ref[0, 0] = a_ref[0].astype(BF16)

    @pl.when(shifted)
    def _():
        last = PROJ_TN - LANES
        o_ref[0, 0] = pltpu.roll(a_ref[0], PROJ_TN - SMALL_W, 1).astype(BF16)
        tail_a = pltpu.roll(a_ref[0, :, last:], LANES - SMALL_W, 1)
        tail_b = pltpu.roll(b_ref[0], LANES - SMALL_W, 1)
        lane = lax.broadcasted_iota(jnp.int32, tail_a.shape, 1)
        o_ref[0, 0, :, last:] = jnp.where(lane < LANES - SMALL_W, tail_a, tail_b).astype(BF16)


def _proj_weight_blocks(w_in):
    depth = w_in.shape[0]
    per = PROJ_TN // LANES
    return pl.pallas_call(
        _pack_kernel,
        grid=(depth, len(PROJ_GROUPS)),
        in_specs=[
            pl.BlockSpec((1, D_MODEL, PROJ_TN), lambda l, n: (l, 0, _pack_src_block(n))),
            pl.BlockSpec((1, D_MODEL, LANES), lambda l, n: (l, 0, (_pack_src_block(n) + 1) * per)),
            pl.BlockSpec((1, D_MODEL, LANES), lambda l, n: (l, 0, SMALL_BLOCK)),
        ],
        out_specs=[
            pl.BlockSpec((1, 1, D_MODEL, PROJ_TN), lambda l, n: (l, n, 0, 0)),
            pl.BlockSpec((1, D_MODEL, LANES), lambda l, n: (l, 0, 0)),
        ],
        out_shape=[jax.ShapeDtypeStruct((depth, len(PROJ_GROUPS), D_MODEL, PROJ_TN), BF16),
                   jax.ShapeDtypeStruct((depth, D_MODEL, LANES), BF16)],
        compiler_params=_cparams(("parallel", "arbitrary"), 40),
        name="pack_w_in",
    )(w_in, w_in, w_in)


def kernel(x_prompt, x_sample, cache_a_k, cache_a_v, cache_c_k, cache_c_v, state_b_conv, state_b_ssm,
           norm1_w, w_in, lam_q1, lam_k1, lam_q2, lam_k2, subln_w, conv_w, a_log, dt_bias,
           gdn_norm_w, w_out, norm2_w, w_up, w_down, final_norm_w):
    depth = w_in.shape[0]
    past_len = cache_a_k.shape[2]
    xp, xs = x_prompt, x_sample
    bp, lp = x_prompt.shape[0], x_prompt.shape[1]
    bs, ls = x_sample.shape[0], x_sample.shape[1]
    p_out = [[] for _ in range(2)]
    s_out = [[] for _ in range(2)]
    p_caches = [lax.empty((depth, bp * lp) + hd, F32) for hd in CACHE_HEADS]
    s_caches = [lax.empty((depth, bs * ls) + hd, F32) for hd in CACHE_HEADS]
    w_blocks, w_small = _proj_weight_blocks(w_in)
    for l in range(depth):
        weights = (norm1_w[l], w_blocks, w_small, lam_q1[l], lam_k1[l], lam_q2[l], lam_k2[l], subln_w[l],
                   conv_w[l], a_log[l], dt_bias[l], gdn_norm_w[l], w_out[l].astype(BF16), norm2_w[l],
                   w_up[l].astype(BF16), w_down[l].astype(BF16))
        final = l == depth - 1
        xp, p_caches, *rest = _layer(xp, 0, None, weights, l, final_norm_w, final, p_caches)
        for lst, a in zip(p_out, rest):
            lst.append(a)
        past = (cache_a_k[l], cache_a_v[l], cache_c_k[l], cache_c_v[l], state_b_conv[l], state_b_ssm[l])
        xs, s_caches, *rest = _layer(xs, past_len, past, weights, l, final_norm_w, final, s_caches)
        for lst, a in zip(s_out, rest):
            lst.append(a)

    def pack(caches, lists, batch, length):
        lead = (depth, batch, length)
        return tuple(c.reshape(lead + c.shape[2:]) for c in caches) + (jnp.stack(lists[0]), jnp.stack(lists[1]))

    return (xp, xs) + pack(p_caches, p_out, bp, lp) + pack(s_caches, s_out, bs, ls)
```

```python
import functools
import math

import jax
import jax.numpy as jnp
from jax import lax
from jax.experimental import pallas as pl
from jax.experimental.pallas import tpu as pltpu

F32 = jnp.float32
BF16 = jnp.bfloat16

D_MODEL = 1024
CHUNK = 64
CHUNK_SHIFT = 6
H_A, HD_A, DV_A = 4, 128, 256
H_B, HD_B = 8, 128
H_C, HD_C = 8, 128
W_B = H_B * HD_B
CONV_W = 4
D_FF = 4 * D_MODEL
ROPE_THETA = 10000.0
EPS = 1e-6
LANES = 128
SUBLANES = 8
MIB = 1024 * 1024
LOG2E = 1.4426950408889634
EXP_UNDERFLOW = -104.0


def _cparams(sem, vmem_mib):
    return pltpu.CompilerParams(dimension_semantics=sem, vmem_limit_bytes=vmem_mib * MIB)


def _sigmoid(x):
    return 1.0 / (1.0 + jnp.exp(-x))


def _softplus_neg_abs(x):
    return jnp.log1p(jnp.exp(-jnp.abs(x)))


def _dot(a, b):
    return jnp.dot(a, b, preferred_element_type=F32)


def _split_bf16(a):
    hi = a.astype(BF16)
    return hi, (a - hi.astype(F32)).astype(BF16)


def _dot3(a, b):
    ah, al = _split_bf16(a)
    bh, bl = _split_bf16(b)
    return _dot(ah, bh) + _dot(ah, bl) + _dot(al, bh)


def _dot_nt(a, b):
    return lax.dot_general(a, b, (((1,), (1,)), ((), ())), preferred_element_type=F32)


def _dot_tn(a, b):
    return lax.dot_general(a, b, (((0,), (0,)), ((), ())), preferred_element_type=F32)


PROJ_TN = 1024
PROJ_GROUPS = ("aq", "ak", "av", "cq", "ck", "cv", "bqkv0", "bqkv1", "bqkv2", "bz", "g0", "g1", "g2")
N_BQKV = 3
B_AQ, B_AK, B_AV, B_CQ, B_CK, B_CV, B_BZ, B_G = 0, 1, 2, 3, 4, 5, 6, 7
N_BSEG = 10
CACHE_HEADS = ((2 * H_A, HD_A), (H_A, DV_A), (H_C, HD_C), (H_C, HD_C))
QA_SCALE = HD_A ** -0.5 * LOG2E
QC_SCALE = HD_C ** -0.5


def _proj_f_index(n):
    return jnp.clip(n - 6, 0, N_BQKV - 1)


def _proj_b_index(n):
    return jnp.where(n <= 5, n, jnp.where(n <= 8, 5, n - 3))


def _proj_kernel(x_ref, nw_ref, w_ref, wsm_ref, cos_ref, sin_ref, *refs):
    ak_ref, av_ref, ck_ref, cv_ref, f_ref, b_ref, sm_ref, h_scr = refs[len(CACHE_HEADS):]
    n = pl.program_id(1)

    @pl.when(n == 0)
    def _():
        x = x_ref[...]
        y = x * lax.rsqrt(jnp.mean(x * x, axis=-1, keepdims=True) + EPS) * nw_ref[...]
        hb = y.astype(BF16)
        h_scr[...] = hb
        sm_ref[...] = _dot(hb, wsm_ref[0])

    def mm():
        return _dot(h_scr[...], w_ref[0, 0])

    def rope(acc):
        cos = cos_ref[...]
        sin = sin_ref[...]
        slabs = [acc[:, s * LANES:(s + 1) * LANES] for s in range(PROJ_TN // LANES)]
        return jnp.concatenate([sl * cos + pltpu.roll(sl, LANES // 2, 1) * sin for sl in slabs], axis=1)

    def cache_out(o_ref, a):
        o_ref[0] = a.reshape(o_ref.shape[1:])
        b_ref[0] = a.astype(BF16)

    @pl.when(n == 0)
    def _():
        b_ref[0] = (rope(mm()) * QA_SCALE).astype(BF16)

    @pl.when(n == 1)
    def _():
        cache_out(ak_ref, rope(mm()))

    @pl.when(n == 2)
    def _():
        cache_out(av_ref, mm())

    @pl.when(n == 3)
    def _():
        b_ref[0] = (mm() * QC_SCALE).astype(BF16)

    @pl.when(n == 4)
    def _():
        cache_out(ck_ref, mm())

    @pl.when(n == 5)
    def _():
        cache_out(cv_ref, mm())

    @pl.when((n >= 6) & (n <= 8))
    def _():
        f_ref[0] = mm()

    @pl.when(n >= 9)
    def _():
        b_ref[0] = mm().astype(BF16)


def _in_proj(x2d, norm_w, w_blocks, w_small, layer, cos_t, sin_t, caches, tm):
    T = x2d.shape[0]
    tbl_blocks = cos_t.shape[0] // tm
    n_fixed = 6
    cache_specs = [pl.BlockSpec((1, tm) + hd, lambda m, n: (layer, m, 0, 0), pipeline_mode=pl.Buffered(1))
                   for hd in CACHE_HEADS]
    outs = pl.pallas_call(
        _proj_kernel,
        grid=(T // tm, len(PROJ_GROUPS)),
        in_specs=[
            pl.BlockSpec((tm, D_MODEL), lambda m, n: (m, 0)),
            pl.BlockSpec((1, D_MODEL), lambda m, n: (0, 0)),
            pl.BlockSpec((1, 1, D_MODEL, PROJ_TN), lambda m, n: (layer, n, 0, 0)),
            pl.BlockSpec((1, D_MODEL, LANES), lambda m, n: (layer, 0, 0)),
            pl.BlockSpec((tm, LANES), lambda m, n: (m % tbl_blocks, 0)),
            pl.BlockSpec((tm, LANES), lambda m, n: (m % tbl_blocks, 0)),
        ] + [pl.BlockSpec(memory_space=pl.ANY)] * len(CACHE_HEADS),
        out_specs=cache_specs + [
            pl.BlockSpec((1, tm, PROJ_TN), lambda m, n: (_proj_f_index(n), m, 0)),
            pl.BlockSpec((1, tm, PROJ_TN), lambda m, n: (_proj_b_index(n), m, 0)),
            pl.BlockSpec((tm, LANES), lambda m, n: (m, 0)),
        ],
        out_shape=[jax.ShapeDtypeStruct(c.shape, c.dtype) for c in caches] + [
            jax.ShapeDtypeStruct((N_BQKV, T, PROJ_TN), F32),
            jax.ShapeDtypeStruct((N_BSEG, T, PROJ_TN), BF16),
            jax.ShapeDtypeStruct((T, LANES), F32)],
        input_output_aliases={n_fixed + i: i for i in range(len(CACHE_HEADS))},
        scratch_shapes=[pltpu.VMEM((tm, D_MODEL), BF16)],
        compiler_params=_cparams(("parallel", "arbitrary"), 52),
        name="in_proj",
    )(x2d, norm_w.reshape(1, D_MODEL), w_blocks, w_small, cos_t, sin_t, *caches)
    return outs[:len(CACHE_HEADS)], outs[-3], outs[-2], outs[-1]


def _attn_a_kernel(q_ref, k_ref, v_ref, lq1_ref, lk1_ref, lq2_ref, lk2_ref, sw_ref, o_ref,
                   m_scr, l_scr, acc_scr, *, tq, tk, rows, q_off, n_valid, lam_init):
    i = pl.program_id(2)
    q = q_ref[0, 0]
    first_q = q_off + i * tq
    last_q = first_q + tq - 1
    vis_first = jnp.minimum(((first_q >> CHUNK_SHIFT) + 1) << CHUNK_SHIFT, n_valid)
    vis_last = jnp.minimum(((last_q >> CHUNK_SHIFT) + 1) << CHUNK_SHIFT, n_valid)
    n_full = vis_first // tk
    n_blocks = (vis_last + tk - 1) // tk

    m_scr[...] = jnp.full(m_scr.shape, -jnp.inf, F32)
    l_scr[...] = jnp.zeros(l_scr.shape, F32)
    acc_scr[...] = jnp.zeros(acc_scr.shape, F32)
    rg = tq // rows
    streams = [(r, c) for r in range(rows) for c in range(2)]
    ss = range(len(streams))
    qs = [q[r * rg:(r + 1) * rg, c * HD_A:(c + 1) * HD_A] for r, c in streams]
    qpos = [first_q + r * rg + lax.broadcasted_iota(jnp.int32, (rg, 1), 0) for r in range(rows)]
    nch = tk // LANES
    nacc = DV_A // LANES

    def block(j, masked):
        base = pl.multiple_of(j * tk, tk)
        k = k_ref[0, 0, pl.ds(base, tk), :]
        v = v_ref[0, 0, pl.ds(base, tk), :]
        kc = [k[:, c * HD_A:(c + 1) * HD_A] for c in range(2)]
        s = [_dot_nt(qs[n], kc[streams[n][1]]) for n in ss]
        if masked:
            kpos = base + lax.broadcasted_iota(jnp.int32, (1, tk), 1)
            kchunk = kpos >> CHUNK_SHIFT
            mask = [(kchunk <= (qpos[r] >> CHUNK_SHIFT)) & (kpos < n_valid) for r in range(rows)]
            s = [jnp.where(mask[streams[n][0]], s[n], -jnp.inf) for n in ss]
        chunks = [[s[n][:, t * LANES:(t + 1) * LANES] for t in range(nch)] for n in ss]
        m_prev = [m_scr[n] for n in ss]
        m_new = [jnp.maximum(m_prev[n], jnp.max(functools.reduce(jnp.maximum, chunks[n]), axis=1, keepdims=True))
                 for n in ss]
        alpha = [jnp.exp2(m_prev[n] - m_new[n]) for n in ss]
        ps = [[jnp.exp2((ch - m_new[n]).astype(BF16)) for ch in chunks[n]] for n in ss]
        for n in ss:
            tot = functools.reduce(jnp.add, ps[n]).astype(F32)
            l_scr[n] = alpha[n] * l_scr[n] + jnp.sum(tot, axis=1, keepdims=True)
            m_scr[n] = m_new[n]
        p = [jnp.concatenate(ps[n], axis=1) for n in ss]
        pv = [_dot(p[n], v) for n in ss]
        for n in ss:
            for t in range(nacc):
                sl = slice(t * LANES, (t + 1) * LANES)
                acc_scr[n, :, sl] = alpha[n] * acc_scr[n, :, sl] + pv[n][:, sl]

    def full_body(j, carry):
        block(j, False)
        return carry

    def edge_body(j, carry):
        block(j, True)
        return carry

    def pair_body(jj, carry):
        block(2 * jj, False)
        block(2 * jj + 1, False)
        return carry

    n_pairs = n_full // 2
    lax.fori_loop(0, n_pairs, pair_body, 0)
    lax.fori_loop(2 * n_pairs, n_full, full_body, 0)
    lax.fori_loop(n_full, n_blocks, edge_body, 0)

    lam = (jnp.exp(jnp.sum(lq1_ref[...] * lk1_ref[...], axis=1, keepdims=True))
           - jnp.exp(jnp.sum(lq2_ref[...] * lk2_ref[...], axis=1, keepdims=True)) + lam_init)
    for r in range(rows):
        parts = []
        for t in range(nacc):
            sl = slice(t * LANES, (t + 1) * LANES)
            parts.append(acc_scr[2 * r, :, sl] / l_scr[2 * r] - lam * (acc_scr[2 * r + 1, :, sl] / l_scr[2 * r + 1]))
        o = jnp.concatenate(parts, axis=1)
        o = o * lax.rsqrt(jnp.mean(o * o, axis=-1, keepdims=True) + EPS) * sw_ref[...]
        o_ref[0, r * rg:(r + 1) * rg, :] = o * (1.0 - lam_init)


def _attn_a(q, k, v, lq1, lk1, lq2, lk2, subln_w, *, tq, tk, rows, q_off, n_valid, lam_init):
    (qa, qs), (ka, ks), (va, vs) = q, k, v
    _, B, Lq, _ = qa.shape
    Lk = ka.shape[2]
    vec = pl.BlockSpec((1, HD_A), lambda b, h, i: (0, 0))
    return pl.pallas_call(
        functools.partial(_attn_a_kernel, tq=tq, tk=tk, rows=rows, q_off=q_off, n_valid=n_valid,
                          lam_init=lam_init),
        grid=(B, H_A, Lq // tq),
        in_specs=[
            pl.BlockSpec((1, 1, tq, 2 * HD_A), lambda b, h, i: (qs, b, i, h)),
            pl.BlockSpec((1, 1, Lk, 2 * HD_A), lambda b, h, i: (ks, b, 0, h)),
            pl.BlockSpec((1, 1, Lk, DV_A), lambda b, h, i: (vs, b, 0, h)),
            vec, vec, vec, vec,
            pl.BlockSpec((1, DV_A), lambda b, h, i: (0, 0)),
        ],
        out_specs=pl.BlockSpec((1, tq, DV_A), lambda b, h, i: (b, i, h)),
        out_shape=jax.ShapeDtypeStruct((B, Lq, H_A * DV_A), F32),
        scratch_shapes=[pltpu.VMEM((2 * rows, tq // rows, LANES), F32),
                        pltpu.VMEM((2 * rows, tq // rows, LANES), F32),
                        pltpu.VMEM((2 * rows, tq // rows, DV_A), F32)],
        compiler_params=_cparams(("parallel", "parallel", "arbitrary"), 40),
        name="attn_a",
    )(qa, ka, va, lq1.reshape(1, HD_A), lk1.reshape(1, HD_A), lq2.reshape(1, HD_A), lk2.reshape(1, HD_A),
      subln_w.reshape(1, DV_A))


C_HEADS = 8


def _attn_c_kernel(q_ref, k_ref, v_ref, o_ref, acc_scr, run_scr, *, tq, tk, q_off):
    i = pl.program_id(2)
    q = q_ref[0, 0]
    first_q = q_off + i * tq
    qpos = first_q + lax.broadcasted_iota(jnp.int32, (tq, 1), 0)
    j_start = (q_off + (i + 1) * tq - 2) // tk
    acc_scr[...] = jnp.zeros(acc_scr.shape, F32)
    run_scr[...] = jnp.zeros(run_scr.shape, F32)
    gs = range(C_HEADS)
    sl = [slice(g * HD_C, (g + 1) * HD_C) for g in gs]

    def cum_weights(w):
        rj = lax.broadcasted_iota(jnp.int32, (w, w + tk), 0)
        cs = lax.broadcasted_iota(jnp.int32, (w, w + tk), 1)
        return jnp.where((cs >= w) | (rj > cs), 1.0, 0.0).astype(BF16)

    def pre(base, w, masked):
        cum_w = cum_weights(w)
        z = [_dot_nt(q[:, sl[g]], k_ref[0, 0, pl.ds(base, w), sl[g]]) for g in gs]
        t = [jnp.log(1.0 + jnp.exp(-jnp.abs(z[g]))) for g in gs]
        log_beta = [jnp.minimum(z[g], 0.0) - t[g] for g in gs]
        log_1m = [jnp.minimum(-z[g], 0.0) - t[g] for g in gs]
        mask = None
        if masked:
            mask = (base + lax.broadcasted_iota(jnp.int32, (1, w), 1)) < qpos
            log_1m = [jnp.where(mask, log_1m[g], 0.0) for g in gs]
        parts = [_split_bf16(log_1m[g]) for g in gs]
        cum = [_dot(parts[g][0], cum_w) + _dot(parts[g][1], cum_w) for g in gs]
        return base, w, mask, log_beta, cum

    def post(state):
        base, w, mask, log_beta, cum = state
        run = [run_scr[g] for g in gs]
        run_w = [jnp.concatenate([run[g]] * (w // tk), axis=1) for g in gs]
        a = [jnp.exp(log_beta[g] + cum[g][:, :w] + run_w[g]) for g in gs]
        if mask is not None:
            a = [jnp.where(mask, a[g], 0.0) for g in gs]
        new_run = [run[g] + cum[g][:, w:] for g in gs]
        for g in gs:
            acc_scr[g] += _dot(a[g].astype(BF16), v_ref[0, 0, pl.ds(base, w), sl[g]])
            run_scr[g] = new_run[g]
        top = functools.reduce(jnp.maximum, new_run)
        return (jnp.max(top) > EXP_UNDERFLOW).astype(jnp.int32)

    def block(base, w, masked):
        return post(pre(base, w, masked))

    wide = 2 * tk
    end0 = (j_start + 1) * tk

    def fused_start(_):
        base_e = pl.multiple_of(end0 - tk, tk)
        base_w = pl.multiple_of(end0 - tk - wide, tk)
        edge = pre(base_e, tk, True)
        left = pre(base_w, wide, False)
        post(edge)
        return base_w, post(left)

    def edge_body(c):
        base = pl.multiple_of(c[0] - tk, tk)
        return base, block(base, tk, True)

    def wide_body(c):
        base = pl.multiple_of(c[0] - wide, tk)
        return base, block(base, wide, False)

    def last_body(c):
        base = pl.multiple_of(c[0] - tk, tk)
        return base, block(base, tk, False)

    one_edge_then_wide = (end0 - tk <= first_q) & (end0 - tk >= wide)
    c = lax.cond(one_edge_then_wide, fused_start, lambda _: (end0, jnp.int32(1)), 0)
    c = lax.while_loop(lambda c: (c[0] > 0) & (c[1] > 0) & (c[0] > first_q), edge_body, c)
    c = lax.while_loop(lambda c: (c[0] >= wide) & (c[1] > 0), wide_body, c)
    lax.while_loop(lambda c: (c[0] > 0) & (c[1] > 0), last_body, c)
    for g in gs:
        o_ref[0, :, sl[g]] = acc_scr[g]


def _attn_c(q, k, v, *, tq, tk, q_off):
    (qa, qs), (ka, ks), (va, vs) = q, k, v
    _, B, Lq, _ = qa.shape
    Lk = ka.shape[2]
    w = C_HEADS * HD_C
    assert tk == LANES
    return pl.pallas_call(
        functools.partial(_attn_c_kernel, tq=tq, tk=tk, q_off=q_off),
        grid=(B, H_C // C_HEADS, Lq // tq),
        in_specs=[
            pl.BlockSpec((1, 1, tq, w), lambda b, g, i: (qs, b, i, g)),
            pl.BlockSpec((1, 1, Lk, w), lambda b, g, i: (ks, b, 0, g), pipeline_mode=pl.Buffered(1)),
            pl.BlockSpec((1, 1, Lk, w), lambda b, g, i: (vs, b, 0, g), pipeline_mode=pl.Buffered(1)),
        ],
        out_specs=pl.BlockSpec((1, tq, w), lambda b, g, i: (b, i, g)),
        out_shape=jax.ShapeDtypeStruct((B, Lq, H_C * HD_C), F32),
        scratch_shapes=[pltpu.VMEM((C_HEADS, tq, HD_C), F32), pltpu.VMEM((C_HEADS, tq, tk), F32)],
        compiler_params=_cparams(("parallel", "parallel", "arbitrary"), 48),
        name="attn_c",
    )(qa, ka, va)


def _gates_kernel(sm_ref, alog_ref, dtb_ref, o_ref, *, r, valid):
    blk = pl.program_id(1)
    x = sm_ref[0]
    y = x + dtb_ref[...]
    g = -jnp.exp(alog_ref[...]) * (jnp.maximum(y, 0.0) + _softplus_neg_abs(y))
    beta = _sigmoid(x)
    row = blk * r + lax.broadcasted_iota(jnp.int32, (r, 1), 0)
    live = row < valid
    g = jnp.where(live, g, 0.0)
    beta = jnp.where(live, beta, 0.0)
    ri = lax.broadcasted_iota(jnp.int32, (r, r), 0)
    ci = lax.broadcasted_iota(jnp.int32, (r, r), 1)
    tri = jnp.where(((ri >> CHUNK_SHIFT) == (ci >> CHUNK_SHIFT)) & (ci <= ri), 1.0, 0.0)
    gc = jnp.dot(tri, g, precision=lax.Precision.HIGHEST, preferred_element_type=F32)
    lane = lax.broadcasted_iota(jnp.int32, (r, LANES), 1)
    o_ref[0] = jnp.where(lane < H_B, gc, beta)


def _gdn_gates(small, a_log, dt_bias, r, valid):
    B, L, _ = small.shape
    pad = lambda v: jnp.pad(v.astype(F32), (0, LANES - H_B)).reshape(1, LANES)
    return pl.pallas_call(
        functools.partial(_gates_kernel, r=r, valid=valid),
        grid=(B, L // r),
        in_specs=[
            pl.BlockSpec((1, r, LANES), lambda b, i: (b, i, 0)),
            pl.BlockSpec((1, LANES), lambda b, i: (0, 0)),
            pl.BlockSpec((1, LANES), lambda b, i: (0, 0)),
        ],
        out_specs=pl.BlockSpec((1, r, LANES), lambda b, i: (b, i, 0)),
        out_shape=jax.ShapeDtypeStruct((B, L, LANES), F32),
        compiler_params=_cparams(("parallel", "parallel"), 32),
        name="gdn_gates",
    )(small, pad(a_log), pad(dt_bias))


def _gdn_kernel(raw0_ref, raw1_ref, raw2_ref, hist_ref, cw_ref, gb_ref, gbt_ref, bz_ref, nw_ref, s0_ref,
                o_ref, s_out_ref, s_scr, xs_scr, *, r):
    blk = pl.program_id(1)
    nblk = pl.num_programs(1)

    @pl.when(blk == 0)
    def _():
        s_scr[...] = s0_ref[0]
        xs_scr[0:SUBLANES, :] = hist_ref[0]

    @pl.when(blk > 0)
    def _():
        xs_scr[0:SUBLANES, :] = xs_scr[r:r + SUBLANES, :]

    for part, raw_ref in enumerate((raw0_ref, raw1_ref, raw2_ref)):
        xs_scr[SUBLANES:, part * PROJ_TN:(part + 1) * PROJ_TN] = raw_ref[0, 0]

    def conv_head(col, l2_scale):
        cols = slice(col * HD_B, (col + 1) * HD_B)
        xs = xs_scr[:, cols]
        acc = xs[SUBLANES:] * cw_ref[CONV_W - 1:CONV_W, cols]
        for t in range(CONV_W - 1):
            back = pltpu.roll(xs, CONV_W - 1 - t, 0)[SUBLANES:]
            acc = acc + back * cw_ref[t:t + 1, cols]
        y = acc * _sigmoid(acc)
        if l2_scale is None:
            return y
        return y * (lax.rsqrt(jnp.sum(y * y, axis=-1, keepdims=True) + EPS) * l2_scale)

    gb = gb_ref[0]
    gbt = gbt_ref[0]
    ri = lax.broadcasted_iota(jnp.int32, (r, r), 0)
    ci = lax.broadcasted_iota(jnp.int32, (r, r), 1)
    same = (ri >> CHUNK_SHIFT) == (ci >> CHUNK_SHIFT)
    tri = same & (ci <= ri)
    strict = same & (ci < ri)
    pair = (ri >> 1) == (ci >> 1)
    off_masks = [((ri >> (lg + 1)) == (ci >> (lg + 1))) & ((ri >> lg) != (ci >> lg))
                 for lg in range(1, CHUNK_SHIFT)]
    nw = nw_ref[...]

    hs = range(H_B)
    q = [conv_head(h, HD_B ** -0.5) for h in hs]
    k = [conv_head(H_B + h, 1.0) for h in hs]
    v = [conv_head(2 * H_B + h, None) for h in hs]
    gc_col = [gb[:, h:h + 1] for h in hs]
    beta_col = [gb[:, H_B + h:H_B + h + 1] for h in hs]
    decay = [jnp.where(tri, jnp.exp(jnp.where(tri, gc_col[h] - gbt[h:h + 1, :], 0.0)), 0.0) for h in hs]
    kbf = [k[h].astype(BF16) for h in hs]
    kb = [k[h] * beta_col[h] for h in hs]
    m = [jnp.where(strict, _dot_nt(kb[h].astype(BF16), kbf[h]) * decay[h], 0.0) for h in hs]
    x = [-jnp.where(pair, m[h], 0.0) for h in hs]
    for off_mask in off_masks:
        off = [jnp.where(off_mask, m[h], 0.0) for h in hs]
        xb = [x[h].astype(BF16) for h in hs]
        t = [off[h] + _dot(xb[h], off[h].astype(BF16)) for h in hs]
        x = [x[h] - t[h] - _dot(t[h].astype(BF16), xb[h]) for h in hs]
    egc = [jnp.exp(gc_col[h]) for h in hs]
    rhs = [jnp.concatenate([v[h] * beta_col[h], kb[h] * egc[h]], axis=1) for h in hs]
    sol = [rhs[h] + _dot(x[h].astype(BF16), rhs[h].astype(BF16)) for h in hs]
    aqk = [jnp.where(tri, _dot_nt(q[h].astype(BF16), kbf[h]) * decay[h], 0.0).astype(BF16) for h in hs]
    q_in = [(q[h] * egc[h]).astype(BF16) for h in hs]

    s = [s_scr[h] for h in hs]
    outs = [[] for _ in hs]
    for c in range(r // CHUNK):
        lo, hi = c * CHUNK, (c + 1) * CHUNK
        g_last = [gc_col[h][hi - 1:hi, :] for h in hs]
        sb = [s[h].astype(BF16) for h in hs]
        u = [sol[h][lo:hi, :HD_B] - _dot(sol[h][lo:hi, HD_B:].astype(BF16), sb[h]) for h in hs]
        ub = [u[h].astype(BF16) for h in hs]
        for h in hs:
            outs[h].append(_dot(q_in[h][lo:hi], sb[h]) + _dot(aqk[h][lo:hi, lo:hi], ub[h]))
        k_out = [(k[h][lo:hi] * jnp.exp(g_last[h] - gc_col[h][lo:hi])).astype(BF16) for h in hs]
        s = [s[h] * jnp.exp(g_last[h]) + _dot_tn(k_out[h], ub[h]) for h in hs]
    for h in hs:
        s_scr[h] = s[h]
        o = outs[h][0] if len(outs[h]) == 1 else jnp.concatenate(outs[h], axis=0)
        o = o * lax.rsqrt(jnp.mean(o * o, axis=-1, keepdims=True) + EPS) * nw
        z = bz_ref[0, 0, :, h * HD_B:(h + 1) * HD_B].astype(F32)
        o_ref[0, :, h * HD_B:(h + 1) * HD_B] = o * (z * _sigmoid(z))

    @pl.when(blk == nblk - 1)
    def _():
        s_out_ref[0] = s_scr[...]


def _gdn(raw, hist8, conv_w, gb, bz, norm_w, s0, r):
    _, B, L, _ = raw.shape
    bza, bzs = bz
    gbt = jnp.swapaxes(gb[:, :, :2 * H_B], 1, 2)
    raw_spec = lambda part: pl.BlockSpec((1, 1, r, PROJ_TN), lambda b, i: (part, b, i, 0))
    return pl.pallas_call(
        functools.partial(_gdn_kernel, r=r),
        grid=(B, L // r),
        in_specs=[
            raw_spec(0), raw_spec(1), raw_spec(2),
            pl.BlockSpec((1, SUBLANES, 3 * W_B), lambda b, i: (b, 0, 0)),
            pl.BlockSpec((CONV_W, 3 * W_B), lambda b, i: (0, 0)),
            pl.BlockSpec((1, r, LANES), lambda b, i: (b, i, 0)),
            pl.BlockSpec((1, 2 * H_B, r), lambda b, i: (b, 0, i)),
            pl.BlockSpec((1, 1, r, W_B), lambda b, i: (bzs, b, i, 0)),
            pl.BlockSpec((1, HD_B), lambda b, i: (0, 0)),
            pl.BlockSpec((1, H_B, HD_B, HD_B), lambda b, i: (b, 0, 0, 0)),
        ],
        out_specs=[
            pl.BlockSpec((1, r, W_B), lambda b, i: (b, i, 0)),
            pl.BlockSpec((1, H_B, HD_B, HD_B), lambda b, i: (b, 0, 0, 0)),
        ],
        out_shape=[jax.ShapeDtypeStruct((B, L, W_B), F32),
                   jax.ShapeDtypeStruct((B, H_B, HD_B, HD_B), F32)],
        scratch_shapes=[pltpu.VMEM((H_B, HD_B, HD_B), F32), pltpu.VMEM((r + SUBLANES, 3 * W_B), F32)],
        compiler_params=_cparams(("parallel", "arbitrary"), 40),
        name="gdn_delta",
    )(raw, raw, raw, hist8, conv_w, gb, gbt, bza, norm_w.reshape(1, HD_B), s0)


MLP_FC = 1024


def _merge_mlp_kernel(x_ref, oa_ref, ob_ref, oc_ref, g0_ref, g1_ref, g2_ref, wo_ref, n2_ref, wu_ref, wd_ref,
                      fw_ref, o_ref, *, final):
    merged = (_sigmoid(g0_ref[0].astype(F32)) * oa_ref[...]
              + _sigmoid(g1_ref[0].astype(F32)) * ob_ref[...]
              + _sigmoid(g2_ref[0].astype(F32)) * oc_ref[...])
    x1 = x_ref[...] + _dot(merged.astype(BF16), wo_ref[...])
    h = (x1 * lax.rsqrt(jnp.mean(x1 * x1, axis=-1, keepdims=True) + EPS) * n2_ref[...]).astype(BF16)
    acc = x1
    for c in range(D_FF // MLP_FC):
        up = _dot(h, wu_ref[:, c * MLP_FC:(c + 1) * MLP_FC])
        up = jnp.square(jnp.maximum(up, 0.0))
        acc = acc + _dot(up.astype(BF16), wd_ref[c * MLP_FC:(c + 1) * MLP_FC, :])
    if final:
        acc = acc * lax.rsqrt(jnp.mean(acc * acc, axis=-1, keepdims=True) + EPS) * fw_ref[...]
    o_ref[...] = acc


def _merge_mlp(x2d, oa, ob, oc, bseg, w_out_bf, norm2_w, w_up_bf, w_down_bf, final_w, tm, final):
    T = x2d.shape[0]
    row = pl.BlockSpec((tm, D_MODEL), lambda m: (m, 0))
    gate = lambda s: pl.BlockSpec((1, tm, D_MODEL), lambda m: (B_G + s, m, 0))
    const = lambda shape: pl.BlockSpec(shape, lambda m: (0, 0), pipeline_mode=pl.Buffered(1))
    return pl.pallas_call(
        functools.partial(_merge_mlp_kernel, final=final),
        grid=(T // tm,),
        in_specs=[row, row, row, row, gate(0), gate(1), gate(2),
                  const((D_MODEL, D_MODEL)), const((1, D_MODEL)),
                  const((D_MODEL, D_FF)), const((D_FF, D_MODEL)), const((1, D_MODEL))],
        out_specs=row,
        out_shape=jax.ShapeDtypeStruct((T, D_MODEL), F32),
        compiler_params=_cparams(("parallel",), 52),
        name="merge_mlp",
    )(x2d, oa, ob, oc, bseg, bseg, bseg, w_out_bf, norm2_w.reshape(1, D_MODEL), w_up_bf, w_down_bf,
      final_w.reshape(1, D_MODEL))


def _rope_tables(pos):
    half = HD_A // 2
    inv = ROPE_THETA ** (-jnp.arange(half, dtype=F32) / half)
    ang = pos.astype(F32)[:, None] * inv[None, :]
    cos, sin = jnp.cos(ang), jnp.sin(ang)
    return jnp.concatenate([cos, cos], axis=-1), jnp.concatenate([-sin, sin], axis=-1)


def _pad_rows(a, n):
    return jnp.pad(a, ((0, 0), (0, n - a.shape[1]), (0, 0)))


def _round_up(n, m):
    return -(-n // m) * m


def _layer(x, pos0, past, weights, layer_idx, final_w, final, caches):
    (norm1_w, w_blocks, w_small, lq1, lk1, lq2, lk2, subln_w, conv_w, a_log, dt_bias,
     gdn_norm_w, w_out_bf, norm2_w, w_up_bf, w_down_bf) = weights
    B, L, _ = x.shape
    T = B * L
    prompt = past is None
    lam_init = 0.8 - 0.6 * math.exp(-0.3 * layer_idx)
    x2d = x.reshape(T, D_MODEL)

    cos_t, sin_t = _rope_tables(pos0 + jnp.arange(L))
    tm = min(1024, T)
    if L % tm:
        cos_t, sin_t = jnp.tile(cos_t, (B, 1)), jnp.tile(sin_t, (B, 1))
    caches, fseg, bseg, small = _in_proj(x2d, norm1_w, w_blocks, w_small, layer_idx, cos_t, sin_t, caches, tm)
    fseg4 = fseg.reshape(N_BQKV, B, L, PROJ_TN)
    bseg4 = bseg.reshape(N_BSEG, B, L, PROJ_TN)
    small = small.reshape(B, L, LANES)

    def with_past(past_arr, seg, lk):
        past_len = past_arr.shape[1]
        allk = jnp.concatenate([past_arr.reshape(B, past_len, -1).astype(BF16), bseg4[seg]], axis=1)
        return _pad_rows(allk, lk)[None], 0

    if prompt:
        oa = _attn_a((bseg4, B_AQ), (bseg4, B_AK), (bseg4, B_AV), lq1, lk1, lq2, lk2, subln_w,
                     tq=512, tk=512, rows=2, q_off=0, n_valid=L, lam_init=lam_init)
    else:
        past_len = past[0].shape[1]
        lk = _round_up(past_len + L, LANES)
        oa = _attn_a((bseg4, B_AQ), with_past(past[0], B_AK, lk), with_past(past[1], B_AV, lk),
                     lq1, lk1, lq2, lk2, subln_w, tq=L, tk=lk, rows=1, q_off=past_len, n_valid=past_len + L,
                     lam_init=lam_init)

    hist8 = jnp.zeros((B, SUBLANES, 3 * W_B), F32)
    if not prompt:
        hist8 = hist8.at[:, SUBLANES - (CONV_W - 1):].set(past[4])
    tail = CONV_W - 1
    assert L >= tail
    new_conv = jnp.concatenate([fseg4[s, :, L - tail:] for s in range(N_BQKV)], axis=-1)
    if prompt:
        r = 128
        gb = _gdn_gates(small, a_log, dt_bias, min(512, L), L)
        s0 = jnp.zeros((B, H_B, HD_B, HD_B), F32)
        ob, s_new = _gdn(fseg4, hist8, conv_w, gb, (bseg4, B_BZ), gdn_norm_w, s0, r)
    else:
        lp = _round_up(L, CHUNK)
        gb = _gdn_gates(_pad_rows(small, lp), a_log, dt_bias, CHUNK, L)
        bz_pad = _pad_rows(bseg4[B_BZ], lp)[None]
        raw_pad = jnp.pad(fseg4, ((0, 0), (0, 0), (0, lp - L), (0, 0)))
        ob, s_new = _gdn(raw_pad, hist8, conv_w, gb, (bz_pad, 0), gdn_norm_w, past[5], CHUNK)
        ob = ob[:, :L]

    if prompt:
        oc = _attn_c((bseg4, B_CQ), (bseg4, B_CK), (bseg4, B_CV), tq=128, tk=128, q_off=0)
    else:
        past_len = past[2].shape[1]
        lk = _round_up(past_len + L, LANES)
        oc = _attn_c((bseg4, B_CQ), with_past(past[2], B_CK, lk), with_past(past[3], B_CV, lk),
                     tq=L, tk=LANES, q_off=past_len)

    tm2 = min(256, T)
    f2 = lambda a: a.reshape(T, a.shape[-1])
    x2 = _merge_mlp(x2d, f2(oa), f2(ob), f2(oc), bseg, w_out_bf, norm2_w, w_up_bf, w_down_bf, final_w, tm2, final)
    return x2.reshape(B, L, D_MODEL), caches, new_conv, s_new


N_ALIGNED = 6
SMALL_W = 2 * H_B
SMALL_BLOCK = N_ALIGNED * PROJ_TN // LANES


def _pack_src_block(n):
    return jnp.where(n < 3, n, jnp.where(n < 6, n + 4, jnp.where(n < 9, n - 3, jnp.where(n == 9, 6, n))))


def _pack_kernel(a_ref, b_ref, sm_ref, o_ref, osm_ref):
    n = pl.program_id(1)
    shifted = ((n >= 3) & (n < 6)) | (n >= 9)

    @pl.when(n == 0)
    def _():
        lane = lax.broadcasted_iota(jnp.int32, sm_ref.shape[1:], 1)
        osm_ref[0] = jnp.where(lane < SMALL_W, sm_ref[0], 0.0).astype(BF16)

    @pl.when(jnp.logical_not(shifted))
    def _():
        o_ref[0, 0] = a_ref[0].astype(BF16)

    @pl.when(shifted)
    def _():
        last = PROJ_TN - LANES
        o_ref[0, 0] = pltpu.roll(a_ref[0], PROJ_TN - SMALL_W, 1).astype(BF16)
        tail_a = pltpu.roll(a_ref[0, :, last:], LANES - SMALL_W, 1)
        tail_b = pltpu.roll(b_ref[0], LANES - SMALL_W, 1)
        lane = lax.broadcasted_iota(jnp.int32, tail_a.shape, 1)
        o_ref[0, 0, :, last:] = jnp.where(lane < LANES - SMALL_W, tail_a, tail_b).astype(BF16)


def _proj_weight_blocks(w_in):
    depth = w_in.shape[0]
    per = PROJ_TN // LANES
    return pl.pallas_call(
        _pack_kernel,
        grid=(depth, len(PROJ_GROUPS)),
        in_specs=[
            pl.BlockSpec((1, D_MODEL, PROJ_TN), lambda l, n: (l, 0, _pack_src_block(n))),
            pl.BlockSpec((1, D_MODEL, LANES), lambda l, n: (l, 0, (_pack_src_block(n) + 1) * per)),
            pl.BlockSpec((1, D_MODEL, LANES), lambda l, n: (l, 0, SMALL_BLOCK)),
        ],
        out_specs=[
            pl.BlockSpec((1, 1, D_MODEL, PROJ_TN), lambda l, n: (l, n, 0, 0)),
            pl.BlockSpec((1, D_MODEL, LANES), lambda l, n: (l, 0, 0)),
        ],
        out_shape=[jax.ShapeDtypeStruct((depth, len(PROJ_GROUPS), D_MODEL, PROJ_TN), BF16),
                   jax.ShapeDtypeStruct((depth, D_MODEL, LANES), BF16)],
        compiler_params=_cparams(("parallel", "arbitrary"), 40),
        name="pack_w_in",
    )(w_in, w_in, w_in)


def kernel(x_prompt, x_sample, cache_a_k, cache_a_v, cache_c_k, cache_c_v, state_b_conv, state_b_ssm,
           norm1_w, w_in, lam_q1, lam_k1, lam_q2, lam_k2, subln_w, conv_w, a_log, dt_bias,
           gdn_norm_w, w_out, norm2_w, w_up, w_down, final_norm_w):
    depth = w_in.shape[0]
    past_len = cache_a_k.shape[2]
    xp, xs = x_prompt, x_sample
    bp, lp = x_prompt.shape[0], x_prompt.shape[1]
    bs, ls = x_sample.shape[0], x_sample.shape[1]
    p_out = [[] for _ in range(2)]
    s_out = [[] for _ in range(2)]
    p_caches = [lax.empty((depth, bp * lp) + hd, F32) for hd in CACHE_HEADS]
    s_caches = [lax.empty((depth, bs * ls) + hd, F32) for hd in CACHE_HEADS]
    w_blocks, w_small = _proj_weight_blocks(w_in)
    for l in range(depth):
        weights = (norm1_w[l], w_blocks, w_small, lam_q1[l], lam_k1[l], lam_q2[l], lam_k2[l], subln_w[l],
                   conv_w[l], a_log[l], dt_bias[l], gdn_norm_w[l], w_out[l].astype(BF16), norm2_w[l],
                   w_up[l].astype(BF16), w_down[l].astype(BF16))
        final = l == depth - 1
        xp, p_caches, *rest = _layer(xp, 0, None, weights, l, final_norm_w, final, p_caches)
        for lst, a in zip(p_out, rest):
            lst.append(a)
        past = (cache_a_k[l], cache_a_v[l], cache_c_k[l], cache_c_v[l], state_b_conv[l], state_b_ssm[l])
        xs, s_caches, *rest = _layer(xs, past_len, past, weights, l, final_norm_w, final, s_caches)
        for lst, a in zip(s_out, rest):
            lst.append(a)

    def pack(caches, lists, batch, length):
        lead = (depth, batch, length)
        return tuple(c.reshape(lead + c.shape[2:]) for c in caches) + (jnp.stack(lists[0]), jnp.stack(lists[1]))

    return (xp, xs) + pack(p_caches, p_out, bp, lp) + pack(s_caches, s_out, bs, ls)
```

```python
import functools
import math

import jax
import jax.numpy as jnp
from jax import lax
from jax.experimental import pallas as pl
from jax.experimental.pallas import tpu as pltpu

F32 = jnp.float32
BF16 = jnp.bfloat16

D_MODEL = 1024
CHUNK = 64
CHUNK_SHIFT = 6
H_A, HD_A, DV_A = 4, 128, 256
H_B, HD_B = 8, 128
H_C, HD_C = 8, 128
W_B = H_B * HD_B
CONV_W = 4
D_FF = 4 * D_MODEL
ROPE_THETA = 10000.0
EPS = 1e-6
LANES = 128
SUBLANES = 8
MIB = 1024 * 1024
LOG2E = 1.4426950408889634
EXP_UNDERFLOW = -104.0


def _cparams(sem, vmem_mib):
    return pltpu.CompilerParams(dimension_semantics=sem, vmem_limit_bytes=vmem_mib * MIB)


def _sigmoid(x):
    return 1.0 / (1.0 + jnp.exp(-x))


def _softplus_neg_abs(x):
    return jnp.log1p(jnp.exp(-jnp.abs(x)))


def _dot(a, b):
    return jnp.dot(a, b, preferred_element_type=F32)


def _split_bf16(a):
    hi = a.astype(BF16)
    return hi, (a - hi.astype(F32)).astype(BF16)


def _dot3(a, b):
    ah, al = _split_bf16(a)
    bh, bl = _split_bf16(b)
    return _dot(ah, bh) + _dot(ah, bl) + _dot(al, bh)


def _dot_nt(a, b):
    return lax.dot_general(a, b, (((1,), (1,)), ((), ())), preferred_element_type=F32)


def _dot_tn(a, b):
    return lax.dot_general(a, b, (((0,), (0,)), ((), ())), preferred_element_type=F32)


PROJ_TN = 1024
PROJ_GROUPS = ("aq", "ak", "av", "cq", "ck", "cv", "bqkv0", "bqkv1", "bqkv2", "bz", "g0", "g1", "g2")
N_BQKV = 3
B_AQ, B_AK, B_AV, B_CQ, B_CK, B_CV, B_BZ, B_G = 0, 1, 2, 3, 4, 5, 6, 7
N_BSEG = 10
CACHE_HEADS = ((2 * H_A, HD_A), (H_A, DV_A), (H_C, HD_C), (H_C, HD_C))
QA_SCALE = HD_A ** -0.5 * LOG2E
QC_SCALE = HD_C ** -0.5


def _proj_f_index(n):
    return jnp.clip(n - 6, 0, N_BQKV - 1)


def _proj_b_index(n):
    return jnp.where(n <= 5, n, jnp.where(n <= 8, 5, n - 3))


def _proj_kernel(x_ref, nw_ref, w_ref, wsm_ref, cos_ref, sin_ref, *refs):
    ak_ref, av_ref, ck_ref, cv_ref, f_ref, b_ref, sm_ref, h_scr = refs[len(CACHE_HEADS):]
    n = pl.program_id(1)

    @pl.when(n == 0)
    def _():
        x = x_ref[...]
        y = x * lax.rsqrt(jnp.mean(x * x, axis=-1, keepdims=True) + EPS) * nw_ref[...]
        hb = y.astype(BF16)
        h_scr[...] = hb
        sm_ref[...] = _dot(hb, wsm_ref[0])

    def mm():
        return _dot(h_scr[...], w_ref[0, n])

    def rope(acc):
        cos = cos_ref[...]
        sin = sin_ref[...]
        slabs = [acc[:, s * LANES:(s + 1) * LANES] for s in range(PROJ_TN // LANES)]
        return jnp.concatenate([sl * cos + pltpu.roll(sl, LANES // 2, 1) * sin for sl in slabs], axis=1)

    def cache_out(o_ref, a):
        o_ref[0] = a.reshape(o_ref.shape[1:])
        b_ref[0] = a.astype(BF16)

    @pl.when(n == 0)
    def _():
        b_ref[0] = (rope(mm()) * QA_SCALE).astype(BF16)

    @pl.when(n == 1)
    def _():
        cache_out(ak_ref, rope(mm()))

    @pl.when(n == 2)
    def _():
        cache_out(av_ref, mm())

    @pl.when(n == 3)
    def _():
        b_ref[0] = (mm() * QC_SCALE).astype(BF16)

    @pl.when(n == 4)
    def _():
        cache_out(ck_ref, mm())

    @pl.when(n == 5)
    def _():
        cache_out(cv_ref, mm())

    @pl.when((n >= 6) & (n <= 8))
    def _():
        f_ref[0] = mm()

    @pl.when(n >= 9)
    def _():
        b_ref[0] = mm().astype(BF16)


def _in_proj(x2d, norm_w, w_blocks, w_small, layer, cos_t, sin_t, caches, tm):
    T = x2d.shape[0]
    tbl_blocks = cos_t.shape[0] // tm
    n_fixed = 6
    cache_specs = [pl.BlockSpec((1, tm) + hd, lambda m, n: (layer, m, 0, 0), pipeline_mode=pl.Buffered(1))
                   for hd in CACHE_HEADS]
    outs = pl.pallas_call(
        _proj_kernel,
        grid=(T // tm, len(PROJ_GROUPS)),
        in_specs=[
            pl.BlockSpec((tm, D_MODEL), lambda m, n: (m, 0)),
            pl.BlockSpec((1, D_MODEL), lambda m, n: (0, 0)),
            pl.BlockSpec((1, len(PROJ_GROUPS), D_MODEL, PROJ_TN), lambda m, n: (layer, 0, 0, 0),
                         pipeline_mode=pl.Buffered(1)),
            pl.BlockSpec((1, D_MODEL, LANES), lambda m, n: (layer, 0, 0)),
            pl.BlockSpec((tm, LANES), lambda m, n: (m % tbl_blocks, 0)),
            pl.BlockSpec((tm, LANES), lambda m, n: (m % tbl_blocks, 0)),
        ] + [pl.BlockSpec(memory_space=pl.ANY)] * len(CACHE_HEADS),
        out_specs=cache_specs + [
            pl.BlockSpec((1, tm, PROJ_TN), lambda m, n: (_proj_f_index(n), m, 0)),
            pl.BlockSpec((1, tm, PROJ_TN), lambda m, n: (_proj_b_index(n), m, 0)),
            pl.BlockSpec((tm, LANES), lambda m, n: (m, 0)),
        ],
        out_shape=[jax.ShapeDtypeStruct(c.shape, c.dtype) for c in caches] + [
            jax.ShapeDtypeStruct((N_BQKV, T, PROJ_TN), F32),
            jax.ShapeDtypeStruct((N_BSEG, T, PROJ_TN), BF16),
            jax.ShapeDtypeStruct((T, LANES), F32)],
        input_output_aliases={n_fixed + i: i for i in range(len(CACHE_HEADS))},
        scratch_shapes=[pltpu.VMEM((tm, D_MODEL), BF16)],
        compiler_params=_cparams(("parallel", "arbitrary"), 56),
        name="in_proj",
    )(x2d, norm_w.reshape(1, D_MODEL), w_blocks, w_small, cos_t, sin_t, *caches)
    return outs[:len(CACHE_HEADS)], outs[-3], outs[-2], outs[-1]


def _attn_a_kernel(q_ref, k_ref, v_ref, lq1_ref, lk1_ref, lq2_ref, lk2_ref, sw_ref, o_ref,
                   m_scr, l_scr, acc_scr, *, tq, tk, rows, q_off, n_valid, lam_init):
    i = pl.program_id(2)
    q = q_ref[0, 0]
    first_q = q_off + i * tq
    last_q = first_q + tq - 1
    vis_first = jnp.minimum(((first_q >> CHUNK_SHIFT) + 1) << CHUNK_SHIFT, n_valid)
    vis_last = jnp.minimum(((last_q >> CHUNK_SHIFT) + 1) << CHUNK_SHIFT, n_valid)
    n_full = vis_first // tk
    n_blocks = (vis_last + tk - 1) // tk

    m_scr[...] = jnp.full(m_scr.shape, -jnp.inf, F32)
    l_scr[...] = jnp.zeros(l_scr.shape, F32)
    acc_scr[...] = jnp.zeros(acc_scr.shape, F32)
    rg = tq // rows
    streams = [(r, c) for r in range(rows) for c in range(2)]
    ss = range(len(streams))
    qs = [q[r * rg:(r + 1) * rg, c * HD_A:(c + 1) * HD_A] for r, c in streams]
    qpos = [first_q + r * rg + lax.broadcasted_iota(jnp.int32, (rg, 1), 0) for r in range(rows)]
    nch = tk // LANES
    nacc = DV_A // LANES

    def block(j, masked):
        base = pl.multiple_of(j * tk, tk)
        k = k_ref[0, 0, pl.ds(base, tk), :]
        v = v_ref[0, 0, pl.ds(base, tk), :]
        kc = [k[:, c * HD_A:(c + 1) * HD_A] for c in range(2)]
        s = [_dot_nt(qs[n], kc[streams[n][1]]) for n in ss]
        if masked:
            kpos = base + lax.broadcasted_iota(jnp.int32, (1, tk), 1)
            kchunk = kpos >> CHUNK_SHIFT
            mask = [(kchunk <= (qpos[r] >> CHUNK_SHIFT)) & (kpos < n_valid) for r in range(rows)]
            s = [jnp.where(mask[streams[n][0]], s[n], -jnp.inf) for n in ss]
        chunks = [[s[n][:, t * LANES:(t + 1) * LANES] for t in range(nch)] for n in ss]
        m_prev = [m_scr[n] for n in ss]
        m_new = [jnp.maximum(m_prev[n], jnp.max(functools.reduce(jnp.maximum, chunks[n]), axis=1, keepdims=True))
                 for n in ss]
        alpha = [jnp.exp2(m_prev[n] - m_new[n]) for n in ss]
        ps = [[jnp.exp2((ch - m_new[n]).astype(BF16)) for ch in chunks[n]] for n in ss]
        for n in ss:
            tot = functools.reduce(jnp.add, ps[n]).astype(F32)
            l_scr[n] = alpha[n] * l_scr[n] + jnp.sum(tot, axis=1, keepdims=True)
            m_scr[n] = m_new[n]
        p = [jnp.concatenate(ps[n], axis=1) for n in ss]
        pv = [_dot(p[n], v) for n in ss]
        for n in ss:
            for t in range(nacc):
                sl = slice(t * LANES, (t + 1) * LANES)
                acc_scr[n, :, sl] = alpha[n] * acc_scr[n, :, sl] + pv[n][:, sl]

    def full_body(j, carry):
        block(j, False)
        return carry

    def edge_body(j, carry):
        block(j, True)
        return carry

    def pair_body(jj, carry):
        block(2 * jj, False)
        block(2 * jj + 1, False)
        return carry

    n_pairs = n_full // 2
    lax.fori_loop(0, n_pairs, pair_body, 0)
    lax.fori_loop(2 * n_pairs, n_full, full_body, 0)
    lax.fori_loop(n_full, n_blocks, edge_body, 0)

    lam = (jnp.exp(jnp.sum(lq1_ref[...] * lk1_ref[...], axis=1, keepdims=True))
           - jnp.exp(jnp.sum(lq2_ref[...] * lk2_ref[...], axis=1, keepdims=True)) + lam_init)
    for r in range(rows):
        parts = []
        for t in range(nacc):
            sl = slice(t * LANES, (t + 1) * LANES)
            parts.append(acc_scr[2 * r, :, sl] / l_scr[2 * r] - lam * (acc_scr[2 * r + 1, :, sl] / l_scr[2 * r + 1]))
        o = jnp.concatenate(parts, axis=1)
        o = o * lax.rsqrt(jnp.mean(o * o, axis=-1, keepdims=True) + EPS) * sw_ref[...]
        o_ref[0, r * rg:(r + 1) * rg, :] = o * (1.0 - lam_init)


def _attn_a(q, k, v, lq1, lk1, lq2, lk2, subln_w, *, tq, tk, rows, q_off, n_valid, lam_init):
    (qa, qs), (ka, ks), (va, vs) = q, k, v
    _, B, Lq, _ = qa.shape
    Lk = ka.shape[2]
    vec = pl.BlockSpec((1, HD_A), lambda b, h, i: (0, 0))
    return pl.pallas_call(
        functools.partial(_attn_a_kernel, tq=tq, tk=tk, rows=rows, q_off=q_off, n_valid=n_valid,
                          lam_init=lam_init),
        grid=(B, H_A, Lq // tq),
        in_specs=[
            pl.BlockSpec((1, 1, tq, 2 * HD_A), lambda b, h, i: (qs, b, i, h)),
            pl.BlockSpec((1, 1, Lk, 2 * HD_A), lambda b, h, i: (ks, b, 0, h)),
            pl.BlockSpec((1, 1, Lk, DV_A), lambda b, h, i: (vs, b, 0, h)),
            vec, vec, vec, vec,
            pl.BlockSpec((1, DV_A), lambda b, h, i: (0, 0)),
        ],
        out_specs=pl.BlockSpec((1, tq, DV_A), lambda b, h, i: (b, i, h)),
        out_shape=jax.ShapeDtypeStruct((B, Lq, H_A * DV_A), F32),
        scratch_shapes=[pltpu.VMEM((2 * rows, tq // rows, LANES), F32),
                        pltpu.VMEM((2 * rows, tq // rows, LANES), F32),
                        pltpu.VMEM((2 * rows, tq // rows, DV_A), F32)],
        compiler_params=_cparams(("parallel", "parallel", "arbitrary"), 40),
        name="attn_a",
    )(qa, ka, va, lq1.reshape(1, HD_A), lk1.reshape(1, HD_A), lq2.reshape(1, HD_A), lk2.reshape(1, HD_A),
      subln_w.reshape(1, DV_A))


C_HEADS = 8


def _attn_c_kernel(q_ref, k_ref, v_ref, o_ref, acc_scr, run_scr, *, tq, tk, q_off):
    i = pl.program_id(2)
    q = q_ref[0, 0]
    first_q = q_off + i * tq
    qpos = first_q + lax.broadcasted_iota(jnp.int32, (tq, 1), 0)
    j_start = (q_off + (i + 1) * tq - 2) // tk
    acc_scr[...] = jnp.zeros(acc_scr.shape, F32)
    run_scr[...] = jnp.zeros(run_scr.shape, F32)
    gs = range(C_HEADS)
    sl = [slice(g * HD_C, (g + 1) * HD_C) for g in gs]

    def cum_weights(w):
        rj = lax.broadcasted_iota(jnp.int32, (w, w + tk), 0)
        cs = lax.broadcasted_iota(jnp.int32, (w, w + tk), 1)
        return jnp.where((cs >= w) | (rj > cs), 1.0, 0.0).astype(BF16)

    def pre(base, w, masked):
        cum_w = cum_weights(w)
        z = [_dot_nt(q[:, sl[g]], k_ref[0, 0, pl.ds(base, w), sl[g]]) for g in gs]
        t = [jnp.log(1.0 + jnp.exp(-jnp.abs(z[g]))) for g in gs]
        log_beta = [jnp.minimum(z[g], 0.0) - t[g] for g in gs]
        log_1m = [jnp.minimum(-z[g], 0.0) - t[g] for g in gs]
        mask = None
        if masked:
            mask = (base + lax.broadcasted_iota(jnp.int32, (1, w), 1)) < qpos
            log_1m = [jnp.where(mask, log_1m[g], 0.0) for g in gs]
        parts = [_split_bf16(log_1m[g]) for g in gs]
        cum = [_dot(parts[g][0], cum_w) + _dot(parts[g][1], cum_w) for g in gs]
        return base, w, mask, log_beta, cum

    def post(state):
        base, w, mask, log_beta, cum = state
        run = [run_scr[g] for g in gs]
        run_w = [jnp.concatenate([run[g]] * (w // tk), axis=1) for g in gs]
        a = [jnp.exp(log_beta[g] + cum[g][:, :w] + run_w[g]) for g in gs]
        if mask is not None:
            a = [jnp.where(mask, a[g], 0.0) for g in gs]
        new_run = [run[g] + cum[g][:, w:] for g in gs]
        for g in gs:
            acc_scr[g] += _dot(a[g].astype(BF16), v_ref[0, 0, pl.ds(base, w), sl[g]])
            run_scr[g] = new_run[g]
        top = functools.reduce(jnp.maximum, new_run)
        return (jnp.max(top) > EXP_UNDERFLOW).astype(jnp.int32)

    def block(base, w, masked):
        return post(pre(base, w, masked))

    wide = 2 * tk
    end0 = (j_start + 1) * tk

    def fused_start(_):
        base_e = pl.multiple_of(end0 - tk, tk)
        base_w = pl.multiple_of(end0 - tk - wide, tk)
        edge = pre(base_e, tk, True)
        left = pre(base_w, wide, False)
        post(edge)
        return base_w, post(left)

    def edge_body(c):
        base = pl.multiple_of(c[0] - tk, tk)
        return base, block(base, tk, True)

    def wide_body(c):
        base = pl.multiple_of(c[0] - wide, tk)
        return base, block(base, wide, False)

    def last_body(c):
        base = pl.multiple_of(c[0] - tk, tk)
        return base, block(base, tk, False)

    one_edge_then_wide = (end0 - tk <= first_q) & (end0 - tk >= wide)
    c = lax.cond(one_edge_then_wide, fused_start, lambda _: (end0, jnp.int32(1)), 0)
    c = lax.while_loop(lambda c: (c[0] > 0) & (c[1] > 0) & (c[0] > first_q), edge_body, c)
    c = lax.while_loop(lambda c: (c[0] >= wide) & (c[1] > 0), wide_body, c)
    lax.while_loop(lambda c: (c[0] > 0) & (c[1] > 0), last_body, c)
    for g in gs:
        o_ref[0, :, sl[g]] = acc_scr[g]


def _attn_c(q, k, v, *, tq, tk, q_off):
    (qa, qs), (ka, ks), (va, vs) = q, k, v
    _, B, Lq, _ = qa.shape
    Lk = ka.shape[2]
    w = C_HEADS * HD_C
    assert tk == LANES
    return pl.pallas_call(
        functools.partial(_attn_c_kernel, tq=tq, tk=tk, q_off=q_off),
        grid=(B, H_C // C_HEADS, Lq // tq),
        in_specs=[
            pl.BlockSpec((1, 1, tq, w), lambda b, g, i: (qs, b, i, g)),
            pl.BlockSpec((1, 1, Lk, w), lambda b, g, i: (ks, b, 0, g), pipeline_mode=pl.Buffered(1)),
            pl.BlockSpec((1, 1, Lk, w), lambda b, g, i: (vs, b, 0, g), pipeline_mode=pl.Buffered(1)),
        ],
        out_specs=pl.BlockSpec((1, tq, w), lambda b, g, i: (b, i, g)),
        out_shape=jax.ShapeDtypeStruct((B, Lq, H_C * HD_C), F32),
        scratch_shapes=[pltpu.VMEM((C_HEADS, tq, HD_C), F32), pltpu.VMEM((C_HEADS, tq, tk), F32)],
        compiler_params=_cparams(("parallel", "parallel", "arbitrary"), 48),
        name="attn_c",
    )(qa, ka, va)


def _gates_kernel(sm_ref, alog_ref, dtb_ref, o_ref, *, r, valid):
    blk = pl.program_id(1)
    x = sm_ref[0]
    y = x + dtb_ref[...]
    g = -jnp.exp(alog_ref[...]) * (jnp.maximum(y, 0.0) + _softplus_neg_abs(y))
    beta = _sigmoid(x)
    row = blk * r + lax.broadcasted_iota(jnp.int32, (r, 1), 0)
    live = row < valid
    g = jnp.where(live, g, 0.0)
    beta = jnp.where(live, beta, 0.0)
    ri = lax.broadcasted_iota(jnp.int32, (r, r), 0)
    ci = lax.broadcasted_iota(jnp.int32, (r, r), 1)
    tri = jnp.where(((ri >> CHUNK_SHIFT) == (ci >> CHUNK_SHIFT)) & (ci <= ri), 1.0, 0.0)
    gc = jnp.dot(tri, g, precision=lax.Precision.HIGHEST, preferred_element_type=F32)
    lane = lax.broadcasted_iota(jnp.int32, (r, LANES), 1)
    o_ref[0] = jnp.where(lane < H_B, gc, beta)


def _gdn_gates(small, a_log, dt_bias, r, valid):
    B, L, _ = small.shape
    pad = lambda v: jnp.pad(v.astype(F32), (0, LANES - H_B)).reshape(1, LANES)
    return pl.pallas_call(
        functools.partial(_gates_kernel, r=r, valid=valid),
        grid=(B, L // r),
        in_specs=[
            pl.BlockSpec((1, r, LANES), lambda b, i: (b, i, 0)),
            pl.BlockSpec((1, LANES), lambda b, i: (0, 0)),
            pl.BlockSpec((1, LANES), lambda b, i: (0, 0)),
        ],
        out_specs=pl.BlockSpec((1, r, LANES), lambda b, i: (b, i, 0)),
        out_shape=jax.ShapeDtypeStruct((B, L, LANES), F32),
        compiler_params=_cparams(("parallel", "parallel"), 32),
        name="gdn_gates",
    )(small, pad(a_log), pad(dt_bias))


def _gdn_kernel(raw0_ref, raw1_ref, raw2_ref, hist_ref, cw_ref, gb_ref, gbt_ref, bz_ref, nw_ref, s0_ref,
                o_ref, s_out_ref, s_scr, xs_scr, *, r):
    blk = pl.program_id(1)
    nblk = pl.num_programs(1)

    @pl.when(blk == 0)
    def _():
        s_scr[...] = s0_ref[0]
        xs_scr[0:SUBLANES, :] = hist_ref[0]

    @pl.when(blk > 0)
    def _():
        xs_scr[0:SUBLANES, :] = xs_scr[r:r + SUBLANES, :]

    for part, raw_ref in enumerate((raw0_ref, raw1_ref, raw2_ref)):
        xs_scr[SUBLANES:, part * PROJ_TN:(part + 1) * PROJ_TN] = raw_ref[0, 0]

    def conv_head(col, l2_scale):
        cols = slice(col * HD_B, (col + 1) * HD_B)
        xs = xs_scr[:, cols]
        acc = xs[SUBLANES:] * cw_ref[CONV_W - 1:CONV_W, cols]
        for t in range(CONV_W - 1):
            back = pltpu.roll(xs, CONV_W - 1 - t, 0)[SUBLANES:]
            acc = acc + back * cw_ref[t:t + 1, cols]
        y = acc * _sigmoid(acc)
        if l2_scale is None:
            return y
        return y * (lax.rsqrt(jnp.sum(y * y, axis=-1, keepdims=True) + EPS) * l2_scale)

    gb = gb_ref[0]
    gbt = gbt_ref[0]
    ri = lax.broadcasted_iota(jnp.int32, (r, r), 0)
    ci = lax.broadcasted_iota(jnp.int32, (r, r), 1)
    same = (ri >> CHUNK_SHIFT) == (ci >> CHUNK_SHIFT)
    tri = same & (ci <= ri)
    strict = same & (ci < ri)
    pair = (ri >> 1) == (ci >> 1)
    off_masks = [((ri >> (lg + 1)) == (ci >> (lg + 1))) & ((ri >> lg) != (ci >> lg))
                 for lg in range(1, CHUNK_SHIFT)]
    nw = nw_ref[...]

    hs = range(H_B)
    q = [conv_head(h, HD_B ** -0.5) for h in hs]
    k = [conv_head(H_B + h, 1.0) for h in hs]
    v = [conv_head(2 * H_B + h, None) for h in hs]
    gc_col = [gb[:, h:h + 1] for h in hs]
    beta_col = [gb[:, H_B + h:H_B + h + 1] for h in hs]
    decay = [jnp.where(tri, jnp.exp(jnp.where(tri, gc_col[h] - gbt[h:h + 1, :], 0.0)), 0.0) for h in hs]
    kbf = [k[h].astype(BF16) for h in hs]
    kb = [k[h] * beta_col[h] for h in hs]
    m = [jnp.where(strict, _dot_nt(kb[h].astype(BF16), kbf[h]) * decay[h], 0.0) for h in hs]
    x = [-jnp.where(pair, m[h], 0.0) for h in hs]
    for off_mask in off_masks:
        off = [jnp.where(off_mask, m[h], 0.0) for h in hs]
        xb = [x[h].astype(BF16) for h in hs]
        t = [off[h] + _dot(xb[h], off[h].astype(BF16)) for h in hs]
        x = [x[h] - t[h] - _dot(t[h].astype(BF16), xb[h]) for h in hs]
    egc = [jnp.exp(gc_col[h]) for h in hs]
    rhs = [jnp.concatenate([v[h] * beta_col[h], kb[h] * egc[h]], axis=1) for h in hs]
    sol = [rhs[h] + _dot(x[h].astype(BF16), rhs[h].astype(BF16)) for h in hs]
    aqk = [jnp.where(tri, _dot_nt(q[h].astype(BF16), kbf[h]) * decay[h], 0.0).astype(BF16) for h in hs]
    q_in = [(q[h] * egc[h]).astype(BF16) for h in hs]

    s = [s_scr[h] for h in hs]
    outs = [[] for _ in hs]
    for c in range(r // CHUNK):
        lo, hi = c * CHUNK, (c + 1) * CHUNK
        g_last = [gc_col[h][hi - 1:hi, :] for h in hs]
        sb = [s[h].astype(BF16) for h in hs]
        u = [sol[h][lo:hi, :HD_B] - _dot(sol[h][lo:hi, HD_B:].astype(BF16), sb[h]) for h in hs]
        ub = [u[h].astype(BF16) for h in hs]
        for h in hs:
            outs[h].append(_dot(q_in[h][lo:hi], sb[h]) + _dot(aqk[h][lo:hi, lo:hi], ub[h]))
        k_out = [(k[h][lo:hi] * jnp.exp(g_last[h] - gc_col[h][lo:hi])).astype(BF16) for h in hs]
        s = [s[h] * jnp.exp(g_last[h]) + _dot_tn(k_out[h], ub[h]) for h in hs]
    for h in hs:
        s_scr[h] = s[h]
        o = outs[h][0] if len(outs[h]) == 1 else jnp.concatenate(outs[h], axis=0)
        o = o * lax.rsqrt(jnp.mean(o * o, axis=-1, keepdims=True) + EPS) * nw
        z = bz_ref[0, 0, :, h * HD_B:(h + 1) * HD_B].astype(F32)
        o_ref[0, :, h * HD_B:(h + 1) * HD_B] = o * (z * _sigmoid(z))

    @pl.when(blk == nblk - 1)
    def _():
        s_out_ref[0] = s_scr[...]


def _gdn(raw, hist8, conv_w, gb, bz, norm_w, s0, r):
    _, B, L, _ = raw.shape
    bza, bzs = bz
    gbt = jnp.swapaxes(gb[:, :, :2 * H_B], 1, 2)
    raw_spec = lambda part: pl.BlockSpec((1, 1, r, PROJ_TN), lambda b, i: (part, b, i, 0))
    return pl.pallas_call(
        functools.partial(_gdn_kernel, r=r),
        grid=(B, L // r),
        in_specs=[
            raw_spec(0), raw_spec(1), raw_spec(2),
            pl.BlockSpec((1, SUBLANES, 3 * W_B), lambda b, i: (b, 0, 0)),
            pl.BlockSpec((CONV_W, 3 * W_B), lambda b, i: (0, 0)),
            pl.BlockSpec((1, r, LANES), lambda b, i: (b, i, 0)),
            pl.BlockSpec((1, 2 * H_B, r), lambda b, i: (b, 0, i)),
            pl.BlockSpec((1, 1, r, W_B), lambda b, i: (bzs, b, i, 0)),
            pl.BlockSpec((1, HD_B), lambda b, i: (0, 0)),
            pl.BlockSpec((1, H_B, HD_B, HD_B), lambda b, i: (b, 0, 0, 0)),
        ],
        out_specs=[
            pl.BlockSpec((1, r, W_B), lambda b, i: (b, i, 0)),
            pl.BlockSpec((1, H_B, HD_B, HD_B), lambda b, i: (b, 0, 0, 0)),
        ],
        out_shape=[jax.ShapeDtypeStruct((B, L, W_B), F32),
                   jax.ShapeDtypeStruct((B, H_B, HD_B, HD_B), F32)],
        scratch_shapes=[pltpu.VMEM((H_B, HD_B, HD_B), F32), pltpu.VMEM((r + SUBLANES, 3 * W_B), F32)],
        compiler_params=_cparams(("parallel", "arbitrary"), 40),
        name="gdn_delta",
    )(raw, raw, raw, hist8, conv_w, gb, gbt, bza, norm_w.reshape(1, HD_B), s0)


MLP_FC = 1024


def _merge_mlp_kernel(x_ref, oa_ref, ob_ref, oc_ref, g0_ref, g1_ref, g2_ref, wo_ref, n2_ref, wu_ref, wd_ref,
                      fw_ref, o_ref, *, final):
    merged = (_sigmoid(g0_ref[0].astype(F32)) * oa_ref[...]
              + _sigmoid(g1_ref[0].astype(F32)) * ob_ref[...]
              + _sigmoid(g2_ref[0].astype(F32)) * oc_ref[...])
    x1 = x_ref[...] + _dot(merged.astype(BF16), wo_ref[...])
    h = (x1 * lax.rsqrt(jnp.mean(x1 * x1, axis=-1, keepdims=True) + EPS) * n2_ref[...]).astype(BF16)
    acc = x1
    for c in range(D_FF // MLP_FC):
        up = _dot(h, wu_ref[:, c * MLP_FC:(c + 1) * MLP_FC])
        up = jnp.square(jnp.maximum(up, 0.0))
        acc = acc + _dot(up.astype(BF16), wd_ref[c * MLP_FC:(c + 1) * MLP_FC, :])
    if final:
        acc = acc * lax.rsqrt(jnp.mean(acc * acc, axis=-1, keepdims=True) + EPS) * fw_ref[...]
    o_ref[...] = acc


def _merge_mlp(x2d, oa, ob, oc, bseg, w_out_bf, norm2_w, w_up_bf, w_down_bf, final_w, tm, final):
    T = x2d.shape[0]
    row = pl.BlockSpec((tm, D_MODEL), lambda m: (m, 0))
    gate = lambda s: pl.BlockSpec((1, tm, D_MODEL), lambda m: (B_G + s, m, 0))
    const = lambda shape: pl.BlockSpec(shape, lambda m: (0, 0), pipeline_mode=pl.Buffered(1))
    return pl.pallas_call(
        functools.partial(_merge_mlp_kernel, final=final),
        grid=(T // tm,),
        in_specs=[row, row, row, row, gate(0), gate(1), gate(2),
                  const((D_MODEL, D_MODEL)), const((1, D_MODEL)),
                  const((D_MODEL, D_FF)), const((D_FF, D_MODEL)), const((1, D_MODEL))],
        out_specs=row,
        out_shape=jax.ShapeDtypeStruct((T, D_MODEL), F32),
        compiler_params=_cparams(("parallel",), 52),
        name="merge_mlp",
    )(x2d, oa, ob, oc, bseg, bseg, bseg, w_out_bf, norm2_w.reshape(1, D_MODEL), w_up_bf, w_down_bf,
      final_w.reshape(1, D_MODEL))


def _rope_tables(pos):
    half = HD_A // 2
    inv = ROPE_THETA ** (-jnp.arange(half, dtype=F32) / half)
    ang = pos.astype(F32)[:, None] * inv[None, :]
    cos, sin = jnp.cos(ang), jnp.sin(ang)
    return jnp.concatenate([cos, cos], axis=-1), jnp.concatenate([-sin, sin], axis=-1)


def _pad_rows(a, n):
    return jnp.pad(a, ((0, 0), (0, n - a.shape[1]), (0, 0)))


def _round_up(n, m):
    return -(-n // m) * m


def _layer(x, pos0, past, weights, layer_idx, final_w, final, caches):
    (norm1_w, w_blocks, w_small, lq1, lk1, lq2, lk2, subln_w, conv_w, a_log, dt_bias,
     gdn_norm_w, w_out_bf, norm2_w, w_up_bf, w_down_bf) = weights
    B, L, _ = x.shape
    T = B * L
    prompt = past is None
    lam_init = 0.8 - 0.6 * math.exp(-0.3 * layer_idx)
    x2d = x.reshape(T, D_MODEL)

    cos_t, sin_t = _rope_tables(pos0 + jnp.arange(L))
    tm = min(512, T)
    if L % tm:
        cos_t, sin_t = jnp.tile(cos_t, (B, 1)), jnp.tile(sin_t, (B, 1))
    caches, fseg, bseg, small = _in_proj(x2d, norm1_w, w_blocks, w_small, layer_idx, cos_t, sin_t, caches, tm)
    fseg4 = fseg.reshape(N_BQKV, B, L, PROJ_TN)
    bseg4 = bseg.reshape(N_BSEG, B, L, PROJ_TN)
    small = small.reshape(B, L, LANES)

    def with_past(past_arr, seg, lk):
        past_len = past_arr.shape[1]
        allk = jnp.concatenate([past_arr.reshape(B, past_len, -1).astype(BF16), bseg4[seg]], axis=1)
        return _pad_rows(allk, lk)[None], 0

    if prompt:
        oa = _attn_a((bseg4, B_AQ), (bseg4, B_AK), (bseg4, B_AV), lq1, lk1, lq2, lk2, subln_w,
                     tq=512, tk=512, rows=2, q_off=0, n_valid=L, lam_init=lam_init)
    else:
        past_len = past[0].shape[1]
        lk = _round_up(past_len + L, LANES)
        oa = _attn_a((bseg4, B_AQ), with_past(past[0], B_AK, lk), with_past(past[1], B_AV, lk),
                     lq1, lk1, lq2, lk2, subln_w, tq=L, tk=lk, rows=1, q_off=past_len, n_valid=past_len + L,
                     lam_init=lam_init)

    hist8 = jnp.zeros((B, SUBLANES, 3 * W_B), F32)
    if not prompt:
        hist8 = hist8.at[:, SUBLANES - (CONV_W - 1):].set(past[4])
    tail = CONV_W - 1
    assert L >= tail
    new_conv = jnp.concatenate([fseg4[s, :, L - tail:] for s in range(N_BQKV)], axis=-1)
    if prompt:
        r = 128
        gb = _gdn_gates(small, a_log, dt_bias, min(512, L), L)
        s0 = jnp.zeros((B, H_B, HD_B, HD_B), F32)
        ob, s_new = _gdn(fseg4, hist8, conv_w, gb, (bseg4, B_BZ), gdn_norm_w, s0, r)
    else:
        lp = _round_up(L, CHUNK)
        gb = _gdn_gates(_pad_rows(small, lp), a_log, dt_bias, CHUNK, L)
        bz_pad = _pad_rows(bseg4[B_BZ], lp)[None]
        raw_pad = jnp.pad(fseg4, ((0, 0), (0, 0), (0, lp - L), (0, 0)))
        ob, s_new = _gdn(raw_pad, hist8, conv_w, gb, (bz_pad, 0), gdn_norm_w, past[5], CHUNK)
        ob = ob[:, :L]

    if prompt:
        oc = _attn_c((bseg4, B_CQ), (bseg4, B_CK), (bseg4, B_CV), tq=128, tk=128, q_off=0)
    else:
        past_len = past[2].shape[1]
        lk = _round_up(past_len + L, LANES)
        oc = _attn_c((bseg4, B_CQ), with_past(past[2], B_CK, lk), with_past(past[3], B_CV, lk),
                     tq=L, tk=LANES, q_off=past_len)

    tm2 = min(256, T)
    f2 = lambda a: a.reshape(T, a.shape[-1])
    x2 = _merge_mlp(x2d, f2(oa), f2(ob), f2(oc), bseg, w_out_bf, norm2_w, w_up_bf, w_down_bf, final_w, tm2, final)
    return x2.reshape(B, L, D_MODEL), caches, new_conv, s_new


N_ALIGNED = 6
SMALL_W = 2 * H_B
SMALL_BLOCK = N_ALIGNED * PROJ_TN // LANES


def _pack_src_block(n):
    return jnp.where(n < 3, n, jnp.where(n < 6, n + 4, jnp.where(n < 9, n - 3, jnp.where(n == 9, 6, n))))


def _pack_kernel(a_ref, b_ref, sm_ref, o_ref, osm_ref):
    n = pl.program_id(1)
    shifted = ((n >= 3) & (n < 6)) | (n >= 9)

    @pl.when(n == 0)
    def _():
        lane = lax.broadcasted_iota(jnp.int32, sm_ref.shape[1:], 1)
        osm_ref[0] = jnp.where(lane < SMALL_W, sm_ref[0], 0.0).astype(BF16)

    @pl.when(jnp.logical_not(shifted))
    def _():
        o_ref[0, 0] = a_ref[0].astype(BF16)

    @pl.when(shifted)
    def _():
        last = PROJ_TN - LANES
        o_ref[0, 0] = pltpu.roll(a_ref[0], PROJ_TN - SMALL_W, 1).astype(BF16)
        tail_a = pltpu.roll(a_ref[0, :, last:], LANES - SMALL_W, 1)
        tail_b = pltpu.roll(b_ref[0], LANES - SMALL_W, 1)
        lane = lax.broadcasted_iota(jnp.int32, tail_a.shape, 1)
        o_ref[0, 0, :, last:] = jnp.where(lane < LANES - SMALL_W, tail_a, tail_b).astype(BF16)


def _proj_weight_blocks(w_in):
    depth = w_in.shape[0]
    per = PROJ_TN // LANES
    return pl.pallas_call(
        _pack_kernel,
        grid=(depth, len(PROJ_GROUPS)),
        in_specs=[
            pl.BlockSpec((1, D_MODEL, PROJ_TN), lambda l, n: (l, 0, _pack_src_block(n))),
            pl.BlockSpec((1, D_MODEL, LANES), lambda l, n: (l, 0, (_pack_src_block(n) + 1) * per)),
            pl.BlockSpec((1, D_MODEL, LANES), lambda l, n: (l, 0, SMALL_BLOCK)),
        ],
        out_specs=[
            pl.BlockSpec((1, 1, D_MODEL, PROJ_TN), lambda l, n: (l, n, 0, 0)),
            pl.BlockSpec((1, D_MODEL, LANES), lambda l, n: (l, 0, 0)),
        ],
        out_shape=[jax.ShapeDtypeStruct((depth, len(PROJ_GROUPS), D_MODEL, PROJ_TN), BF16),
                   jax.ShapeDtypeStruct((depth, D_MODEL, LANES), BF16)],
        compiler_params=_cparams(("parallel", "arbitrary"), 40),
        name="pack_w_in",
    )(w_in, w_in, w_in)


def kernel(x_prompt, x_sample, cache_a_k, cache_a_v, cache_c_k, cache_c_v, state_b_conv, state_b_ssm,
           norm1_w, w_in, lam_q1, lam_k1, lam_q2, lam_k2, subln_w, conv_w, a_log, dt_bias,
           gdn_norm_w, w_out, norm2_w, w_up, w_down, final_norm_w):
    depth = w_in.shape[0]
    past_len = cache_a_k.shape[2]
    xp, xs = x_prompt, x_sample
    bp, lp = x_prompt.shape[0], x_prompt.shape[1]
    bs, ls = x_sample.shape[0], x_sample.shape[1]
    p_out = [[] for _ in range(2)]
    s_out = [[] for _ in range(2)]
    p_caches = [lax.empty((depth, bp * lp) + hd, F32) for hd in CACHE_HEADS]
    s_caches = [lax.empty((depth, bs * ls) + hd, F32) for hd in CACHE_HEADS]
    w_blocks, w_small = _proj_weight_blocks(w_in)
    for l in range(depth):
        weights = (norm1_w[l], w_blocks, w_small, lam_q1[l], lam_k1[l], lam_q2[l], lam_k2[l], subln_w[l],
                   conv_w[l], a_log[l], dt_bias[l], gdn_norm_w[l], w_out[l].astype(BF16), norm2_w[l],
                   w_up[l].astype(BF16), w_down[l].astype(BF16))
        final = l == depth - 1
        xp, p_caches, *rest = _layer(xp, 0, None, weights, l, final_norm_w, final, p_caches)
        for lst, a in zip(p_out, rest):
            lst.append(a)
        past = (cache_a_k[l], cache_a_v[l], cache_c_k[l], cache_c_v[l], state_b_conv[l], state_b_ssm[l])
        xs, s_caches, *rest = _layer(xs, past_len, past, weights, l, final_norm_w, final, s_caches)
        for lst, a in zip(s_out, rest):
            lst.append(a)

    def pack(caches, lists, batch, length):
        lead = (depth, batch, length)
        return tuple(c.reshape(lead + c.shape[2:]) for c in caches) + (jnp.stack(lists[0]), jnp.stack(lists[1]))

    return (xp, xs) + pack(p_caches, p_out, bp, lp) + pack(s_caches, s_out, bs, ls)
```

```python
import functools
import math

import jax
import jax.numpy as jnp
from jax import lax
from jax.experimental import pallas as pl
from jax.experimental.pallas import tpu as pltpu

F32 = jnp.float32
BF16 = jnp.bfloat16

D_MODEL = 1024
CHUNK = 64
CHUNK_SHIFT = 6
H_A, HD_A, DV_A = 4, 128, 256
H_B, HD_B = 8, 128
H_C, HD_C = 8, 128
W_B = H_B * HD_B
CONV_W = 4
D_FF = 4 * D_MODEL
ROPE_THETA = 10000.0
EPS = 1e-6
LANES = 128
SUBLANES = 8
MIB = 1024 * 1024
LOG2E = 1.4426950408889634
EXP_UNDERFLOW = -104.0


def _cparams(sem, vmem_mib):
    return pltpu.CompilerParams(dimension_semantics=sem, vmem_limit_bytes=vmem_mib * MIB)


def _sigmoid(x):
    return 1.0 / (1.0 + jnp.exp(-x))


def _softplus_neg_abs(x):
    return jnp.log1p(jnp.exp(-jnp.abs(x)))


def _dot(a, b):
    return jnp.dot(a, b, preferred_element_type=F32)


def _split_bf16(a):
    hi = a.astype(BF16)
    return hi, (a - hi.astype(F32)).astype(BF16)


def _dot3(a, b):
    ah, al = _split_bf16(a)
    bh, bl = _split_bf16(b)
    return _dot(ah, bh) + _dot(ah, bl) + _dot(al, bh)


def _dot_nt(a, b):
    return lax.dot_general(a, b, (((1,), (1,)), ((), ())), preferred_element_type=F32)


def _dot_tn(a, b):
    return lax.dot_general(a, b, (((0,), (0,)), ((), ())), preferred_element_type=F32)


PROJ_TN = 1024
PROJ_GROUPS = ("aq", "ak", "cq", "av", "bz", "ck", "g0", "cv", "g1", "bqkv0", "g2", "bqkv1", "bqkv2")
N_BQKV = 3
B_GROUPS = tuple(g for g in PROJ_GROUPS if not g.startswith("bqkv"))
B_AQ, B_AK, B_AV, B_CQ, B_CK, B_CV, B_BZ = (B_GROUPS.index(g) for g in ("aq", "ak", "av", "cq", "ck", "cv", "bz"))
B_GATES = tuple(B_GROUPS.index(g) for g in ("g0", "g1", "g2"))
N_BSEG = len(B_GROUPS)
CACHE_HEADS = ((2 * H_A, HD_A), (H_A, DV_A), (H_C, HD_C), (H_C, HD_C))
QA_SCALE = HD_A ** -0.5 * LOG2E
QC_SCALE = HD_C ** -0.5


def _step_table(n, table):
    out = table[-1]
    for i in range(len(table) - 2, -1, -1):
        out = jnp.where(n == i, table[i], out)
    return out


def _segment_table(member):
    table, count = [], 0
    for g in PROJ_GROUPS:
        count += member(g)
        table.append(max(count - 1, 0))
    return tuple(table)


F_TABLE = _segment_table(lambda g: g.startswith("bqkv"))
B_TABLE = _segment_table(lambda g: not g.startswith("bqkv"))


def _proj_f_index(n):
    return _step_table(n, F_TABLE)


def _proj_b_index(n):
    return _step_table(n, B_TABLE)


def _is_group(n, *names):
    hit = n == PROJ_GROUPS.index(names[0])
    for name in names[1:]:
        hit = hit | (n == PROJ_GROUPS.index(name))
    return hit


def _proj_kernel(x_ref, nw_ref, w_ref, wsm_ref, cos_ref, sin_ref, *refs):
    ak_ref, av_ref, ck_ref, cv_ref, f_ref, b_ref, sm_ref, h_scr = refs[len(CACHE_HEADS):]
    n = pl.program_id(1)

    @pl.when(n == 0)
    def _():
        x = x_ref[...]
        y = x * lax.rsqrt(jnp.mean(x * x, axis=-1, keepdims=True) + EPS) * nw_ref[...]
        hb = y.astype(BF16)
        h_scr[...] = hb
        sm_ref[...] = _dot(hb, wsm_ref[0])

    def mm():
        return _dot(h_scr[...], w_ref[0, 0])

    def rope(acc):
        cos = cos_ref[...]
        sin = sin_ref[...]
        slabs = [acc[:, s * LANES:(s + 1) * LANES] for s in range(PROJ_TN // LANES)]
        return jnp.concatenate([sl * cos + pltpu.roll(sl, LANES // 2, 1) * sin for sl in slabs], axis=1)

    def cache_out(o_ref, a):
        o_ref[0] = a.reshape(o_ref.shape[1:])
        b_ref[0] = a.astype(BF16)

    @pl.when(_is_group(n, "aq"))
    def _():
        b_ref[0] = (rope(mm()) * QA_SCALE).astype(BF16)

    @pl.when(_is_group(n, "ak"))
    def _():
        cache_out(ak_ref, rope(mm()))

    @pl.when(_is_group(n, "av"))
    def _():
        cache_out(av_ref, mm())

    @pl.when(_is_group(n, "cq"))
    def _():
        b_ref[0] = (mm() * QC_SCALE).astype(BF16)

    @pl.when(_is_group(n, "ck"))
    def _():
        cache_out(ck_ref, mm())

    @pl.when(_is_group(n, "cv"))
    def _():
        cache_out(cv_ref, mm())

    @pl.when(_is_group(n, "bqkv0", "bqkv1", "bqkv2"))
    def _():
        f_ref[0] = mm()

    @pl.when(_is_group(n, "bz", "g0", "g1", "g2"))
    def _():
        b_ref[0] = mm().astype(BF16)


def _in_proj(x2d, norm_w, w_blocks, w_small, layer, cos_t, sin_t, caches, tm):
    T = x2d.shape[0]
    tbl_blocks = cos_t.shape[0] // tm
    n_fixed = 6
    cache_specs = [pl.BlockSpec((1, tm) + hd, lambda m, n: (layer, m, 0, 0), pipeline_mode=pl.Buffered(1))
                   for hd in CACHE_HEADS]
    outs = pl.pallas_call(
        _proj_kernel,
        grid=(T // tm, len(PROJ_GROUPS)),
        in_specs=[
            pl.BlockSpec((tm, D_MODEL), lambda m, n: (m, 0)),
            pl.BlockSpec((1, D_MODEL), lambda m, n: (0, 0)),
            pl.BlockSpec((1, 1, D_MODEL, PROJ_TN), lambda m, n: (layer, n, 0, 0)),
            pl.BlockSpec((1, D_MODEL, LANES), lambda m, n: (layer, 0, 0)),
            pl.BlockSpec((tm, LANES), lambda m, n: (m % tbl_blocks, 0)),
            pl.BlockSpec((tm, LANES), lambda m, n: (m % tbl_blocks, 0)),
        ] + [pl.BlockSpec(memory_space=pl.ANY)] * len(CACHE_HEADS),
        out_specs=cache_specs + [
            pl.BlockSpec((1, tm, PROJ_TN), lambda m, n: (_proj_f_index(n), m, 0)),
            pl.BlockSpec((1, tm, PROJ_TN), lambda m, n: (_proj_b_index(n), m, 0)),
            pl.BlockSpec((tm, LANES), lambda m, n: (m, 0)),
        ],
        out_shape=[jax.ShapeDtypeStruct(c.shape, c.dtype) for c in caches] + [
            jax.ShapeDtypeStruct((N_BQKV, T, PROJ_TN), F32),
            jax.ShapeDtypeStruct((N_BSEG, T, PROJ_TN), BF16),
            jax.ShapeDtypeStruct((T, LANES), F32)],
        input_output_aliases={n_fixed + i: i for i in range(len(CACHE_HEADS))},
        scratch_shapes=[pltpu.VMEM((tm, D_MODEL), BF16)],
        compiler_params=_cparams(("parallel", "arbitrary"), 52),
        name="in_proj",
    )(x2d, norm_w.reshape(1, D_MODEL), w_blocks, w_small, cos_t, sin_t, *caches)
    return outs[:len(CACHE_HEADS)], outs[-3], outs[-2], outs[-1]


def _attn_a_kernel(q_ref, k_ref, v_ref, lq1_ref, lk1_ref, lq2_ref, lk2_ref, sw_ref, o_ref,
                   m_scr, l_scr, acc_scr, *, tq, tk, rows, q_off, n_valid, lam_init):
    i = pl.program_id(2)
    q = q_ref[0, 0]
    first_q = q_off + i * tq
    last_q = first_q + tq - 1
    vis_first = jnp.minimum(((first_q >> CHUNK_SHIFT) + 1) << CHUNK_SHIFT, n_valid)
    vis_last = jnp.minimum(((last_q >> CHUNK_SHIFT) + 1) << CHUNK_SHIFT, n_valid)
    n_full = vis_first // tk
    n_blocks = (vis_last + tk - 1) // tk

    m_scr[...] = jnp.full(m_scr.shape, -jnp.inf, F32)
    l_scr[...] = jnp.zeros(l_scr.shape, F32)
    acc_scr[...] = jnp.zeros(acc_scr.shape, F32)
    rg = tq // rows
    streams = [(r, c) for r in range(rows) for c in range(2)]
    ss = range(len(streams))
    qs = [q[r * rg:(r + 1) * rg, c * HD_A:(c + 1) * HD_A] for r, c in streams]
    qpos = [first_q + r * rg + lax.broadcasted_iota(jnp.int32, (rg, 1), 0) for r in range(rows)]
    nch = tk // LANES
    nacc = DV_A // LANES

    def block(j, masked):
        base = pl.multiple_of(j * tk, tk)
        k = k_ref[0, 0, pl.ds(base, tk), :]
        v = v_ref[0, 0, pl.ds(base, tk), :]
        kc = [k[:, c * HD_A:(c + 1) * HD_A] for c in range(2)]
        s = [_dot_nt(qs[n], kc[streams[n][1]]) for n in ss]
        if masked:
            kpos = base + lax.broadcasted_iota(jnp.int32, (1, tk), 1)
            kchunk = kpos >> CHUNK_SHIFT
            mask = [(kchunk <= (qpos[r] >> CHUNK_SHIFT)) & (kpos < n_valid) for r in range(rows)]
            s = [jnp.where(mask[streams[n][0]], s[n], -jnp.inf) for n in ss]
        chunks = [[s[n][:, t * LANES:(t + 1) * LANES] for t in range(nch)] for n in ss]
        m_prev = [m_scr[n] for n in ss]
        m_new = [jnp.maximum(m_prev[n], jnp.max(functools.reduce(jnp.maximum, chunks[n]), axis=1, keepdims=True))
                 for n in ss]
        alpha = [jnp.exp2(m_prev[n] - m_new[n]) for n in ss]
        ps = [[jnp.exp2((ch - m_new[n]).astype(BF16)) for ch in chunks[n]] for n in ss]
        for n in ss:
            tot = functools.reduce(jnp.add, ps[n]).astype(F32)
            l_scr[n] = alpha[n] * l_scr[n] + jnp.sum(tot, axis=1, keepdims=True)
            m_scr[n] = m_new[n]
        p = [jnp.concatenate(ps[n], axis=1) for n in ss]
        pv = [_dot(p[n], v) for n in ss]
        for n in ss:
            for t in range(nacc):
                sl = slice(t * LANES, (t + 1) * LANES)
                acc_scr[n, :, sl] = alpha[n] * acc_scr[n, :, sl] + pv[n][:, sl]

    def full_body(j, carry):
        block(j, False)
        return carry

    def edge_body(j, carry):
        block(j, True)
        return carry

    def pair_body(jj, carry):
        block(2 * jj, False)
        block(2 * jj + 1, False)
        return carry

    n_pairs = n_full // 2
    lax.fori_loop(0, n_pairs, pair_body, 0)
    lax.fori_loop(2 * n_pairs, n_full, full_body, 0)
    lax.fori_loop(n_full, n_blocks, edge_body, 0)

    lam = (jnp.exp(jnp.sum(lq1_ref[...] * lk1_ref[...], axis=1, keepdims=True))
           - jnp.exp(jnp.sum(lq2_ref[...] * lk2_ref[...], axis=1, keepdims=True)) + lam_init)
    for r in range(rows):
        parts = []
        for t in range(nacc):
            sl = slice(t * LANES, (t + 1) * LANES)
            parts.append(acc_scr[2 * r, :, sl] / l_scr[2 * r] - lam * (acc_scr[2 * r + 1, :, sl] / l_scr[2 * r + 1]))
        o = jnp.concatenate(parts, axis=1)
        o = o * lax.rsqrt(jnp.mean(o * o, axis=-1, keepdims=True) + EPS) * sw_ref[...]
        o_ref[0, r * rg:(r + 1) * rg, :] = o * (1.0 - lam_init)


def _attn_a(q, k, v, lq1, lk1, lq2, lk2, subln_w, *, tq, tk, rows, q_off, n_valid, lam_init):
    (qa, qs), (ka, ks), (va, vs) = q, k, v
    _, B, Lq, _ = qa.shape
    Lk = ka.shape[2]
    vec = pl.BlockSpec((1, HD_A), lambda b, h, i: (0, 0))
    return pl.pallas_call(
        functools.partial(_attn_a_kernel, tq=tq, tk=tk, rows=rows, q_off=q_off, n_valid=n_valid,
                          lam_init=lam_init),
        grid=(B, H_A, Lq // tq),
        in_specs=[
            pl.BlockSpec((1, 1, tq, 2 * HD_A), lambda b, h, i: (qs, b, i, h)),
            pl.BlockSpec((1, 1, Lk, 2 * HD_A), lambda b, h, i: (ks, b, 0, h)),
            pl.BlockSpec((1, 1, Lk, DV_A), lambda b, h, i: (vs, b, 0, h)),
            vec, vec, vec, vec,
            pl.BlockSpec((1, DV_A), lambda b, h, i: (0, 0)),
        ],
        out_specs=pl.BlockSpec((1, tq, DV_A), lambda b, h, i: (b, i, h)),
        out_shape=jax.ShapeDtypeStruct((B, Lq, H_A * DV_A), F32),
        scratch_shapes=[pltpu.VMEM((2 * rows, tq // rows, LANES), F32),
                        pltpu.VMEM((2 * rows, tq // rows, LANES), F32),
                        pltpu.VMEM((2 * rows, tq // rows, DV_A), F32)],
        compiler_params=_cparams(("parallel", "parallel", "arbitrary"), 40),
        name="attn_a",
    )(qa, ka, va, lq1.reshape(1, HD_A), lk1.reshape(1, HD_A), lq2.reshape(1, HD_A), lk2.reshape(1, HD_A),
      subln_w.reshape(1, DV_A))


C_HEADS = 8


def _attn_c_kernel(q_ref, k_ref, v_ref, o_ref, acc_scr, run_scr, *, tq, tk, q_off):
    i = pl.program_id(2)
    q = q_ref[0, 0]
    first_q = q_off + i * tq
    qpos = first_q + lax.broadcasted_iota(jnp.int32, (tq, 1), 0)
    j_start = (q_off + (i + 1) * tq - 2) // tk
    acc_scr[...] = jnp.zeros(acc_scr.shape, F32)
    run_scr[...] = jnp.zeros(run_scr.shape, F32)
    gs = range(C_HEADS)
    sl = [slice(g * HD_C, (g + 1) * HD_C) for g in gs]

    def cum_weights(w):
        rj = lax.broadcasted_iota(jnp.int32, (w, w + tk), 0)
        cs = lax.broadcasted_iota(jnp.int32, (w, w + tk), 1)
        return jnp.where((cs >= w) | (rj > cs), 1.0, 0.0).astype(BF16)

    def pre(base, w, masked):
        cum_w = cum_weights(w)
        z = [_dot_nt(q[:, sl[g]], k_ref[0, 0, pl.ds(base, w), sl[g]]) for g in gs]
        t = [jnp.log(1.0 + jnp.exp(-jnp.abs(z[g]))) for g in gs]
        log_beta = [jnp.minimum(z[g], 0.0) - t[g] for g in gs]
        log_1m = [jnp.minimum(-z[g], 0.0) - t[g] for g in gs]
        mask = None
        if masked:
            mask = (base + lax.broadcasted_iota(jnp.int32, (1, w), 1)) < qpos
            log_1m = [jnp.where(mask, log_1m[g], 0.0) for g in gs]
        parts = [_split_bf16(log_1m[g]) for g in gs]
        cum = [_dot(parts[g][0], cum_w) + _dot(parts[g][1], cum_w) for g in gs]
        return base, w, mask, log_beta, cum

    def post(state):
        base, w, mask, log_beta, cum = state
        run = [run_scr[g] for g in gs]
        run_w = [jnp.concatenate([run[g]] * (w // tk), axis=1) for g in gs]
        a = [jnp.exp(log_beta[g] + cum[g][:, :w] + run_w[g]) for g in gs]
        if mask is not None:
            a = [jnp.where(mask, a[g], 0.0) for g in gs]
        new_run = [run[g] + cum[g][:, w:] for g in gs]
        for g in gs:
            acc_scr[g] += _dot(a[g].astype(BF16), v_ref[0, 0, pl.ds(base, w), sl[g]])
            run_scr[g] = new_run[g]
        top = functools.reduce(jnp.maximum, new_run)
        return (jnp.max(top) > EXP_UNDERFLOW).astype(jnp.int32)

    def block(base, w, masked):
        return post(pre(base, w, masked))

    wide = 2 * tk
    end0 = (j_start + 1) * tk

    def fused_start(_):
        base_e = pl.multiple_of(end0 - tk, tk)
        base_w = pl.multiple_of(end0 - tk - wide, tk)
        edge = pre(base_e, tk, True)
        left = pre(base_w, wide, False)
        post(edge)
        return base_w, post(left)

    def edge_body(c):
        base = pl.multiple_of(c[0] - tk, tk)
        return base, block(base, tk, True)

    def wide_body(c):
        base = pl.multiple_of(c[0] - wide, tk)
        return base, block(base, wide, False)

    def last_body(c):
        base = pl.multiple_of(c[0] - tk, tk)
        return base, block(base, tk, False)

    one_edge_then_wide = (end0 - tk <= first_q) & (end0 - tk >= wide)
    c = lax.cond(one_edge_then_wide, fused_start, lambda _: (end0, jnp.int32(1)), 0)
    c = lax.while_loop(lambda c: (c[0] > 0) & (c[1] > 0) & (c[0] > first_q), edge_body, c)
    c = lax.while_loop(lambda c: (c[0] >= wide) & (c[1] > 0), wide_body, c)
    lax.while_loop(lambda c: (c[0] > 0) & (c[1] > 0), last_body, c)
    for g in gs:
        o_ref[0, :, sl[g]] = acc_scr[g]


def _attn_c(q, k, v, *, tq, tk, q_off):
    (qa, qs), (ka, ks), (va, vs) = q, k, v
    _, B, Lq, _ = qa.shape
    Lk = ka.shape[2]
    w = C_HEADS * HD_C
    assert tk == LANES
    return pl.pallas_call(
        functools.partial(_attn_c_kernel, tq=tq, tk=tk, q_off=q_off),
        grid=(B, H_C // C_HEADS, Lq // tq),
        in_specs=[
            pl.BlockSpec((1, 1, tq, w), lambda b, g, i: (qs, b, i, g)),
            pl.BlockSpec((1, 1, Lk, w), lambda b, g, i: (ks, b, 0, g), pipeline_mode=pl.Buffered(1)),
            pl.BlockSpec((1, 1, Lk, w), lambda b, g, i: (vs, b, 0, g), pipeline_mode=pl.Buffered(1)),
        ],
        out_specs=pl.BlockSpec((1, tq, w), lambda b, g, i: (b, i, g)),
        out_shape=jax.ShapeDtypeStruct((B, Lq, H_C * HD_C), F32),
        scratch_shapes=[pltpu.VMEM((C_HEADS, tq, HD_C), F32), pltpu.VMEM((C_HEADS, tq, tk), F32)],
        compiler_params=_cparams(("parallel", "parallel", "arbitrary"), 48),
        name="attn_c",
    )(qa, ka, va)


def _gates_kernel(sm_ref, alog_ref, dtb_ref, o_ref, *, r, valid):
    blk = pl.program_id(1)
    x = sm_ref[0]
    y = x + dtb_ref[...]
    g = -jnp.exp(alog_ref[...]) * (jnp.maximum(y, 0.0) + _softplus_neg_abs(y))
    beta = _sigmoid(x)
    row = blk * r + lax.broadcasted_iota(jnp.int32, (r, 1), 0)
    live = row < valid
    g = jnp.where(live, g, 0.0)
    beta = jnp.where(live, beta, 0.0)
    ri = lax.broadcasted_iota(jnp.int32, (r, r), 0)
    ci = lax.broadcasted_iota(jnp.int32, (r, r), 1)
    tri = jnp.where(((ri >> CHUNK_SHIFT) == (ci >> CHUNK_SHIFT)) & (ci <= ri), 1.0, 0.0)
    gc = jnp.dot(tri, g, precision=lax.Precision.HIGHEST, preferred_element_type=F32)
    lane = lax.broadcasted_iota(jnp.int32, (r, LANES), 1)
    o_ref[0] = jnp.where(lane < H_B, gc, beta)


def _gdn_gates(small, a_log, dt_bias, r, valid):
    B, L, _ = small.shape
    pad = lambda v: jnp.pad(v.astype(F32), (0, LANES - H_B)).reshape(1, LANES)
    return pl.pallas_call(
        functools.partial(_gates_kernel, r=r, valid=valid),
        grid=(B, L // r),
        in_specs=[
            pl.BlockSpec((1, r, LANES), lambda b, i: (b, i, 0)),
            pl.BlockSpec((1, LANES), lambda b, i: (0, 0)),
            pl.BlockSpec((1, LANES), lambda b, i: (0, 0)),
        ],
        out_specs=pl.BlockSpec((1, r, LANES), lambda b, i: (b, i, 0)),
        out_shape=jax.ShapeDtypeStruct((B, L, LANES), F32),
        compiler_params=_cparams(("parallel", "parallel"), 32),
        name="gdn_gates",
    )(small, pad(a_log), pad(dt_bias))


def _gdn_kernel(raw0_ref, raw1_ref, raw2_ref, hist_ref, cw_ref, gb_ref, gbt_ref, bz_ref, nw_ref, s0_ref,
                o_ref, s_out_ref, s_scr, xs_scr, *, r):
    blk = pl.program_id(1)
    nblk = pl.num_programs(1)

    @pl.when(blk == 0)
    def _():
        s_scr[...] = s0_ref[0]
        xs_scr[0:SUBLANES, :] = hist_ref[0]

    @pl.when(blk > 0)
    def _():
        xs_scr[0:SUBLANES, :] = xs_scr[r:r + SUBLANES, :]

    for part, raw_ref in enumerate((raw0_ref, raw1_ref, raw2_ref)):
        xs_scr[SUBLANES:, part * PROJ_TN:(part + 1) * PROJ_TN] = raw_ref[0, 0]

    def conv_head(col, l2_scale):
        cols = slice(col * HD_B, (col + 1) * HD_B)
        xs = xs_scr[:, cols]
        acc = xs[SUBLANES:] * cw_ref[CONV_W - 1:CONV_W, cols]
        for t in range(CONV_W - 1):
            back = pltpu.roll(xs, CONV_W - 1 - t, 0)[SUBLANES:]
            acc = acc + back * cw_ref[t:t + 1, cols]
        y = acc * _sigmoid(acc)
        if l2_scale is None:
            return y
        return y * (lax.rsqrt(jnp.sum(y * y, axis=-1, keepdims=True) + EPS) * l2_scale)

    gb = gb_ref[0]
    gbt = gbt_ref[0]
    ri = lax.broadcasted_iota(jnp.int32, (r, r), 0)
    ci = lax.broadcasted_iota(jnp.int32, (r, r), 1)
    same = (ri >> CHUNK_SHIFT) == (ci >> CHUNK_SHIFT)
    tri = same & (ci <= ri)
    strict = same & (ci < ri)
    pair = (ri >> 1) == (ci >> 1)
    off_masks = [((ri >> (lg + 1)) == (ci >> (lg + 1))) & ((ri >> lg) != (ci >> lg))
                 for lg in range(1, CHUNK_SHIFT)]
    nw = nw_ref[...]

    hs = range(H_B)
    q = [conv_head(h, HD_B ** -0.5) for h in hs]
    k = [conv_head(H_B + h, 1.0) for h in hs]
    v = [conv_head(2 * H_B + h, None) for h in hs]
    gc_col = [gb[:, h:h + 1] for h in hs]
    beta_col = [gb[:, H_B + h:H_B + h + 1] for h in hs]
    decay = [jnp.where(tri, jnp.exp(jnp.where(tri, gc_col[h] - gbt[h:h + 1, :], 0.0)), 0.0) for h in hs]
    kbf = [k[h].astype(BF16) for h in hs]
    kb = [k[h] * beta_col[h] for h in hs]
    m = [jnp.where(strict, _dot_nt(kb[h].astype(BF16), kbf[h]) * decay[h], 0.0) for h in hs]
    x = [-jnp.where(pair, m[h], 0.0) for h in hs]
    for off_mask in off_masks:
        off = [jnp.where(off_mask, m[h], 0.0) for h in hs]
        xb = [x[h].astype(BF16) for h in hs]
        t = [off[h] + _dot(xb[h], off[h].astype(BF16)) for h in hs]
        x = [x[h] - t[h] - _dot(t[h].astype(BF16), xb[h]) for h in hs]
    egc = [jnp.exp(gc_col[h]) for h in hs]
    rhs = [jnp.concatenate([v[h] * beta_col[h], kb[h] * egc[h]], axis=1) for h in hs]
    sol = [rhs[h] + _dot(x[h].astype(BF16), rhs[h].astype(BF16)) for h in hs]
    aqk = [jnp.where(tri, _dot_nt(q[h].astype(BF16), kbf[h]) * decay[h], 0.0).astype(BF16) for h in hs]
    q_in = [(q[h] * egc[h]).astype(BF16) for h in hs]

    s = [s_scr[h] for h in hs]
    outs = [[] for _ in hs]
    for c in range(r // CHUNK):
        lo, hi = c * CHUNK, (c + 1) * CHUNK
        g_last = [gc_col[h][hi - 1:hi, :] for h in hs]
        sb = [s[h].astype(BF16) for h in hs]
        u = [sol[h][lo:hi, :HD_B] - _dot(sol[h][lo:hi, HD_B:].astype(BF16), sb[h]) for h in hs]
        ub = [u[h].astype(BF16) for h in hs]
        for h in hs:
            outs[h].append(_dot(q_in[h][lo:hi], sb[h]) + _dot(aqk[h][lo:hi, lo:hi], ub[h]))
        k_out = [(k[h][lo:hi] * jnp.exp(g_last[h] - gc_col[h][lo:hi])).astype(BF16) for h in hs]
        s = [s[h] * jnp.exp(g_last[h]) + _dot_tn(k_out[h], ub[h]) for h in hs]
    for h in hs:
        s_scr[h] = s[h]
        o = outs[h][0] if len(outs[h]) == 1 else jnp.concatenate(outs[h], axis=0)
        o = o * lax.rsqrt(jnp.mean(o * o, axis=-1, keepdims=True) + EPS) * nw
        z = bz_ref[0, 0, :, h * HD_B:(h + 1) * HD_B].astype(F32)
        o_ref[0, :, h * HD_B:(h + 1) * HD_B] = o * (z * _sigmoid(z))

    @pl.when(blk == nblk - 1)
    def _():
        s_out_ref[0] = s_scr[...]


def _gdn(raw, hist8, conv_w, gb, bz, norm_w, s0, r):
    _, B, L, _ = raw.shape
    bza, bzs = bz
    gbt = jnp.swapaxes(gb[:, :, :2 * H_B], 1, 2)
    raw_spec = lambda part: pl.BlockSpec((1, 1, r, PROJ_TN), lambda b, i: (part, b, i, 0))
    return pl.pallas_call(
        functools.partial(_gdn_kernel, r=r),
        grid=(B, L // r),
        in_specs=[
            raw_spec(0), raw_spec(1), raw_spec(2),
            pl.BlockSpec((1, SUBLANES, 3 * W_B), lambda b, i: (b, 0, 0)),
            pl.BlockSpec((CONV_W, 3 * W_B), lambda b, i: (0, 0)),
            pl.BlockSpec((1, r, LANES), lambda b, i: (b, i, 0)),
            pl.BlockSpec((1, 2 * H_B, r), lambda b, i: (b, 0, i)),
            pl.BlockSpec((1, 1, r, W_B), lambda b, i: (bzs, b, i, 0)),
            pl.BlockSpec((1, HD_B), lambda b, i: (0, 0)),
            pl.BlockSpec((1, H_B, HD_B, HD_B), lambda b, i: (b, 0, 0, 0)),
        ],
        out_specs=[
            pl.BlockSpec((1, r, W_B), lambda b, i: (b, i, 0)),
            pl.BlockSpec((1, H_B, HD_B, HD_B), lambda b, i: (b, 0, 0, 0)),
        ],
        out_shape=[jax.ShapeDtypeStruct((B, L, W_B), F32),
                   jax.ShapeDtypeStruct((B, H_B, HD_B, HD_B), F32)],
        scratch_shapes=[pltpu.VMEM((H_B, HD_B, HD_B), F32), pltpu.VMEM((r + SUBLANES, 3 * W_B), F32)],
        compiler_params=_cparams(("parallel", "arbitrary"), 40),
        name="gdn_delta",
    )(raw, raw, raw, hist8, conv_w, gb, gbt, bza, norm_w.reshape(1, HD_B), s0)


MLP_FC = 1024


def _merge_mlp_kernel(x_ref, oa_ref, ob_ref, oc_ref, g0_ref, g1_ref, g2_ref, wo_ref, n2_ref, wu_ref, wd_ref,
                      fw_ref, o_ref, *, final):
    merged = (_sigmoid(g0_ref[0].astype(F32)) * oa_ref[...]
              + _sigmoid(g1_ref[0].astype(F32)) * ob_ref[...]
              + _sigmoid(g2_ref[0].astype(F32)) * oc_ref[...])
    x1 = x_ref[...] + _dot(merged.astype(BF16), wo_ref[...])
    h = (x1 * lax.rsqrt(jnp.mean(x1 * x1, axis=-1, keepdims=True) + EPS) * n2_ref[...]).astype(BF16)
    acc = x1
    for c in range(D_FF // MLP_FC):
        up = _dot(h, wu_ref[:, c * MLP_FC:(c + 1) * MLP_FC])
        up = jnp.square(jnp.maximum(up, 0.0))
        acc = acc + _dot(up.astype(BF16), wd_ref[c * MLP_FC:(c + 1) * MLP_FC, :])
    if final:
        acc = acc * lax.rsqrt(jnp.mean(acc * acc, axis=-1, keepdims=True) + EPS) * fw_ref[...]
    o_ref[...] = acc


def _merge_mlp(x2d, oa, ob, oc, bseg, w_out_bf, norm2_w, w_up_bf, w_down_bf, final_w, tm, final):
    T = x2d.shape[0]
    row = pl.BlockSpec((tm, D_MODEL), lambda m: (m, 0))
    gate = lambda s: pl.BlockSpec((1, tm, D_MODEL), lambda m: (B_GATES[s], m, 0))
    const = lambda shape: pl.BlockSpec(shape, lambda m: (0, 0), pipeline_mode=pl.Buffered(1))
    return pl.pallas_call(
        functools.partial(_merge_mlp_kernel, final=final),
        grid=(T // tm,),
        in_specs=[row, row, row, row, gate(0), gate(1), gate(2),
                  const((D_MODEL, D_MODEL)), const((1, D_MODEL)),
                  const((D_MODEL, D_FF)), const((D_FF, D_MODEL)), const((1, D_MODEL))],
        out_specs=row,
        out_shape=jax.ShapeDtypeStruct((T, D_MODEL), F32),
        compiler_params=_cparams(("parallel",), 52),
        name="merge_mlp",
    )(x2d, oa, ob, oc, bseg, bseg, bseg, w_out_bf, norm2_w.reshape(1, D_MODEL), w_up_bf, w_down_bf,
      final_w.reshape(1, D_MODEL))


def _rope_tables(pos):
    half = HD_A // 2
    inv = ROPE_THETA ** (-jnp.arange(half, dtype=F32) / half)
    ang = pos.astype(F32)[:, None] * inv[None, :]
    cos, sin = jnp.cos(ang), jnp.sin(ang)
    return jnp.concatenate([cos, cos], axis=-1), jnp.concatenate([-sin, sin], axis=-1)


def _pad_rows(a, n):
    return jnp.pad(a, ((0, 0), (0, n - a.shape[1]), (0, 0)))


def _round_up(n, m):
    return -(-n // m) * m


def _layer(x, pos0, past, weights, layer_idx, final_w, final, caches):
    (norm1_w, w_blocks, w_small, lq1, lk1, lq2, lk2, subln_w, conv_w, a_log, dt_bias,
     gdn_norm_w, w_out_bf, norm2_w, w_up_bf, w_down_bf) = weights
    B, L, _ = x.shape
    T = B * L
    prompt = past is None
    lam_init = 0.8 - 0.6 * math.exp(-0.3 * layer_idx)
    x2d = x.reshape(T, D_MODEL)

    cos_t, sin_t = _rope_tables(pos0 + jnp.arange(L))
    tm = min(1024, T)
    if L % tm:
        cos_t, sin_t = jnp.tile(cos_t, (B, 1)), jnp.tile(sin_t, (B, 1))
    caches, fseg, bseg, small = _in_proj(x2d, norm1_w, w_blocks, w_small, layer_idx, cos_t, sin_t, caches, tm)
    fseg4 = fseg.reshape(N_BQKV, B, L, PROJ_TN)
    bseg4 = bseg.reshape(N_BSEG, B, L, PROJ_TN)
    small = small.reshape(B, L, LANES)

    def with_past(past_arr, seg, lk):
        past_len = past_arr.shape[1]
        allk = jnp.concatenate([past_arr.reshape(B, past_len, -1).astype(BF16), bseg4[seg]], axis=1)
        return _pad_rows(allk, lk)[None], 0

    if prompt:
        oa = _attn_a((bseg4, B_AQ), (bseg4, B_AK), (bseg4, B_AV), lq1, lk1, lq2, lk2, subln_w,
                     tq=512, tk=512, rows=2, q_off=0, n_valid=L, lam_init=lam_init)
    else:
        past_len = past[0].shape[1]
        lk = _round_up(past_len + L, LANES)
        oa = _attn_a((bseg4, B_AQ), with_past(past[0], B_AK, lk), with_past(past[1], B_AV, lk),
                     lq1, lk1, lq2, lk2, subln_w, tq=L, tk=lk, rows=1, q_off=past_len, n_valid=past_len + L,
                     lam_init=lam_init)

    hist8 = jnp.zeros((B, SUBLANES, 3 * W_B), F32)
    if not prompt:
        hist8 = hist8.at[:, SUBLANES - (CONV_W - 1):].set(past[4])
    tail = CONV_W - 1
    assert L >= tail
    new_conv = jnp.concatenate([fseg4[s, :, L - tail:] for s in range(N_BQKV)], axis=-1)
    if prompt:
        r = 128
        gb = _gdn_gates(small, a_log, dt_bias, min(512, L), L)
        s0 = jnp.zeros((B, H_B, HD_B, HD_B), F32)
        ob, s_new = _gdn(fseg4, hist8, conv_w, gb, (bseg4, B_BZ), gdn_norm_w, s0, r)
    else:
        lp = _round_up(L, CHUNK)
        gb = _gdn_gates(_pad_rows(small, lp), a_log, dt_bias, CHUNK, L)
        bz_pad = _pad_rows(bseg4[B_BZ], lp)[None]
        raw_pad = jnp.pad(fseg4, ((0, 0), (0, 0), (0, lp - L), (0, 0)))
        ob, s_new = _gdn(raw_pad, hist8, conv_w, gb, (bz_pad, 0), gdn_norm_w, past[5], CHUNK)
        ob = ob[:, :L]

    if prompt:
        oc = _attn_c((bseg4, B_CQ), (bseg4, B_CK), (bseg4, B_CV), tq=128, tk=128, q_off=0)
    else:
        past_len = past[2].shape[1]
        lk = _round_up(past_len + L, LANES)
        oc = _attn_c((bseg4, B_CQ), with_past(past[2], B_CK, lk), with_past(past[3], B_CV, lk),
                     tq=L, tk=LANES, q_off=past_len)

    tm2 = min(256, T)
    f2 = lambda a: a.reshape(T, a.shape[-1])
    x2 = _merge_mlp(x2d, f2(oa), f2(ob), f2(oc), bseg, w_out_bf, norm2_w, w_up_bf, w_down_bf, final_w, tm2, final)
    return x2.reshape(B, L, D_MODEL), caches, new_conv, s_new


N_ALIGNED = 6
SMALL_W = 2 * H_B
SMALL_BLOCK = N_ALIGNED * PROJ_TN // LANES
W_IN_ORDER = ("aq", "ak", "av", "bqkv0", "bqkv1", "bqkv2", "bz", "cq", "ck", "cv", "g0", "g1", "g2")
PACK_SRC = tuple(W_IN_ORDER.index(g) for g in PROJ_GROUPS)


def _pack_src_block(n):
    return _step_table(n, PACK_SRC)


def _pack_kernel(a_ref, b_ref, sm_ref, o_ref, osm_ref):
    n = pl.program_id(1)
    shifted = _pack_src_block(n) >= N_ALIGNED

    @pl.when(n == 0)
    def _():
        lane = lax.broadcasted_iota(jnp.int32, sm_ref.shape[1:], 1)
        osm_ref[0] = jnp.where(lane < SMALL_W, sm_ref[0], 0.0).astype(BF16)

    @pl.when(jnp.logical_not(shifted))
    def _():
        o_ref[0, 0] = a_ref[0].astype(BF16)

    @pl.when(shifted)
    def _():
        last = PROJ_TN - LANES
        o_ref[0, 0] = pltpu.roll(a_ref[0], PROJ_TN - SMALL_W, 1).astype(BF16)
        tail_a = pltpu.roll(a_ref[0, :, last:], LANES - SMALL_W, 1)
        tail_b = pltpu.roll(b_ref[0], LANES - SMALL_W, 1)
        lane = lax.broadcasted_iota(jnp.int32, tail_a.shape, 1)
        o_ref[0, 0, :, last:] = jnp.where(lane < LANES - SMALL_W, tail_a, tail_b).astype(BF16)


def _proj_weight_blocks(w_in):
    depth = w_in.shape[0]
    per = PROJ_TN // LANES
    return pl.pallas_call(
        _pack_kernel,
        grid=(depth, len(PROJ_GROUPS)),
        in_specs=[
            pl.BlockSpec((1, D_MODEL, PROJ_TN), lambda l, n: (l, 0, _pack_src_block(n))),
            pl.BlockSpec((1, D_MODEL, LANES), lambda l, n: (l, 0, (_pack_src_block(n) + 1) * per)),
            pl.BlockSpec((1, D_MODEL, LANES), lambda l, n: (l, 0, SMALL_BLOCK)),
        ],
        out_specs=[
            pl.BlockSpec((1, 1, D_MODEL, PROJ_TN), lambda l, n: (l, n, 0, 0)),
            pl.BlockSpec((1, D_MODEL, LANES), lambda l, n: (l, 0, 0)),
        ],
        out_shape=[jax.ShapeDtypeStruct((depth, len(PROJ_GROUPS), D_MODEL, PROJ_TN), BF16),
                   jax.ShapeDtypeStruct((depth, D_MODEL, LANES), BF16)],
        compiler_params=_cparams(("parallel", "arbitrary"), 40),
        name="pack_w_in",
    )(w_in, w_in, w_in)


def kernel(x_prompt, x_sample, cache_a_k, cache_a_v, cache_c_k, cache_c_v, state_b_conv, state_b_ssm,
           norm1_w, w_in, lam_q1, lam_k1, lam_q2, lam_k2, subln_w, conv_w, a_log, dt_bias,
           gdn_norm_w, w_out, norm2_w, w_up, w_down, final_norm_w):
    depth = w_in.shape[0]
    past_len = cache_a_k.shape[2]
    xp, xs = x_prompt, x_sample
    bp, lp = x_prompt.shape[0], x_prompt.shape[1]
    bs, ls = x_sample.shape[0], x_sample.shape[1]
    p_out = [[] for _ in range(2)]
    s_out = [[] for _ in range(2)]
    p_caches = [lax.empty((depth, bp * lp) + hd, F32) for hd in CACHE_HEADS]
    s_caches = [lax.empty((depth, bs * ls) + hd, F32) for hd in CACHE_HEADS]
    w_blocks, w_small = _proj_weight_blocks(w_in)
    for l in range(depth):
        weights = (norm1_w[l], w_blocks, w_small, lam_q1[l], lam_k1[l], lam_q2[l], lam_k2[l], subln_w[l],
                   conv_w[l], a_log[l], dt_bias[l], gdn_norm_w[l], w_out[l].astype(BF16), norm2_w[l],
                   w_up[l].astype(BF16), w_down[l].astype(BF16))
        final = l == depth - 1
        xp, p_caches, *rest = _layer(xp, 0, None, weights, l, final_norm_w, final, p_caches)
        for lst, a in zip(p_out, rest):
            lst.append(a)
        past = (cache_a_k[l], cache_a_v[l], cache_c_k[l], cache_c_v[l], state_b_conv[l], state_b_ssm[l])
        xs, s_caches, *rest = _layer(xs, past_len, past, weights, l, final_norm_w, final, s_caches)
        for lst, a in zip(s_out, rest):
            lst.append(a)

    def pack(caches, lists, batch, length):
        lead = (depth, batch, length)
        return tuple(c.reshape(lead + c.shape[2:]) for c in caches) + (jnp.stack(lists[0]), jnp.stack(lists[1]))

    return (xp, xs) + pack(p_caches, p_out, bp, lp) + pack(s_caches, s_out, bs, ls)
```
